```python
import jax, jax.numpy as jnp
from jax import lax
import numpy as np

D_MODEL = 2048
BATCH = 1
SEQ = 8192
DEPTH = 2
DEC_BATCH = 16
DEC_SEQ = 16
PAST_LEN = 4096

CHUNK = 64
N_META = 16
CONV_DIM = D_MODEL // 2
CONV_W = 3
HEAD_DIM = 128
ATTN_DIM = D_MODEL // 2
N_HEADS = ATTN_DIM // HEAD_DIM
D_FF = 2 * D_MODEL
FFN_CONV_W = 3
Q_BLOCK = 128
EPS = 1e-6
ATTN_SCALE = HEAD_DIM ** -0.5
NEG_INF = -1e30
SPLIT_SIZES = (CONV_DIM, CONV_DIM, CONV_DIM, ATTN_DIM, ATTN_DIM, ATTN_DIM, N_HEADS, D_MODEL, D_MODEL)
PROJ_DIM = 3 * CONV_DIM + 3 * ATTN_DIM + N_HEADS + 2 * D_MODEL
F_OFF = 3 * CONV_DIM + 3 * ATTN_DIM

kernel_name = 'hybrid_gatedconv_fox_convffn_stream_step'


def rmsnorm(x, g):
    xf = x.astype(jnp.float32)
    y = xf * lax.rsqrt(jnp.mean(xf * xf, axis=-1, keepdims=True) + EPS)
    return (y * g.astype(jnp.float32)).astype(x.dtype)


def causal_dwconv(u, hist, w):
    L = u.shape[1]
    width = w.shape[0]
    full = jnp.concatenate([hist.astype(u.dtype), u], axis=1)
    y = full[:, 0:L] * w[0]
    for j in range(1, width):
        y = y + full[:, j:j + L] * w[j]
    return y, full[:, full.shape[1] - (width - 1):]


def split_proj(z):
    points = np.cumsum(np.array(SPLIT_SIZES))[:-1].tolist()
    return jnp.split(z, points, axis=-1)


def fox_prompt(q, k, v, logf):
    B, L, H, Dh = q.shape
    nb = -(-L // Q_BLOCK)
    Lp = nb * Q_BLOCK
    pad = Lp - L
    pad4 = ((0, 0), (0, pad), (0, 0), (0, 0))
    qp = jnp.pad(q, pad4)
    kp = jnp.pad(k, pad4)
    vp = jnp.pad(v, pad4)
    c = jnp.pad(jnp.cumsum(logf, axis=1), ((0, 0), (0, pad), (0, 0)))
    c_k = c.transpose(0, 2, 1)
    q_blocks = qp.reshape(B, nb, Q_BLOCK, H, Dh).swapaxes(0, 1)
    c_blocks = c.reshape(B, nb, Q_BLOCK, H).swapaxes(0, 1)
    k_pos = jnp.arange(Lp)

    def one_block(args):
        blk, qi, ci = args
        s = jnp.einsum('bqhd,bkhd->bhqk', qi, kp, preferred_element_type=jnp.float32) * ATTN_SCALE
        s = s + ci.astype(jnp.float32).transpose(0, 2, 1)[..., None] - c_k[:, :, None, :]
        q_pos = blk * Q_BLOCK + jnp.arange(Q_BLOCK)
        s = jnp.where(k_pos[None, :] <= q_pos[:, None], s, NEG_INF)
        p = jax.nn.softmax(s, axis=-1)
        return jnp.einsum('bhqk,bkhd->bqhd', p.astype(vp.dtype), vp)

    out = lax.map(one_block, (jnp.arange(nb), q_blocks, c_blocks))
    return out.swapaxes(0, 1).reshape(B, Lp, H, Dh)[:, :L]


def fox_sample(q, k, v, logf, cache_k, cache_v, cache_logf):
    T = q.shape[1]
    P = cache_k.shape[1]
    lc = cache_logf.astype(jnp.float32)
    suffix = lax.cumsum(lc, axis=1, reverse=True) - lc
    cn = jnp.cumsum(logf, axis=1)
    keys = jnp.concatenate([cache_k.astype(k.dtype), k], axis=1)
    vals = jnp.concatenate([cache_v.astype(v.dtype), v], axis=1)
    k_bias = jnp.concatenate([suffix, -cn], axis=1)
    s = jnp.einsum('bthd,bshd->bhts', q, keys, preferred_element_type=jnp.float32) * ATTN_SCALE
    s = s + cn.transpose(0, 2, 1)[..., None] + k_bias.transpose(0, 2, 1)[:, :, None, :]
    new_mask = jnp.arange(T)[None, :] <= jnp.arange(T)[:, None]
    mask = jnp.concatenate([jnp.ones((T, P), dtype=bool), new_mask], axis=1)
    s = jnp.where(mask, s, NEG_INF)
    p = jax.nn.softmax(s, axis=-1)
    return jnp.einsum('bhts,bshd->bthd', p.astype(vals.dtype), vals)


def layer_forward(x, conv_hist, ffn_hist, attn, g_mix, w_in, b_in, w_conv, w_pc, w_pa, w_o,
                  g_ffn, w_up, w_fconv, b_fconv, w_down):
    B, L, _ = x.shape
    h = rmsnorm(x, g_mix)
    z = h @ w_in + b_in
    xc, bc, cc, q, k, v, fl, gc, ga = split_proj(z)
    yc, conv_new = causal_dwconv(cc * xc, conv_hist, w_conv)
    yc = bc * yc
    q = q.reshape(B, L, N_HEADS, HEAD_DIM)
    k = k.reshape(B, L, N_HEADS, HEAD_DIM)
    v = v.reshape(B, L, N_HEADS, HEAD_DIM)
    logf = jax.nn.log_sigmoid(fl.astype(jnp.float32))
    ya = attn(q, k, v, logf).reshape(B, L, ATTN_DIM)
    merged = jax.nn.sigmoid(gc) * (yc @ w_pc) + jax.nn.sigmoid(ga) * (ya @ w_pa)
    x = x + merged @ w_o
    u = rmsnorm(x, g_ffn) @ w_up
    u, ffn_new = causal_dwconv(u, ffn_hist, w_fconv)
    a, b = jnp.split(u + b_fconv, 2, axis=-1)
    x = x + (jax.nn.silu(a) * b) @ w_down
    return x, conv_new, ffn_new, k, v, logf


def setup_inputs(seed: int = 0) -> dict:
    key = jax.random.key(seed)
    ks = jax.random.split(key, 24)
    f32 = jnp.float32
    nrm = lambda kk, shape, scale: jax.random.normal(kk, shape, f32) * scale
    b_in = nrm(ks[0], (DEPTH, PROJ_DIM), 0.02)
    b_f = jax.random.uniform(ks[1], (DEPTH, N_HEADS), f32, 1.0, 4.0)
    b_in = b_in.at[:, F_OFF:F_OFF + N_HEADS].set(b_f)
    cache_logf = jax.nn.log_sigmoid(nrm(ks[2], (DEPTH, DEC_BATCH, PAST_LEN, N_HEADS), 1.0) + 2.5)
    return {
        'x_prompt': nrm(ks[3], (BATCH, SEQ, D_MODEL), 1.0),
        'x_sample': nrm(ks[4], (DEC_BATCH, DEC_SEQ, D_MODEL), 1.0),
        'cache_k': nrm(ks[5], (DEPTH, DEC_BATCH, PAST_LEN, N_HEADS, HEAD_DIM), 1.0),
        'cache_v': nrm(ks[6], (DEPTH, DEC_BATCH, PAST_LEN, N_HEADS, HEAD_DIM), 1.0),
        'cache_logf': cache_logf,
        'state_conv': nrm(ks[7], (DEPTH, DEC_BATCH, CONV_W - 1, CONV_DIM), 1.0),
        'state_ffn_conv': nrm(ks[8], (DEPTH, DEC_BATCH, FFN_CONV_W - 1, 2 * D_FF), 1.0),
        'meta_tokens': nrm(ks[9], (N_META, D_MODEL), 1.0),
        'g_mix': 1.0 + nrm(ks[10], (DEPTH, D_MODEL), 0.02),
        'w_in': nrm(ks[11], (DEPTH, D_MODEL, PROJ_DIM), D_MODEL ** -0.5),
        'b_in': b_in,
        'w_conv': nrm(ks[12], (DEPTH, CONV_W, CONV_DIM), 0.5),
        'w_pc': nrm(ks[13], (DEPTH, CONV_DIM, D_MODEL), CONV_DIM ** -0.5),
        'w_pa': nrm(ks[14], (DEPTH, ATTN_DIM, D_MODEL), ATTN_DIM ** -0.5),
        'w_o': nrm(ks[15], (DEPTH, D_MODEL, D_MODEL), D_MODEL ** -0.5),
        'g_ffn': 1.0 + nrm(ks[16], (DEPTH, D_MODEL), 0.02),
        'w_up': nrm(ks[17], (DEPTH, D_MODEL, 2 * D_FF), D_MODEL ** -0.5),
        'w_fconv': nrm(ks[18], (DEPTH, FFN_CONV_W, 2 * D_FF), 0.5),
        'b_fconv': nrm(ks[19], (DEPTH, 2 * D_FF), 0.02),
        'w_down': nrm(ks[20], (DEPTH, D_FF, D_MODEL), D_FF ** -0.5),
        'g_final': 1.0 + nrm(ks[21], (D_MODEL,), 0.02),
    }


def reference(x_prompt, x_sample, cache_k, cache_v, cache_logf, state_conv, state_ffn_conv,
              meta_tokens, g_mix, w_in, b_in, w_conv, w_pc, w_pa, w_o, g_ffn, w_up, w_fconv,
              b_fconv, w_down, g_final):
    B = x_prompt.shape[0]
    meta = jnp.broadcast_to(meta_tokens[None].astype(x_prompt.dtype), (B, N_META, D_MODEL))
    xp = jnp.concatenate([meta, x_prompt], axis=1)
    xs = x_sample
    zeros_conv = jnp.zeros((B, CONV_W - 1, CONV_DIM), x_prompt.dtype)
    zeros_ffn = jnp.zeros((B, FFN_CONV_W - 1, 2 * D_FF), x_prompt.dtype)
    kp_l, vp_l, fp_l, cp_l, ffp_l = [], [], [], [], []
    ks_l, vs_l, fs_l, cs_l, ffs_l = [], [], [], [], []
    for i in range(DEPTH):
        wts = (g_mix[i], w_in[i], b_in[i], w_conv[i], w_pc[i], w_pa[i], w_o[i],
               g_ffn[i], w_up[i], w_fconv[i], b_fconv[i], w_down[i])
        xp, c_new, f_new, k_new, v_new, lf_new = layer_forward(
            xp, zeros_conv, zeros_ffn, fox_prompt, *wts)
        kp_l.append(k_new); vp_l.append(v_new); fp_l.append(lf_new)
        cp_l.append(c_new); ffp_l.append(f_new)
        ck, cv, cl = cache_k[i], cache_v[i], cache_logf[i]
        attn_s = lambda q, k, v, lf, ck=ck, cv=cv, cl=cl: fox_sample(q, k, v, lf, ck, cv, cl)
        xs, c_new, f_new, k_new, v_new, lf_new = layer_forward(
            xs, state_conv[i], state_ffn_conv[i], attn_s, *wts)
        ks_l.append(k_new); vs_l.append(v_new); fs_l.append(lf_new)
        cs_l.append(c_new); ffs_l.append(f_new)
    y_prompt = rmsnorm(xp, g_final)[:, N_META:]
    y_sample = rmsnorm(xs, g_final)
    return (y_prompt, y_sample,
            jnp.stack(kp_l), jnp.stack(vp_l), jnp.stack(fp_l), jnp.stack(cp_l), jnp.stack(ffp_l),
            jnp.stack(ks_l), jnp.stack(vs_l), jnp.stack(fs_l), jnp.stack(cs_l), jnp.stack(ffs_l))
```

```python
import functools

import jax
import jax.numpy as jnp
from jax import lax
from jax.experimental import pallas as pl
from jax.experimental.pallas import tpu as pltpu

F32 = jnp.float32
BF16 = jnp.bfloat16

D_MODEL = 2048
N_META = 16
CONV_DIM = D_MODEL // 2
HEAD_DIM = 128
ATTN_DIM = D_MODEL // 2
N_HEADS = ATTN_DIM // HEAD_DIM
D_FF = 2 * D_MODEL
EPS = 1e-6
ATTN_SCALE = HEAD_DIM ** -0.5
NEG_INF = -1e30
F_OFF = 3 * CONV_DIM + 3 * ATTN_DIM
MAIN_DIM = F_OFF + 2 * D_MODEL

LANES = 128
VMEM_LIMIT = 60000 * 1024

Q_OFF = 3 * CONV_DIM
K_OFF = Q_OFF + ATTN_DIM
V_OFF = K_OFF + ATTN_DIM
G_OFF = V_OFF + ATTN_DIM

TN_IN = 512
TF = 512
TM_BIG = 912
TM_SMALL = 432
KV_CHUNK = 1024
SUFFIX_BLOCK = 256


def _params(n_axes):
    return pltpu.CompilerParams(dimension_semantics=("arbitrary",) * n_axes,
                                vmem_limit_bytes=VMEM_LIMIT)


def _dot(a, b):
    return jnp.dot(a, b, preferred_element_type=F32)


def _dot_nt(a, b):
    return lax.dot_general(a, b, (((1,), (1,)), ((), ())), preferred_element_type=F32)


def _dot_exact01(mask_bf16, x, nt=False):
    hi = x.astype(BF16)
    r1 = x - hi.astype(F32)
    mid = r1.astype(BF16)
    lo = (r1 - mid.astype(F32)).astype(BF16)
    if nt:
        return _dot(hi, mask_bf16) + _dot(mid, mask_bf16) + _dot(lo, mask_bf16)
    return _dot(mask_bf16, hi) + _dot(mask_bf16, mid) + _dot(mask_bf16, lo)


def _rmsnorm(x, g):
    ms = jnp.mean(x * x, axis=-1, keepdims=True)
    return (x * lax.rsqrt(ms + EPS)) * g


def _sigmoid(x):
    return 1.0 / (1.0 + jnp.exp(-x))


def _conv3(u, h0rows, h1rows, w, seg):
    tm = u.shape[0]
    r = lax.broadcasted_iota(jnp.int32, (tm, 1), 0)
    if seg < tm:
        r = r % seg
    um1 = jnp.where(r == 0, h1rows, pltpu.roll(u, 1, 0))
    um2 = jnp.where(r == 0, h0rows, jnp.where(r == 1, h1rows, pltpu.roll(u, 2, 0)))
    return um2 * w[0:1] + um1 * w[1:2] + u * w[2:3]


def _seg_rows(h, seg):
    nseg, c = h.shape
    return jnp.broadcast_to(h[:, None, :], (nseg, seg, c)).reshape(nseg * seg, c)


def _inproj_body(*refs, seg, use_carry, n_alias):
    x_ref, g_ref, w_ref, b_ref, wf_ref, bf_ref = refs[:6]
    (z_ref, k_ref, v_ref, lf_ref, ccol_ref, crow_ref, h_scr, carry_scr) = refs[6 + n_alias:]
    i = pl.program_id(0)
    j = pl.program_id(1)
    tm = x_ref.shape[0]
    jq, jk, jv, jg = (Q_OFF // TN_IN, K_OFF // TN_IN, V_OFF // TN_IN, G_OFF // TN_IN)

    @pl.when(j == 0)
    def _():
        hb = _rmsnorm(x_ref[...], g_ref[...]).astype(BF16)
        h_scr[...] = hb
        fl = _dot(hb, wf_ref[...]) + bf_ref[...]
        lf = jnp.minimum(fl, 0.0) - jnp.log1p(jnp.exp(-jnp.abs(fl)))
        lf_ref[...] = lf[:, :N_HEADS]
        r = lax.broadcasted_iota(jnp.int32, (tm, tm), 0)
        c = lax.broadcasted_iota(jnp.int32, (tm, tm), 1)
        tri = c <= r
        if seg < tm:
            tri = tri & ((r // seg) == (c // seg))
        cum = _dot_exact01(jnp.where(tri, 1.0, 0.0).astype(BF16), lf)
        if use_carry:
            @pl.when(i == 0)
            def _():
                carry_scr[...] = jnp.zeros_like(carry_scr)
            cum = cum + carry_scr[...]
            carry_scr[...] = cum[tm - 1:tm, :]
        ccol_ref[...] = cum
        crow_ref[...] = cum.T[:N_HEADS, :]

    acc = _dot(h_scr[...], w_ref[...]) + b_ref[...]
    is_q = (j >= jq) & (j < jk)
    is_gate = j >= jg

    @pl.when(is_q)
    def _():
        z_ref[...] = (acc * ATTN_SCALE).astype(BF16)

    @pl.when(is_gate)
    def _():
        z_ref[...] = _sigmoid(acc).astype(BF16)

    @pl.when(jnp.logical_not(is_q | is_gate))
    def _():
        z_ref[...] = acc.astype(BF16)

    @pl.when((j >= jk) & (j < jv))
    def _():
        k_ref[...] = acc

    @pl.when((j >= jv) & (j < jg))
    def _():
        v_ref[...] = acc


def _inproj(x, g, w_main, b_main, w_f, b_f, layer, n_valid, tm, seg, use_carry, kv_prev):
    rows = x.shape[0]
    nm = rows // tm
    nj = MAIN_DIM // TN_IN
    jk, jv = K_OFF // TN_IN, V_OFF // TN_IN
    nkv = ATTN_DIM // TN_IN
    depth = w_main.shape[0]
    n_alias = 0 if kv_prev is None else 2
    in_specs = [
        pl.BlockSpec((tm, D_MODEL), lambda i, j: (i, 0)),
        pl.BlockSpec((None, 1, D_MODEL), lambda i, j: (layer, 0, 0)),
        pl.BlockSpec((None, D_MODEL, TN_IN), lambda i, j: (layer, 0, j)),
        pl.BlockSpec((None, 1, TN_IN), lambda i, j: (layer, 0, j)),
        pl.BlockSpec((None, D_MODEL, LANES), lambda i, j: (layer, 0, 0)),
        pl.BlockSpec((None, 1, LANES), lambda i, j: (layer, 0, 0)),
    ]
    args = [x, g, w_main, b_main, w_f, b_f]
    aliases = {}
    if kv_prev is not None:
        in_specs += [pl.BlockSpec(memory_space=pl.ANY)] * 2
        args += list(kv_prev)
        aliases = {6: 1, 7: 2}
    out_shape = (
        jax.ShapeDtypeStruct((rows, MAIN_DIM), BF16),
        jax.ShapeDtypeStruct((depth, n_valid, ATTN_DIM), F32),
        jax.ShapeDtypeStruct((depth, n_valid, ATTN_DIM), F32),
        jax.ShapeDtypeStruct((n_valid, N_HEADS), F32),
        jax.ShapeDtypeStruct((rows, LANES), F32),
        jax.ShapeDtypeStruct((nm, N_HEADS, tm), F32),
    )
    out_specs = (
        pl.BlockSpec((tm, TN_IN), lambda i, j: (i, j)),
        pl.BlockSpec((None, tm, TN_IN), lambda i, j: (layer, i, jnp.clip(j - jk, 0, nkv - 1))),
        pl.BlockSpec((None, tm, TN_IN), lambda i, j: (layer, i, jnp.clip(j - jv, 0, nkv - 1))),
        pl.BlockSpec((tm, N_HEADS), lambda i, j: (i, 0)),
        pl.BlockSpec((tm, LANES), lambda i, j: (i, 0)),
        pl.BlockSpec((None, N_HEADS, tm), lambda i, j: (i, 0, 0)),
    )
    return pl.pallas_call(
        functools.partial(_inproj_body, seg=seg, use_carry=use_carry, n_alias=n_alias),
        grid=(nm, nj),
        in_specs=in_specs,
        out_specs=out_specs,
        out_shape=out_shape,
        scratch_shapes=[pltpu.VMEM((tm, D_MODEL), BF16), pltpu.VMEM((1, LANES), F32)],
        input_output_aliases=aliases,
        compiler_params=_params(2),
        name="inproj",
    )(*args)


def _fox_prompt_body(q_ref, k_ref, v_ref, ccol_ref, crow_ref, o_ref, m_scr, l_scr, acc_scr, cq_scr):
    h = pl.program_id(0)
    qi = pl.program_id(1)
    ki = pl.program_id(2)
    tq, tk = q_ref.shape[0], k_ref.shape[0]

    @pl.when(ki == 0)
    def _():
        m_scr[...] = jnp.full_like(m_scr, NEG_INF)
        l_scr[...] = jnp.zeros_like(l_scr)
        acc_scr[...] = jnp.zeros_like(acc_scr)
        lane = lax.broadcasted_iota(jnp.int32, (tq, LANES), 1)
        cq_scr[...] = jnp.sum(jnp.where(lane == h, ccol_ref[...], 0.0), axis=1, keepdims=True)

    def step(masked):
        s = _dot_nt(q_ref[...], k_ref[...])
        s = s + cq_scr[...] - crow_ref[pl.ds(h, 1), :]
        if masked:
            r = lax.broadcasted_iota(jnp.int32, (tq, tk), 0)
            c = lax.broadcasted_iota(jnp.int32, (tq, tk), 1)
            s = jnp.where(c <= r, s, NEG_INF)
        m_prev = m_scr[...]
        m_new = jnp.maximum(m_prev, jnp.max(s, axis=1, keepdims=True))
        alpha = jnp.exp(m_prev - m_new)
        p = jnp.exp(s - m_new)
        l_scr[...] = alpha * l_scr[...] + jnp.sum(p, axis=1, keepdims=True)
        acc_scr[...] = alpha * acc_scr[...] + _dot(p.astype(BF16), v_ref[...])
        m_scr[...] = m_new

    @pl.when(ki < qi)
    def _():
        step(False)

    @pl.when(ki == qi)
    def _():
        step(True)
        o_ref[...] = (acc_scr[...] / l_scr[...]).astype(BF16)


def _fox_prompt(z, ccol, crow, tq):
    rows = z.shape[0]
    nq = rows // tq
    qb, kb, vb = Q_OFF // HEAD_DIM, K_OFF // HEAD_DIM, V_OFF // HEAD_DIM
    return pl.pallas_call(
        _fox_prompt_body,
        grid=(N_HEADS, nq, nq),
        in_specs=[
            pl.BlockSpec((tq, HEAD_DIM), lambda h, qi, ki: (qi, qb + h)),
            pl.BlockSpec((tq, HEAD_DIM), lambda h, qi, ki: (jnp.minimum(ki, qi), kb + h)),
            pl.BlockSpec((tq, HEAD_DIM), lambda h, qi, ki: (jnp.minimum(ki, qi), vb + h)),
            pl.BlockSpec((tq, LANES), lambda h, qi, ki: (qi, 0)),
            pl.BlockSpec((None, N_HEADS, tq), lambda h, qi, ki: (jnp.minimum(ki, qi), 0, 0)),
        ],
        out_specs=pl.BlockSpec((tq, HEAD_DIM), lambda h, qi, ki: (qi, h)),
        out_shape=jax.ShapeDtypeStruct((rows, ATTN_DIM), BF16),
        scratch_shapes=[pltpu.VMEM((tq, 1), F32), pltpu.VMEM((tq, 1), F32),
                        pltpu.VMEM((tq, HEAD_DIM), F32), pltpu.VMEM((tq, 1), F32)],
        compiler_params=_params(3),
        name="fox_prompt",
    )(z, z, z, ccol, crow)


def _fox_sample_body(q_ref, kn_ref, vn_ref, cn_ref, ck_ref, cv_ref, lc_ref, o_ref,
                     qbd_scr, m_scr, l_scr, acc_scr, tail_scr):
    c = pl.program_id(1)
    nc = pl.num_programs(1)
    t_new = q_ref.shape[0]
    nrow = t_new * N_HEADS
    kc = ck_ref.shape[0]
    nblk = kc // SUFFIX_BLOCK
    cn = cn_ref[...]
    cq = jnp.concatenate([cn[:, t:t + 1] for t in range(t_new)], axis=0)
    col_head = lax.broadcasted_iota(jnp.int32, (N_HEADS, ATTN_DIM), 1) // HEAD_DIM
    row_head = lax.broadcasted_iota(jnp.int32, (N_HEADS, ATTN_DIM), 0)
    head_mask = col_head == row_head

    @pl.when(c == 0)
    def _():
        q = q_ref[...].astype(F32)
        q3 = jnp.where(head_mask[None], jnp.broadcast_to(q[:, None, :], (t_new, N_HEADS, ATTN_DIM)), 0.0)
        qbd_scr[...] = q3.reshape(nrow, ATTN_DIM).astype(BF16)
        m_scr[...] = jnp.full_like(m_scr, NEG_INF)
        l_scr[...] = jnp.zeros_like(l_scr)
        acc_scr[...] = jnp.zeros_like(acc_scr)
        tail_scr[...] = jnp.zeros_like(tail_scr)

    def online(s, vals):
        m_prev = m_scr[...]
        m_new = jnp.maximum(m_prev, jnp.max(s, axis=1, keepdims=True))
        alpha = jnp.exp(m_prev - m_new)
        p = jnp.exp(s - m_new)
        l_scr[...] = alpha * l_scr[...] + jnp.sum(p, axis=1, keepdims=True)
        acc_scr[...] = alpha * acc_scr[...] + _dot(p.astype(BF16), vals)
        m_scr[...] = m_new

    lc = lc_ref[...]
    x = jnp.concatenate([lc[:, b * SUFFIX_BLOCK:(b + 1) * SUFFIX_BLOCK] for b in range(nblk)], axis=0)
    jj = lax.broadcasted_iota(jnp.int32, (SUFFIX_BLOCK, SUFFIX_BLOCK), 0)
    ss = lax.broadcasted_iota(jnp.int32, (SUFFIX_BLOCK, SUFFIX_BLOCK), 1)
    excl = _dot_exact01(jnp.where(jj > ss, 1.0, 0.0).astype(BF16), x, nt=True)
    off = tail_scr[...]
    pieces = [None] * nblk
    for b in range(nblk - 1, -1, -1):
        eb = excl[b * N_HEADS:(b + 1) * N_HEADS, :]
        pieces[b] = eb + off
        off = off + (eb[:, 0:1] + x[b * N_HEADS:(b + 1) * N_HEADS, 0:1])
    tail_scr[...] = off
    suffix = jnp.concatenate(pieces, axis=1)
    s = _dot_nt(qbd_scr[...], ck_ref[...].astype(BF16))
    s = (s.reshape(t_new, N_HEADS, kc) + cq.reshape(t_new, N_HEADS, 1) + suffix[None]).reshape(nrow, kc)
    online(s, cv_ref[...].astype(BF16))

    @pl.when(c == nc - 1)
    def _():
        sn = _dot_nt(qbd_scr[...], kn_ref[...])
        sn = sn.reshape(t_new, N_HEADS, t_new) + cq.reshape(t_new, N_HEADS, 1) + (-cn)[None]
        tq = lax.broadcasted_iota(jnp.int32, (t_new, N_HEADS, t_new), 0)
        tk = lax.broadcasted_iota(jnp.int32, (t_new, N_HEADS, t_new), 2)
        sn = jnp.where(tk <= tq, sn, NEG_INF).reshape(nrow, t_new)
        online(sn, vn_ref[...])
        o = (acc_scr[...] / l_scr[...]).reshape(t_new, N_HEADS, ATTN_DIM)
        o_ref[...] = jnp.sum(jnp.where(head_mask[None], o, 0.0), axis=1).astype(BF16)


def _fox_sample(z, cn_t, cache_k, cache_v, lc_t, layer, t_new):
    nb = cache_k.shape[1]
    past = cache_k.shape[2]
    nc = past // KV_CHUNK
    nrow = t_new * N_HEADS
    qb, kb, vb = Q_OFF // ATTN_DIM, K_OFF // ATTN_DIM, V_OFF // ATTN_DIM
    return pl.pallas_call(
        _fox_sample_body,
        grid=(nb, nc),
        in_specs=[
            pl.BlockSpec((t_new, ATTN_DIM), lambda b, c: (b, qb)),
            pl.BlockSpec((t_new, ATTN_DIM), lambda b, c: (b, kb)),
            pl.BlockSpec((t_new, ATTN_DIM), lambda b, c: (b, vb)),
            pl.BlockSpec((None, N_HEADS, t_new), lambda b, c: (b, 0, 0)),
            pl.BlockSpec((None, None, KV_CHUNK, ATTN_DIM), lambda b, c: (layer, b, nc - 1 - c, 0)),
            pl.BlockSpec((None, None, KV_CHUNK, ATTN_DIM), lambda b, c: (layer, b, nc - 1 - c, 0)),
            pl.BlockSpec((None, None, N_HEADS, KV_CHUNK), lambda b, c: (layer, b, 0, nc - 1 - c)),
        ],
        out_specs=pl.BlockSpec((t_new, ATTN_DIM), lambda b, c: (b, 0)),
        out_shape=jax.ShapeDtypeStruct((nb * t_new, ATTN_DIM), BF16),
        scratch_shapes=[pltpu.VMEM((nrow, ATTN_DIM), BF16), pltpu.VMEM((nrow, 1), F32),
                        pltpu.VMEM((nrow, 1), F32), pltpu.VMEM((nrow, ATTN_DIM), F32),
                        pltpu.VMEM((N_HEADS, 1), F32)],
        compiler_params=_params(2),
        name="fox_sample",
    )(z, z, z, cn_t, cache_k, cache_v, lc_t)


def _mix_body(*refs, seg, has_hist):
    zc_ref, ya_ref, gc_ref, ga_ref, wc_ref, wpc_ref, wpa_ref = refs[:7]
    n_in = 8 if has_hist else 7
    mg_ref, st_ref = refs[n_in:n_in + 2]
    i = pl.program_id(0)
    tm = zc_ref.shape[0]
    xc = zc_ref[:, 0:CONV_DIM].astype(F32)
    bc = zc_ref[:, CONV_DIM:2 * CONV_DIM].astype(F32)
    cc = zc_ref[:, 2 * CONV_DIM:3 * CONV_DIM].astype(F32)
    u = cc * xc
    if has_hist:
        hist = refs[7][...]
        h0 = _seg_rows(hist[:, 0, :], seg)
        h1 = _seg_rows(hist[:, 1, :], seg)
        nseg = tm // seg
        st_ref[...] = u.reshape(nseg, seg, CONV_DIM)[:, seg - 2:seg, :]
    else:
        carry_scr = refs[n_in + 2]

        @pl.when(i == 0)
        def _():
            carry_scr[...] = jnp.zeros_like(carry_scr)
        h0 = carry_scr[0:1, :]
        h1 = carry_scr[1:2, :]
        st_ref[...] = u[tm - 2:tm, :][None]
    yc = (bc * _conv3(u, h0, h1, wc_ref[...], seg)).astype(BF16)
    if not has_hist:
        carry_scr[0:2, :] = u[tm - 2:tm, :]
    t1 = _dot(yc, wpc_ref[...])
    t2 = _dot(ya_ref[...], wpa_ref[...])
    mg_ref[...] = (gc_ref[...].astype(F32) * t1 + ga_ref[...].astype(F32) * t2).astype(BF16)


def _mix(z, ya, w_conv, w_pc, w_pa, layer, tm, seg, hist):
    rows = z.shape[0]
    nm = rows // tm
    has_hist = hist is not None
    nseg = tm // seg if has_hist else 1
    in_specs = [
        pl.BlockSpec((tm, 3 * CONV_DIM), lambda i: (i, 0)),
        pl.BlockSpec((tm, ATTN_DIM), lambda i: (i, 0)),
        pl.BlockSpec((tm, D_MODEL), lambda i: (i, G_OFF // D_MODEL)),
        pl.BlockSpec((tm, D_MODEL), lambda i: (i, G_OFF // D_MODEL + 1)),
        pl.BlockSpec((None, 3, CONV_DIM), lambda i: (layer, 0, 0)),
        pl.BlockSpec((None, CONV_DIM, D_MODEL), lambda i: (layer, 0, 0)),
        pl.BlockSpec((None, ATTN_DIM, D_MODEL), lambda i: (layer, 0, 0)),
    ]
    args = [z, ya, z, z, w_conv, w_pc, w_pa]
    scratch = []
    if has_hist:
        in_specs.append(pl.BlockSpec((nseg, 2, CONV_DIM), lambda i: (0, 0, 0)))
        args.append(hist)
    else:
        scratch.append(pltpu.VMEM((8, CONV_DIM), F32))
    return pl.pallas_call(
        functools.partial(_mix_body, seg=seg, has_hist=has_hist),
        grid=(nm,),
        in_specs=in_specs,
        out_specs=(pl.BlockSpec((tm, D_MODEL), lambda i: (i, 0)),
                   pl.BlockSpec((nseg, 2, CONV_DIM), lambda i: (0, 0, 0))),
        out_shape=(jax.ShapeDtypeStruct((rows, D_MODEL), BF16),
                   jax.ShapeDtypeStruct((nseg, 2, CONV_DIM), F32)),
        scratch_shapes=scratch,
        compiler_params=_params(1),
        name="mix",
    )(*args)


def _oproj_body(x_ref, mg_ref, wo_ref, o_ref):
    o_ref[...] = x_ref[...] + _dot(mg_ref[...], wo_ref[...])


def _oproj(x, mg, w_o, layer, tm):
    rows = x.shape[0]
    return pl.pallas_call(
        _oproj_body,
        grid=(rows // tm,),
        in_specs=[pl.BlockSpec((tm, D_MODEL), lambda i: (i, 0)),
                  pl.BlockSpec((tm, D_MODEL), lambda i: (i, 0)),
                  pl.BlockSpec((None, D_MODEL, D_MODEL), lambda i: (layer, 0, 0))],
        out_specs=pl.BlockSpec((tm, D_MODEL), lambda i: (i, 0)),
        out_shape=jax.ShapeDtypeStruct((rows, D_MODEL), F32),
        compiler_params=_params(1),
        name="oproj",
    )(x, mg, w_o)


def _ffn_body(*refs, seg, has_hist):
    (x_ref, g_ref, wua_ref, wub_ref, wca_ref, wcb_ref, ba_ref, bb_ref, wd_ref) = refs[:9]
    n_in = 11 if has_hist else 9
    o_ref, st_ref, h_scr, acc_scr = refs[n_in:n_in + 4]
    i = pl.program_id(0)
    j = pl.program_id(1)
    nj = pl.num_programs(1)
    tm = x_ref.shape[0]
    tf = wua_ref.shape[1]

    @pl.when(j == 0)
    def _():
        h_scr[...] = _rmsnorm(x_ref[...], g_ref[...]).astype(BF16)
        acc_scr[...] = jnp.zeros_like(acc_scr)

    hb = h_scr[...]
    ua = _dot(hb, wua_ref[...])
    ub = _dot(hb, wub_ref[...])
    if has_hist:
        nseg = tm // seg
        hists = []
        for part, (u, hist_ref) in enumerate(((ua, refs[9]), (ub, refs[10]))):
            hist = hist_ref[...]
            hists.append((_seg_rows(hist[:, 0, :], seg), _seg_rows(hist[:, 1, :], seg)))
            st_ref[:, :, part, :] = u.reshape(nseg, seg, tf)[:, seg - 2:seg, :]
    else:
        ca_scr, cb_scr = refs[n_in + 4:n_in + 6]

        @pl.when(i == 0)
        def _():
            ca_scr[j] = jnp.zeros((8, tf), F32)
            cb_scr[j] = jnp.zeros((8, tf), F32)
        hists = [(ca_scr[j, 0:1, :], ca_scr[j, 1:2, :]), (cb_scr[j, 0:1, :], cb_scr[j, 1:2, :])]
        st_ref[0, :, 0, :] = ua[tm - 2:tm, :]
        st_ref[0, :, 1, :] = ub[tm - 2:tm, :]
    a = _conv3(ua, hists[0][0], hists[0][1], wca_ref[...], seg) + ba_ref[...]
    b = _conv3(ub, hists[1][0], hists[1][1], wcb_ref[...], seg) + bb_ref[...]
    if not has_hist:
        ca_scr[j, 0:2, :] = ua[tm - 2:tm, :]
        cb_scr[j, 0:2, :] = ub[tm - 2:tm, :]
    gated = ((a * _sigmoid(a)) * b).astype(BF16)
    acc_scr[...] += _dot(gated, wd_ref[...])

    @pl.when(j == nj - 1)
    def _():
        o_ref[...] = x_ref[...] + acc_scr[...]


def _ffn(x, g, w_up, w_fconv, b_fconv, w_down, layer, tm, seg, hist):
    rows = x.shape[0]
    nm = rows // tm
    nf = D_FF // TF
    has_hist = hist is not None
    nseg = tm // seg if has_hist else 1
    in_specs = [
        pl.BlockSpec((tm, D_MODEL), lambda i, j: (i, 0), pipeline_mode=pl.Buffered(1)),
        pl.BlockSpec((None, 1, D_MODEL), lambda i, j: (layer, 0, 0)),
        pl.BlockSpec((None, D_MODEL, TF), lambda i, j: (layer, 0, j)),
        pl.BlockSpec((None, D_MODEL, TF), lambda i, j: (layer, 0, nf + j)),
        pl.BlockSpec((None, 3, TF), lambda i, j: (layer, 0, j)),
        pl.BlockSpec((None, 3, TF), lambda i, j: (layer, 0, nf + j)),
        pl.BlockSpec((None, 1, TF), lambda i, j: (layer, 0, j)),
        pl.BlockSpec((None, 1, TF), lambda i, j: (layer, 0, nf + j)),
        pl.BlockSpec((None, TF, D_MODEL), lambda i, j: (layer, j, 0)),
    ]
    args = [x, g, w_up, w_up, w_fconv, w_fconv, b_fconv, b_fconv, w_down]
    scratch = [pltpu.VMEM((tm, D_MODEL), BF16), pltpu.VMEM((tm, D_MODEL), F32)]
    if has_hist:
        in_specs += [pl.BlockSpec((nseg, 2, TF), lambda i, j: (0, 0, j)),
                     pl.BlockSpec((nseg, 2, TF), lambda i, j: (0, 0, nf + j))]
        args += [hist, hist]
    else:
        scratch += [pltpu.VMEM((nf, 8, TF), F32), pltpu.VMEM((nf, 8, TF), F32)]
    return pl.pallas_call(
        functools.partial(_ffn_body, seg=seg, has_hist=has_hist),
        grid=(nm, nf),
        in_specs=in_specs,
        out_specs=(pl.BlockSpec((tm, D_MODEL), lambda i, j: (i, 0), pipeline_mode=pl.Buffered(1)),
                   pl.BlockSpec((None, nseg, 2, 2, TF), lambda i, j: (i, 0, 0, 0, j))),
        out_shape=(jax.ShapeDtypeStruct((rows, D_MODEL), F32),
                   jax.ShapeDtypeStruct((nm, nseg, 2, 2, D_FF), F32)),
        scratch_shapes=scratch,
        compiler_params=_params(2),
        name="ffn",
    )(*args)


def _final_norm_body(x_ref, g_ref, o_ref):
    o_ref[...] = _rmsnorm(x_ref[...], g_ref[...])


def _final_norm(x, g, tm):
    rows = x.shape[0]
    return pl.pallas_call(
        _final_norm_body,
        grid=(rows // tm,),
        in_specs=[pl.BlockSpec((tm, D_MODEL), lambda i: (i, 0)),
                  pl.BlockSpec((1, D_MODEL), lambda i: (0, 0))],
        out_specs=pl.BlockSpec((tm, D_MODEL), lambda i: (i, 0)),
        out_shape=jax.ShapeDtypeStruct((rows, D_MODEL), F32),
        compiler_params=_params(1),
        name="final_norm",
    )(x, g)


def kernel(x_prompt, x_sample, cache_k, cache_v, cache_logf, state_conv, state_ffn_conv, meta_tokens,
           g_mix, w_in, b_in, w_conv, w_pc, w_pa, w_o, g_ffn, w_up, w_fconv, b_fconv, w_down, g_final):
    depth = w_in.shape[0]
    batch, seq, _ = x_prompt.shape
    nb, t_new, _ = x_sample.shape
    past = cache_k.shape[2]
    lp = N_META + seq
    assert batch == 1 and past % KV_CHUNK == 0 and lp % TM_BIG == 0 and lp % TM_SMALL == 0
    rows_s = nb * t_new

    w_main = jnp.concatenate([w_in[:, :, :F_OFF], w_in[:, :, F_OFF + N_HEADS:]], axis=2).astype(BF16)
    b_main = jnp.concatenate([b_in[:, :F_OFF], b_in[:, F_OFF + N_HEADS:]], axis=1)[:, None, :]
    w_f = jnp.pad(w_in[:, :, F_OFF:F_OFF + N_HEADS], ((0, 0), (0, 0), (0, LANES - N_HEADS))).astype(BF16)
    b_f = jnp.pad(b_in[:, F_OFF:F_OFF + N_HEADS], ((0, 0), (0, LANES - N_HEADS)))[:, None, :]
    w_pc_b, w_pa_b, w_o_b = w_pc.astype(BF16), w_pa.astype(BF16), w_o.astype(BF16)
    w_up_b, w_down_b = w_up.astype(BF16), w_down.astype(BF16)
    g_mix3, g_ffn3, b_fconv3 = g_mix[:, None, :], g_ffn[:, None, :], b_fconv[:, None, :]

    xp = jnp.concatenate([meta_tokens.astype(F32), x_prompt[0]], axis=0)
    xs = x_sample.reshape(rows_s, D_MODEL)
    ck = cache_k.reshape(depth, nb, past, ATTN_DIM)
    cv = cache_v.reshape(depth, nb, past, ATTN_DIM)
    lc_t = jnp.swapaxes(cache_logf.astype(F32), 2, 3)

    kv_p = kv_s = None
    lf_p, lf_s, cst_p, cst_s, fst_p, fst_s = [], [], [], [], [], []
    for l in range(depth):
        z, k_all, v_all, lf, ccol, crow = _inproj(xp, g_mix3, w_main, b_main, w_f, b_f, l, lp, TM_BIG,
                                                   TM_BIG, True, kv_p)
        kv_p = (k_all, v_all)
        ya = _fox_prompt(z, ccol, crow, TM_BIG)
        mg, cst = _mix(z, ya, w_conv, w_pc_b, w_pa_b, l, TM_SMALL, TM_SMALL, None)
        x1 = _oproj(xp, mg, w_o_b, l, TM_SMALL)
        xp, fst = _ffn(x1, g_ffn3, w_up_b, w_fconv, b_fconv3, w_down_b, l, TM_BIG, TM_BIG, None)
        lf_p.append(lf); cst_p.append(cst); fst_p.append(fst[-1].reshape(1, 2, 2 * D_FF))
        z, k_all, v_all, lf, _, crow = _inproj(xs, g_mix3, w_main, b_main, w_f, b_f, l, rows_s, rows_s,
                                               t_new, False, kv_s)
        kv_s = (k_all, v_all)
        cn_t = jnp.swapaxes(crow[0].reshape(N_HEADS, nb, t_new), 0, 1)
        ya = _fox_sample(z, cn_t, ck, cv, lc_t, l, t_new)
        mg, cst = _mix(z, ya, w_conv, w_pc_b, w_pa_b, l, rows_s, t_new, state_conv[l])
        x1 = _oproj(xs, mg, w_o_b, l, rows_s)
        xs, fst = _ffn(x1, g_ffn3, w_up_b, w_fconv, b_fconv3, w_down_b, l, rows_s, t_new,
                       state_ffn_conv[l])
        lf_s.append(lf); cst_s.append(cst); fst_s.append(fst[-1].reshape(nb, 2, 2 * D_FF))

    g_fin = g_final[None, :]
    y_prompt = _final_norm(xp, g_fin, TM_BIG)[N_META:][None]
    y_sample = _final_norm(xs, g_fin, rows_s).reshape(nb, t_new, D_MODEL)
    return (y_prompt, y_sample,
            kv_p[0].reshape(depth, 1, lp, N_HEADS, HEAD_DIM),
            kv_p[1].reshape(depth, 1, lp, N_HEADS, HEAD_DIM),
            jnp.stack(lf_p).reshape(depth, 1, lp, N_HEADS),
            jnp.stack(cst_p),
            jnp.stack(fst_p),
            kv_s[0].reshape(depth, nb, t_new, N_HEADS, HEAD_DIM),
            kv_s[1].reshape(depth, nb, t_new, N_HEADS, HEAD_DIM),
            jnp.stack(lf_s).reshape(depth, nb, t_new, N_HEADS),
            jnp.stack(cst_s),
            jnp.stack(fst_s))
```

```python
import functools

import jax
import jax.numpy as jnp
from jax import lax
from jax.experimental import pallas as pl
from jax.experimental.pallas import tpu as pltpu

F32 = jnp.float32
BF16 = jnp.bfloat16

D_MODEL = 2048
N_META = 16
CONV_DIM = D_MODEL // 2
HEAD_DIM = 128
ATTN_DIM = D_MODEL // 2
N_HEADS = ATTN_DIM // HEAD_DIM
D_FF = 2 * D_MODEL
EPS = 1e-6
ATTN_SCALE = HEAD_DIM ** -0.5
NEG_INF = -1e30
F_OFF = 3 * CONV_DIM + 3 * ATTN_DIM
MAIN_DIM = F_OFF + 2 * D_MODEL

LANES = 128
VMEM_LIMIT = 60000 * 1024

Q_OFF = 3 * CONV_DIM
K_OFF = Q_OFF + ATTN_DIM
V_OFF = K_OFF + ATTN_DIM
G_OFF = V_OFF + ATTN_DIM

TN_IN = 512
TF = 512
TM_BIG = 912
TM_SMALL = 432
ROW_CHUNK = 304
KV_CHUNK = 1024
SUFFIX_BLOCK = 256


def _params(n_axes):
    return pltpu.CompilerParams(dimension_semantics=("arbitrary",) * n_axes,
                                vmem_limit_bytes=VMEM_LIMIT)


def _dot(a, b):
    return jnp.dot(a, b, preferred_element_type=F32)


def _dot_nt(a, b):
    return lax.dot_general(a, b, (((1,), (1,)), ((), ())), preferred_element_type=F32)


def _dot_exact01(mask_bf16, x, nt=False):
    hi = x.astype(BF16)
    r1 = x - hi.astype(F32)
    mid = r1.astype(BF16)
    lo = (r1 - mid.astype(F32)).astype(BF16)
    if nt:
        return _dot(hi, mask_bf16) + _dot(mid, mask_bf16) + _dot(lo, mask_bf16)
    return _dot(mask_bf16, hi) + _dot(mask_bf16, mid) + _dot(mask_bf16, lo)


def _rmsnorm(x, g):
    ms = jnp.mean(x * x, axis=-1, keepdims=True)
    return (x * lax.rsqrt(ms + EPS)) * g


def _sigmoid(x):
    return 1.0 / (1.0 + jnp.exp(-x))


def _conv3(u, h0rows, h1rows, w, seg):
    tm = u.shape[0]
    r = lax.broadcasted_iota(jnp.int32, (tm, 1), 0)
    if seg < tm:
        r = r % seg
    um1 = jnp.where(r == 0, h1rows, pltpu.roll(u, 1, 0))
    um2 = jnp.where(r == 0, h0rows, jnp.where(r == 1, h1rows, pltpu.roll(u, 2, 0)))
    return um2 * w[0:1] + um1 * w[1:2] + u * w[2:3]


def _seg_rows(h, seg):
    nseg, c = h.shape
    return jnp.broadcast_to(h[:, None, :], (nseg, seg, c)).reshape(nseg * seg, c)


def _inproj_body(*refs, seg, use_carry, n_alias):
    x_ref, g_ref, w_ref, b_ref, wf_ref, bf_ref = refs[:6]
    (z_ref, k_ref, v_ref, lf_ref, ccol_ref, crow_ref, h_scr, carry_scr) = refs[6 + n_alias:]
    i = pl.program_id(0)
    j = pl.program_id(1)
    tm = x_ref.shape[0]
    jq, jk, jv, jg = (Q_OFF // TN_IN, K_OFF // TN_IN, V_OFF // TN_IN, G_OFF // TN_IN)

    @pl.when(j == 0)
    def _():
        hb = _rmsnorm(x_ref[...], g_ref[...]).astype(BF16)
        h_scr[...] = hb
        fl = _dot(hb, wf_ref[...]) + bf_ref[...]
        lf = jnp.minimum(fl, 0.0) - jnp.log1p(jnp.exp(-jnp.abs(fl)))
        lf_ref[...] = lf[:, :N_HEADS]
        r = lax.broadcasted_iota(jnp.int32, (tm, tm), 0)
        c = lax.broadcasted_iota(jnp.int32, (tm, tm), 1)
        tri = c <= r
        if seg < tm:
            tri = tri & ((r // seg) == (c // seg))
        cum = _dot_exact01(jnp.where(tri, 1.0, 0.0).astype(BF16), lf)
        if use_carry:
            @pl.when(i == 0)
            def _():
                carry_scr[...] = jnp.zeros_like(carry_scr)
            cum = cum + carry_scr[...]
            carry_scr[...] = cum[tm - 1:tm, :]
        ccol_ref[...] = cum
        crow_ref[...] = cum.T[:N_HEADS, :]

    acc = _dot(h_scr[...], w_ref[...]) + b_ref[...]
    is_q = (j >= jq) & (j < jk)
    is_gate = j >= jg

    @pl.when(is_q)
    def _():
        z_ref[...] = (acc * ATTN_SCALE).astype(BF16)

    @pl.when(is_gate)
    def _():
        z_ref[...] = _sigmoid(acc).astype(BF16)

    @pl.when(jnp.logical_not(is_q | is_gate))
    def _():
        z_ref[...] = acc.astype(BF16)

    @pl.when((j >= jk) & (j < jv))
    def _():
        k_ref[...] = acc

    @pl.when((j >= jv) & (j < jg))
    def _():
        v_ref[...] = acc


def _inproj(x, g, w_main, b_main, w_f, b_f, layer, n_valid, tm, seg, use_carry, kv_prev):
    rows = x.shape[0]
    nm = rows // tm
    nj = MAIN_DIM // TN_IN
    jk, jv = K_OFF // TN_IN, V_OFF // TN_IN
    nkv = ATTN_DIM // TN_IN
    depth = w_main.shape[0]
    n_alias = 0 if kv_prev is None else 2
    in_specs = [
        pl.BlockSpec((tm, D_MODEL), lambda i, j: (i, 0)),
        pl.BlockSpec((None, 1, D_MODEL), lambda i, j: (layer, 0, 0)),
        pl.BlockSpec((None, D_MODEL, TN_IN), lambda i, j: (layer, 0, j)),
        pl.BlockSpec((None, 1, TN_IN), lambda i, j: (layer, 0, j)),
        pl.BlockSpec((None, D_MODEL, LANES), lambda i, j: (layer, 0, 0)),
        pl.BlockSpec((None, 1, LANES), lambda i, j: (layer, 0, 0)),
    ]
    args = [x, g, w_main, b_main, w_f, b_f]
    aliases = {}
    if kv_prev is not None:
        in_specs += [pl.BlockSpec(memory_space=pl.ANY)] * 2
        args += list(kv_prev)
        aliases = {6: 1, 7: 2}
    out_shape = (
        jax.ShapeDtypeStruct((rows, MAIN_DIM), BF16),
        jax.ShapeDtypeStruct((depth, n_valid, ATTN_DIM), F32),
        jax.ShapeDtypeStruct((depth, n_valid, ATTN_DIM), F32),
        jax.ShapeDtypeStruct((n_valid, N_HEADS), F32),
        jax.ShapeDtypeStruct((rows, LANES), F32),
        jax.ShapeDtypeStruct((nm, N_HEADS, tm), F32),
    )
    out_specs = (
        pl.BlockSpec((tm, TN_IN), lambda i, j: (i, j)),
        pl.BlockSpec((None, tm, TN_IN), lambda i, j: (layer, i, jnp.clip(j - jk, 0, nkv - 1))),
        pl.BlockSpec((None, tm, TN_IN), lambda i, j: (layer, i, jnp.clip(j - jv, 0, nkv - 1))),
        pl.BlockSpec((tm, N_HEADS), lambda i, j: (i, 0)),
        pl.BlockSpec((tm, LANES), lambda i, j: (i, 0)),
        pl.BlockSpec((None, N_HEADS, tm), lambda i, j: (i, 0, 0)),
    )
    return pl.pallas_call(
        functools.partial(_inproj_body, seg=seg, use_carry=use_carry, n_alias=n_alias),
        grid=(nm, nj),
        in_specs=in_specs,
        out_specs=out_specs,
        out_shape=out_shape,
        scratch_shapes=[pltpu.VMEM((tm, D_MODEL), BF16), pltpu.VMEM((1, LANES), F32)],
        input_output_aliases=aliases,
        compiler_params=_params(2),
        name="inproj",
    )(*args)


def _fox_prompt_body(q_ref, k_ref, v_ref, ccol_ref, crow_ref, o_ref, m_scr, l_scr, acc_scr, cq_scr):
    h = pl.program_id(0)
    qi = pl.program_id(1)
    tq = q_ref.shape[0]
    tk = tq
    m_scr[...] = jnp.full_like(m_scr, NEG_INF)
    l_scr[...] = jnp.zeros_like(l_scr)
    acc_scr[...] = jnp.zeros_like(acc_scr)
    lane = lax.broadcasted_iota(jnp.int32, (tq, LANES), 1)
    cq_scr[...] = jnp.sum(jnp.where(lane == h, ccol_ref[...], 0.0), axis=1, keepdims=True)

    def tile(kb, diagonal):
        start = pl.multiple_of(kb * tk, ROW_CHUNK)
        k = k_ref[pl.ds(start, tk), :]
        v = v_ref[pl.ds(start, tk), :]
        ck = crow_ref[kb, pl.ds(h, 1), :]
        chunks = list(range(0, tq, ROW_CHUNK))

        def n_keys(r0):
            return r0 + ROW_CHUNK if diagonal else tk

        def scores(r0):
            nk = n_keys(r0)
            return _dot_nt(q_ref[r0:r0 + ROW_CHUNK, :], k[:nk])

        def softmax(r0, s):
            rows = slice(r0, r0 + ROW_CHUNK)
            nk = n_keys(r0)
            s = s + cq_scr[rows, :] - ck[:, :nk]
            if diagonal:
                r = lax.broadcasted_iota(jnp.int32, (ROW_CHUNK, nk), 0) + r0
                c = lax.broadcasted_iota(jnp.int32, (ROW_CHUNK, nk), 1)
                s = jnp.where(c <= r, s, NEG_INF)
            m_prev = m_scr[rows, :]
            m_new = jnp.maximum(m_prev, jnp.max(s, axis=1, keepdims=True))
            alpha = jnp.exp(m_prev - m_new)
            p = jnp.exp(s - m_new)
            l_scr[rows, :] = alpha * l_scr[rows, :] + jnp.sum(p, axis=1, keepdims=True)
            m_scr[rows, :] = m_new
            return alpha, p.astype(BF16)

        def values(r0, alpha, p):
            rows = slice(r0, r0 + ROW_CHUNK)
            acc_scr[rows, :] = alpha * acc_scr[rows, :] + _dot(p, v[:n_keys(r0)])

        s_next = scores(chunks[0])
        pending = None
        for idx, r0 in enumerate(chunks):
            s_cur = s_next
            if idx + 1 < len(chunks):
                s_next = scores(chunks[idx + 1])
            alpha, p = softmax(r0, s_cur)
            if pending is not None:
                values(*pending)
            pending = (r0, alpha, p)
        values(*pending)

    tile(qi, True)

    def older(kk, carry):
        tile(qi - kk, False)
        return carry

    lax.fori_loop(1, qi + 1, older, 0)
    o_ref[...] = (acc_scr[...] / l_scr[...]).astype(BF16)


def _fox_prompt(z, ccol, crow, tq):
    rows = z.shape[0]
    nq = rows // tq
    qb, kb, vb = Q_OFF // HEAD_DIM, K_OFF // HEAD_DIM, V_OFF // HEAD_DIM
    return pl.pallas_call(
        _fox_prompt_body,
        grid=(N_HEADS, nq),
        in_specs=[
            pl.BlockSpec((tq, HEAD_DIM), lambda h, qi: (qi, qb + h)),
            pl.BlockSpec((rows, HEAD_DIM), lambda h, qi: (0, kb + h)),
            pl.BlockSpec((rows, HEAD_DIM), lambda h, qi: (0, vb + h)),
            pl.BlockSpec((tq, LANES), lambda h, qi: (qi, 0)),
            pl.BlockSpec((nq, N_HEADS, tq), lambda h, qi: (0, 0, 0)),
        ],
        out_specs=pl.BlockSpec((tq, HEAD_DIM), lambda h, qi: (qi, h)),
        out_shape=jax.ShapeDtypeStruct((rows, ATTN_DIM), BF16),
        scratch_shapes=[pltpu.VMEM((tq, 1), F32), pltpu.VMEM((tq, 1), F32),
                        pltpu.VMEM((tq, HEAD_DIM), F32), pltpu.VMEM((tq, 1), F32)],
        compiler_params=_params(2),
        name="fox_prompt",
    )(z, z, z, ccol, crow)


def _fox_sample_body(*refs):
    (q_ref, kn_ref, vn_ref, cn_ref, lc_ref, ck_ref, cv_ref, o_ref,
     qbd_scr, m_scr, l_scr, acc_scr, tail_scr) = refs
    c = pl.program_id(1)
    nc = pl.num_programs(1)
    t_new = q_ref.shape[0]
    nrow = t_new * N_HEADS
    kc = ck_ref.shape[0] // N_HEADS
    nblk = kc // SUFFIX_BLOCK
    cn = cn_ref[...]
    cq = jnp.concatenate([cn[:, t:t + 1] for t in range(t_new)], axis=0)
    col_head = lax.broadcasted_iota(jnp.int32, (N_HEADS, ATTN_DIM), 1) // HEAD_DIM
    row_head = lax.broadcasted_iota(jnp.int32, (N_HEADS, ATTN_DIM), 0)
    head_mask = col_head == row_head

    @pl.when(c == 0)
    def _():
        q = q_ref[...].astype(F32)
        q3 = jnp.where(head_mask[None], jnp.broadcast_to(q[:, None, :], (t_new, N_HEADS, ATTN_DIM)), 0.0)
        qbd_scr[...] = q3.reshape(nrow, ATTN_DIM).astype(BF16)
        m_scr[...] = jnp.full_like(m_scr, NEG_INF)
        l_scr[...] = jnp.zeros_like(l_scr)
        acc_scr[...] = jnp.zeros_like(acc_scr)
        tail_scr[...] = jnp.zeros_like(tail_scr)

    def online(s, vals):
        m_prev = m_scr[...]
        m_new = jnp.maximum(m_prev, jnp.max(s, axis=1, keepdims=True))
        alpha = jnp.exp(m_prev - m_new)
        p = jnp.exp(s - m_new)
        l_scr[...] = alpha * l_scr[...] + jnp.sum(p, axis=1, keepdims=True)
        acc_scr[...] = alpha * acc_scr[...] + _dot(p.astype(BF16), vals)
        m_scr[...] = m_new

    lc = lc_ref[...]
    x = jnp.concatenate([lc[:, b * SUFFIX_BLOCK:(b + 1) * SUFFIX_BLOCK] for b in range(nblk)], axis=0)
    jj = lax.broadcasted_iota(jnp.int32, (SUFFIX_BLOCK, SUFFIX_BLOCK), 0)
    ss = lax.broadcasted_iota(jnp.int32, (SUFFIX_BLOCK, SUFFIX_BLOCK), 1)
    excl = _dot_exact01(jnp.where(jj > ss, 1.0, 0.0).astype(BF16), x, nt=True)
    off = tail_scr[...]
    pieces = [None] * nblk
    for b in range(nblk - 1, -1, -1):
        eb = excl[b * N_HEADS:(b + 1) * N_HEADS, :]
        pieces[b] = eb + off
        off = off + (eb[:, 0:1] + x[b * N_HEADS:(b + 1) * N_HEADS, 0:1])
    tail_scr[...] = off
    suffix = jnp.concatenate(pieces, axis=1)

    def all_heads(ref):
        return jnp.concatenate([ref[pl.ds(h, kc, stride=N_HEADS), :].astype(BF16)
                                for h in range(N_HEADS)], axis=1)

    s = _dot_nt(qbd_scr[...], all_heads(ck_ref))
    s = (s.reshape(t_new, N_HEADS, kc) + cq.reshape(t_new, N_HEADS, 1) + suffix[None]).reshape(nrow, kc)
    online(s, all_heads(cv_ref))

    @pl.when(c == nc - 1)
    def _():
        sn = _dot_nt(qbd_scr[...], kn_ref[...])
        sn = sn.reshape(t_new, N_HEADS, t_new) + cq.reshape(t_new, N_HEADS, 1) + (-cn)[None]
        tq = lax.broadcasted_iota(jnp.int32, (t_new, N_HEADS, t_new), 0)
        tk = lax.broadcasted_iota(jnp.int32, (t_new, N_HEADS, t_new), 2)
        sn = jnp.where(tk <= tq, sn, NEG_INF).reshape(nrow, t_new)
        online(sn, vn_ref[...])
        o = (acc_scr[...] / l_scr[...]).reshape(t_new, N_HEADS, ATTN_DIM)
        o_ref[...] = jnp.sum(jnp.where(head_mask[None], o, 0.0), axis=1).astype(BF16)


def _fox_sample(z, cn_t, cache_k, cache_v, lc_t, layer, t_new):
    nb = cache_k.shape[1]
    past = cache_k.shape[2] // N_HEADS
    nc = past // KV_CHUNK
    nrow = t_new * N_HEADS
    qb, kb, vb = Q_OFF // ATTN_DIM, K_OFF // ATTN_DIM, V_OFF // ATTN_DIM
    cache_spec = pl.BlockSpec((None, None, KV_CHUNK * N_HEADS, HEAD_DIM),
                              lambda b, c: (layer, b, nc - 1 - c, 0))
    return pl.pallas_call(
        _fox_sample_body,
        grid=(nb, nc),
        in_specs=[
            pl.BlockSpec((t_new, ATTN_DIM), lambda b, c: (b, qb)),
            pl.BlockSpec((t_new, ATTN_DIM), lambda b, c: (b, kb)),
            pl.BlockSpec((t_new, ATTN_DIM), lambda b, c: (b, vb)),
            pl.BlockSpec((None, N_HEADS, t_new), lambda b, c: (b, 0, 0)),
            pl.BlockSpec((None, None, N_HEADS, KV_CHUNK), lambda b, c: (layer, b, 0, nc - 1 - c)),
            cache_spec, cache_spec,
        ],
        out_specs=pl.BlockSpec((t_new, ATTN_DIM), lambda b, c: (b, 0)),
        out_shape=jax.ShapeDtypeStruct((nb * t_new, ATTN_DIM), BF16),
        scratch_shapes=[pltpu.VMEM((nrow, ATTN_DIM), BF16), pltpu.VMEM((nrow, 1), F32),
                        pltpu.VMEM((nrow, 1), F32), pltpu.VMEM((nrow, ATTN_DIM), F32),
                        pltpu.VMEM((N_HEADS, 1), F32)],
        compiler_params=_params(2),
        name="fox_sample",
    )(z, z, z, cn_t, lc_t, cache_k, cache_v)


def _mix_body(*refs, seg, has_hist):
    zc_ref, ya_ref, gc_ref, ga_ref, wc_ref, wpc_ref, wpa_ref = refs[:7]
    n_in = 8 if has_hist else 7
    mg_ref, st_ref = refs[n_in:n_in + 2]
    i = pl.program_id(0)
    tm = zc_ref.shape[0]
    xc = zc_ref[:, 0:CONV_DIM].astype(F32)
    bc = zc_ref[:, CONV_DIM:2 * CONV_DIM].astype(F32)
    cc = zc_ref[:, 2 * CONV_DIM:3 * CONV_DIM].astype(F32)
    u = cc * xc
    if has_hist:
        hist = refs[7][...]
        h0 = _seg_rows(hist[:, 0, :], seg)
        h1 = _seg_rows(hist[:, 1, :], seg)
        nseg = tm // seg
        st_ref[...] = u.reshape(nseg, seg, CONV_DIM)[:, seg - 2:seg, :]
    else:
        carry_scr = refs[n_in + 2]

        @pl.when(i == 0)
        def _():
            carry_scr[...] = jnp.zeros_like(carry_scr)
        h0 = carry_scr[0:1, :]
        h1 = carry_scr[1:2, :]
        st_ref[...] = u[tm - 2:tm, :][None]
    yc = (bc * _conv3(u, h0, h1, wc_ref[...], seg)).astype(BF16)
    if not has_hist:
        carry_scr[0:2, :] = u[tm - 2:tm, :]
    t1 = _dot(yc, wpc_ref[...])
    t2 = _dot(ya_ref[...], wpa_ref[...])
    mg_ref[...] = (gc_ref[...].astype(F32) * t1 + ga_ref[...].astype(F32) * t2).astype(BF16)


def _mix(z, ya, w_conv, w_pc, w_pa, layer, tm, seg, hist):
    rows = z.shape[0]
    nm = rows // tm
    has_hist = hist is not None
    nseg = tm // seg if has_hist else 1
    in_specs = [
        pl.BlockSpec((tm, 3 * CONV_DIM), lambda i: (i, 0)),
        pl.BlockSpec((tm, ATTN_DIM), lambda i: (i, 0)),
        pl.BlockSpec((tm, D_MODEL), lambda i: (i, G_OFF // D_MODEL)),
        pl.BlockSpec((tm, D_MODEL), lambda i: (i, G_OFF // D_MODEL + 1)),
        pl.BlockSpec((None, 3, CONV_DIM), lambda i: (layer, 0, 0)),
        pl.BlockSpec((None, CONV_DIM, D_MODEL), lambda i: (layer, 0, 0)),
        pl.BlockSpec((None, ATTN_DIM, D_MODEL), lambda i: (layer, 0, 0)),
    ]
    args = [z, ya, z, z, w_conv, w_pc, w_pa]
    scratch = []
    if has_hist:
        in_specs.append(pl.BlockSpec((nseg, 2, CONV_DIM), lambda i: (0, 0, 0)))
        args.append(hist)
    else:
        scratch.append(pltpu.VMEM((8, CONV_DIM), F32))
    return pl.pallas_call(
        functools.partial(_mix_body, seg=seg, has_hist=has_hist),
        grid=(nm,),
        in_specs=in_specs,
        out_specs=(pl.BlockSpec((tm, D_MODEL), lambda i: (i, 0)),
                   pl.BlockSpec((nseg, 2, CONV_DIM), lambda i: (0, 0, 0))),
        out_shape=(jax.ShapeDtypeStruct((rows, D_MODEL), BF16),
                   jax.ShapeDtypeStruct((nseg, 2, CONV_DIM), F32)),
        scratch_shapes=scratch,
        compiler_params=_params(1),
        name="mix",
    )(*args)


def _oproj_body(x_ref, mg_ref, wo_ref, o_ref):
    o_ref[...] = x_ref[...] + _dot(mg_ref[...], wo_ref[...])


def _oproj(x, mg, w_o, layer, tm):
    rows = x.shape[0]
    return pl.pallas_call(
        _oproj_body,
        grid=(rows // tm,),
        in_specs=[pl.BlockSpec((tm, D_MODEL), lambda i: (i, 0)),
                  pl.BlockSpec((tm, D_MODEL), lambda i: (i, 0)),
                  pl.BlockSpec((None, D_MODEL, D_MODEL), lambda i: (layer, 0, 0))],
        out_specs=pl.BlockSpec((tm, D_MODEL), lambda i: (i, 0)),
        out_shape=jax.ShapeDtypeStruct((rows, D_MODEL), F32),
        compiler_params=_params(1),
        name="oproj",
    )(x, mg, w_o)


def _ffn_body(*refs, seg, has_hist):
    (x_ref, g_ref, wua_ref, wub_ref, wca_ref, wcb_ref, ba_ref, bb_ref, wd_ref) = refs[:9]
    n_in = 11 if has_hist else 9
    o_ref, st_ref, h_scr, acc_scr = refs[n_in:n_in + 4]
    i = pl.program_id(0)
    j = pl.program_id(1)
    nj = pl.num_programs(1)
    tm = x_ref.shape[0]
    tf = wua_ref.shape[1]

    @pl.when(j == 0)
    def _():
        h_scr[...] = _rmsnorm(x_ref[...], g_ref[...]).astype(BF16)
        acc_scr[...] = jnp.zeros_like(acc_scr)

    hb = h_scr[...]
    ua = _dot(hb, wua_ref[...])
    ub = _dot(hb, wub_ref[...])
    if has_hist:
        nseg = tm // seg
        hists = []
        for part, (u, hist_ref) in enumerate(((ua, refs[9]), (ub, refs[10]))):
            hist = hist_ref[...]
            hists.append((_seg_rows(hist[:, 0, :], seg), _seg_rows(hist[:, 1, :], seg)))
            st_ref[:, :, part, :] = u.reshape(nseg, seg, tf)[:, seg - 2:seg, :]
    else:
        ca_scr, cb_scr = refs[n_in + 4:n_in + 6]

        @pl.when(i == 0)
        def _():
            ca_scr[j] = jnp.zeros((8, tf), F32)
            cb_scr[j] = jnp.zeros((8, tf), F32)
        hists = [(ca_scr[j, 0:1, :], ca_scr[j, 1:2, :]), (cb_scr[j, 0:1, :], cb_scr[j, 1:2, :])]
        st_ref[0, :, 0, :] = ua[tm - 2:tm, :]
        st_ref[0, :, 1, :] = ub[tm - 2:tm, :]
    a = _conv3(ua, hists[0][0], hists[0][1], wca_ref[...], seg) + ba_ref[...]
    b = _conv3(ub, hists[1][0], hists[1][1], wcb_ref[...], seg) + bb_ref[...]
    if not has_hist:
        ca_scr[j, 0:2, :] = ua[tm - 2:tm, :]
        cb_scr[j, 0:2, :] = ub[tm - 2:tm, :]
    gated = ((a * _sigmoid(a)) * b).astype(BF16)
    acc_scr[...] += _dot(gated, wd_ref[...])

    @pl.when(j == nj - 1)
    def _():
        o_ref[...] = x_ref[...] + acc_scr[...]


def _ffn(x, g, w_up, w_fconv, b_fconv, w_down, layer, tm, seg, hist):
    rows = x.shape[0]
    nm = rows // tm
    nf = D_FF // TF
    has_hist = hist is not None
    nseg = tm // seg if has_hist else 1
    in_specs = [
        pl.BlockSpec((tm, D_MODEL), lambda i, j: (i, 0), pipeline_mode=pl.Buffered(1)),
        pl.BlockSpec((None, 1, D_MODEL), lambda i, j: (layer, 0, 0)),
        pl.BlockSpec((None, D_MODEL, TF), lambda i, j: (layer, 0, j)),
        pl.BlockSpec((None, D_MODEL, TF), lambda i, j: (layer, 0, nf + j)),
        pl.BlockSpec((None, 3, TF), lambda i, j: (layer, 0, j)),
        pl.BlockSpec((None, 3, TF), lambda i, j: (layer, 0, nf + j)),
        pl.BlockSpec((None, 1, TF), lambda i, j: (layer, 0, j)),
        pl.BlockSpec((None, 1, TF), lambda i, j: (layer, 0, nf + j)),
        pl.BlockSpec((None, TF, D_MODEL), lambda i, j: (layer, j, 0)),
    ]
    args = [x, g, w_up, w_up, w_fconv, w_fconv, b_fconv, b_fconv, w_down]
    scratch = [pltpu.VMEM((tm, D_MODEL), BF16), pltpu.VMEM((tm, D_MODEL), F32)]
    if has_hist:
        in_specs += [pl.BlockSpec((nseg, 2, TF), lambda i, j: (0, 0, j)),
                     pl.BlockSpec((nseg, 2, TF), lambda i, j: (0, 0, nf + j))]
        args += [hist, hist]
    else:
        scratch += [pltpu.VMEM((nf, 8, TF), F32), pltpu.VMEM((nf, 8, TF), F32)]
    return pl.pallas_call(
        functools.partial(_ffn_body, seg=seg, has_hist=has_hist),
        grid=(nm, nf),
        in_specs=in_specs,
        out_specs=(pl.BlockSpec((tm, D_MODEL), lambda i, j: (i, 0), pipeline_mode=pl.Buffered(1)),
                   pl.BlockSpec((None, nseg, 2, 2, TF), lambda i, j: (i, 0, 0, 0, j))),
        out_shape=(jax.ShapeDtypeStruct((rows, D_MODEL), F32),
                   jax.ShapeDtypeStruct((nm, nseg, 2, 2, D_FF), F32)),
        scratch_shapes=scratch,
        compiler_params=_params(2),
        name="ffn",
    )(*args)


def _final_norm_body(x_ref, g_ref, o_ref):
    o_ref[...] = _rmsnorm(x_ref[...], g_ref[...])


def _final_norm(x, g, tm):
    rows = x.shape[0]
    return pl.pallas_call(
        _final_norm_body,
        grid=(rows // tm,),
        in_specs=[pl.BlockSpec((tm, D_MODEL), lambda i: (i, 0)),
                  pl.BlockSpec((1, D_MODEL), lambda i: (0, 0))],
        out_specs=pl.BlockSpec((tm, D_MODEL), lambda i: (i, 0)),
        out_shape=jax.ShapeDtypeStruct((rows, D_MODEL), F32),
        compiler_params=_params(1),
        name="final_norm",
    )(x, g)


def kernel(x_prompt, x_sample, cache_k, cache_v, cache_logf, state_conv, state_ffn_conv, meta_tokens,
           g_mix, w_in, b_in, w_conv, w_pc, w_pa, w_o, g_ffn, w_up, w_fconv, b_fconv, w_down, g_final):
    depth = w_in.shape[0]
    batch, seq, _ = x_prompt.shape
    nb, t_new, _ = x_sample.shape
    past = cache_k.shape[2]
    lp = N_META + seq
    assert batch == 1 and past % KV_CHUNK == 0 and lp % TM_BIG == 0 and lp % TM_SMALL == 0
    rows_s = nb * t_new

    w_main = jnp.concatenate([w_in[:, :, :F_OFF], w_in[:, :, F_OFF + N_HEADS:]], axis=2).astype(BF16)
    b_main = jnp.concatenate([b_in[:, :F_OFF], b_in[:, F_OFF + N_HEADS:]], axis=1)[:, None, :]
    w_f = jnp.pad(w_in[:, :, F_OFF:F_OFF + N_HEADS], ((0, 0), (0, 0), (0, LANES - N_HEADS))).astype(BF16)
    b_f = jnp.pad(b_in[:, F_OFF:F_OFF + N_HEADS], ((0, 0), (0, LANES - N_HEADS)))[:, None, :]
    w_pc_b, w_pa_b, w_o_b = w_pc.astype(BF16), w_pa.astype(BF16), w_o.astype(BF16)
    w_up_b, w_down_b = w_up.astype(BF16), w_down.astype(BF16)
    g_mix3, g_ffn3, b_fconv3 = g_mix[:, None, :], g_ffn[:, None, :], b_fconv[:, None, :]

    xp = jnp.concatenate([meta_tokens.astype(F32), x_prompt[0]], axis=0)
    xs = x_sample.reshape(rows_s, D_MODEL)
    ck = cache_k.reshape(depth, nb, past * N_HEADS, HEAD_DIM)
    cv = cache_v.reshape(depth, nb, past * N_HEADS, HEAD_DIM)
    lc_t = jnp.swapaxes(cache_logf.astype(F32), 2, 3)

    kv_p = kv_s = None
    lf_p, lf_s, cst_p, cst_s, fst_p, fst_s = [], [], [], [], [], []
    for l in range(depth):
        z, k_all, v_all, lf, ccol, crow = _inproj(xp, g_mix3, w_main, b_main, w_f, b_f, l, lp, TM_BIG,
                                                   TM_BIG, True, kv_p)
        kv_p = (k_all, v_all)
        ya = _fox_prompt(z, ccol, crow, TM_BIG)
        mg, cst = _mix(z, ya, w_conv, w_pc_b, w_pa_b, l, TM_SMALL, TM_SMALL, None)
        x1 = _oproj(xp, mg, w_o_b, l, TM_SMALL)
        xp, fst = _ffn(x1, g_ffn3, w_up_b, w_fconv, b_fconv3, w_down_b, l, TM_BIG, TM_BIG, None)
        lf_p.append(lf); cst_p.append(cst); fst_p.append(fst[-1].reshape(1, 2, 2 * D_FF))
        z, k_all, v_all, lf, _, crow = _inproj(xs, g_mix3, w_main, b_main, w_f, b_f, l, rows_s, rows_s,
                                               t_new, False, kv_s)
        kv_s = (k_all, v_all)
        cn_t = jnp.swapaxes(crow[0].reshape(N_HEADS, nb, t_new), 0, 1)
        ya = _fox_sample(z, cn_t, ck, cv, lc_t, l, t_new)
        mg, cst = _mix(z, ya, w_conv, w_pc_b, w_pa_b, l, rows_s, t_new, state_conv[l])
        x1 = _oproj(xs, mg, w_o_b, l, rows_s)
        xs, fst = _ffn(x1, g_ffn3, w_up_b, w_fconv, b_fconv3, w_down_b, l, rows_s, t_new,
                       state_ffn_conv[l])
        lf_s.append(lf); cst_s.append(cst); fst_s.append(fst[-1].reshape(nb, 2, 2 * D_FF))

    g_fin = g_final[None, :]
    y_prompt = _final_norm(xp, g_fin, TM_BIG)[N_META:][None]
    y_sample = _final_norm(xs, g_fin, rows_s).reshape(nb, t_new, D_MODEL)
    return (y_prompt, y_sample,
            kv_p[0].reshape(depth, 1, lp, N_HEADS, HEAD_DIM),
            kv_p[1].reshape(depth, 1, lp, N_HEADS, HEAD_DIM),
            jnp.stack(lf_p).reshape(depth, 1, lp, N_HEADS),
            jnp.stack(cst_p),
            jnp.stack(fst_p),
            kv_s[0].reshape(depth, nb, t_new, N_HEADS, HEAD_DIM),
            kv_s[1].reshape(depth, nb, t_new, N_HEADS, HEAD_DIM),
            jnp.stack(lf_s).reshape(depth, nb, t_new, N_HEADS),
            jnp.stack(cst_s),
            jnp.stack(fst_s))
```

```python
import functools

import jax
import jax.numpy as jnp
from jax import lax
from jax.experimental import pallas as pl
from jax.experimental.pallas import tpu as pltpu

F32 = jnp.float32
BF16 = jnp.bfloat16

D_MODEL = 2048
N_META = 16
CONV_DIM = D_MODEL // 2
HEAD_DIM = 128
ATTN_DIM = D_MODEL // 2
N_HEADS = ATTN_DIM // HEAD_DIM
D_FF = 2 * D_MODEL
EPS = 1e-6
ATTN_SCALE = HEAD_DIM ** -0.5
NEG_INF = -1e30
EXP_ZERO_BELOW = 106.0
F_OFF = 3 * CONV_DIM + 3 * ATTN_DIM
MAIN_DIM = F_OFF + 2 * D_MODEL

LANES = 128
VMEM_LIMIT = 60000 * 1024

Q_OFF = 3 * CONV_DIM
K_OFF = Q_OFF + ATTN_DIM
V_OFF = K_OFF + ATTN_DIM
G_OFF = V_OFF + ATTN_DIM

TN_IN = 512
TF = 512
TM_BIG = 912
TM_SMALL = 432
TM_OUT = 512
ROW_CHUNK = 304
KV_CHUNK = 1024
SUFFIX_BLOCK = 256


def _params(n_axes):
    return pltpu.CompilerParams(dimension_semantics=("arbitrary",) * n_axes,
                                vmem_limit_bytes=VMEM_LIMIT)


def _dot(a, b):
    return jnp.dot(a, b, preferred_element_type=F32)


def _dot_nt(a, b):
    return lax.dot_general(a, b, (((1,), (1,)), ((), ())), preferred_element_type=F32)


def _dot_exact01(mask_bf16, x, nt=False):
    hi = x.astype(BF16)
    r1 = x - hi.astype(F32)
    mid = r1.astype(BF16)
    lo = (r1 - mid.astype(F32)).astype(BF16)
    if nt:
        return _dot(hi, mask_bf16) + _dot(mid, mask_bf16) + _dot(lo, mask_bf16)
    return _dot(mask_bf16, hi) + _dot(mask_bf16, mid) + _dot(mask_bf16, lo)


def _rmsnorm(x, g):
    ms = jnp.mean(x * x, axis=-1, keepdims=True)
    return (x * lax.rsqrt(ms + EPS)) * g


def _sigmoid(x):
    return 1.0 / (1.0 + jnp.exp(-x))


def _conv3(u, h0rows, h1rows, w, seg):
    tm = u.shape[0]
    r = lax.broadcasted_iota(jnp.int32, (tm, 1), 0)
    if seg < tm:
        r = r % seg
    um1 = jnp.where(r == 0, h1rows, pltpu.roll(u, 1, 0))
    um2 = jnp.where(r == 0, h0rows, jnp.where(r == 1, h1rows, pltpu.roll(u, 2, 0)))
    return um2 * w[0:1] + um1 * w[1:2] + u * w[2:3]


def _seg_rows(h, seg):
    nseg, c = h.shape
    return jnp.broadcast_to(h[:, None, :], (nseg, seg, c)).reshape(nseg * seg, c)


def _inproj_body(*refs, seg, use_carry, n_alias):
    x_ref, g_ref, w_ref, b_ref, wf_ref, bf_ref = refs[:6]
    (z_ref, k_ref, v_ref, lf_ref, ccol_ref, crow_ref, h_scr, carry_scr) = refs[6 + n_alias:]
    i = pl.program_id(0)
    j = pl.program_id(1)
    tm = x_ref.shape[0]
    jq, jk, jv, jg = (Q_OFF // TN_IN, K_OFF // TN_IN, V_OFF // TN_IN, G_OFF // TN_IN)

    @pl.when(j == 0)
    def _():
        hb = _rmsnorm(x_ref[...], g_ref[...]).astype(BF16)
        h_scr[...] = hb
        fl = _dot(hb, wf_ref[...]) + bf_ref[...]
        lf = jnp.minimum(fl, 0.0) - jnp.log1p(jnp.exp(-jnp.abs(fl)))
        lf_ref[...] = lf[:, :N_HEADS]
        r = lax.broadcasted_iota(jnp.int32, (tm, tm), 0)
        c = lax.broadcasted_iota(jnp.int32, (tm, tm), 1)
        tri = c <= r
        if seg < tm:
            tri = tri & ((r // seg) == (c // seg))
        cum = _dot_exact01(jnp.where(tri, 1.0, 0.0).astype(BF16), lf)
        if use_carry:
            @pl.when(i == 0)
            def _():
                carry_scr[...] = jnp.zeros_like(carry_scr)
            cum = cum + carry_scr[...]
            carry_scr[...] = cum[tm - 1:tm, :]
        ccol_ref[...] = cum
        crow_ref[...] = cum.T[:N_HEADS, :]

    acc = _dot(h_scr[...], w_ref[...]) + b_ref[...]
    is_q = (j >= jq) & (j < jk)
    is_gate = j >= jg

    @pl.when(is_q)
    def _():
        z_ref[...] = (acc * ATTN_SCALE).astype(BF16)

    @pl.when(is_gate)
    def _():
        z_ref[...] = _sigmoid(acc).astype(BF16)

    @pl.when(jnp.logical_not(is_q | is_gate))
    def _():
        z_ref[...] = acc.astype(BF16)

    @pl.when((j >= jk) & (j < jv))
    def _():
        k_ref[...] = acc

    @pl.when((j >= jv) & (j < jg))
    def _():
        v_ref[...] = acc


def _inproj(x, g, w_main, b_main, w_f, b_f, layer, n_valid, tm, seg, use_carry, kv_prev):
    rows = x.shape[0]
    nm = rows // tm
    nj = MAIN_DIM // TN_IN
    jk, jv = K_OFF // TN_IN, V_OFF // TN_IN
    nkv = ATTN_DIM // TN_IN
    depth = w_main.shape[0]
    n_alias = 0 if kv_prev is None else 2
    in_specs = [
        pl.BlockSpec((tm, D_MODEL), lambda i, j: (i, 0)),
        pl.BlockSpec((None, 1, D_MODEL), lambda i, j: (layer, 0, 0)),
        pl.BlockSpec((None, D_MODEL, TN_IN), lambda i, j: (layer, 0, j)),
        pl.BlockSpec((None, 1, TN_IN), lambda i, j: (layer, 0, j)),
        pl.BlockSpec((None, D_MODEL, LANES), lambda i, j: (layer, 0, 0)),
        pl.BlockSpec((None, 1, LANES), lambda i, j: (layer, 0, 0)),
    ]
    args = [x, g, w_main, b_main, w_f, b_f]
    aliases = {}
    if kv_prev is not None:
        in_specs += [pl.BlockSpec(memory_space=pl.ANY)] * 2
        args += list(kv_prev)
        aliases = {6: 1, 7: 2}
    out_shape = (
        jax.ShapeDtypeStruct((rows, MAIN_DIM), BF16),
        jax.ShapeDtypeStruct((depth, n_valid, ATTN_DIM), F32),
        jax.ShapeDtypeStruct((depth, n_valid, ATTN_DIM), F32),
        jax.ShapeDtypeStruct((n_valid, N_HEADS), F32),
        jax.ShapeDtypeStruct((rows, LANES), F32),
        jax.ShapeDtypeStruct((nm, N_HEADS, tm), F32),
    )
    out_specs = (
        pl.BlockSpec((tm, TN_IN), lambda i, j: (i, j)),
        pl.BlockSpec((None, tm, TN_IN), lambda i, j: (layer, i, jnp.clip(j - jk, 0, nkv - 1))),
        pl.BlockSpec((None, tm, TN_IN), lambda i, j: (layer, i, jnp.clip(j - jv, 0, nkv - 1))),
        pl.BlockSpec((tm, N_HEADS), lambda i, j: (i, 0)),
        pl.BlockSpec((tm, LANES), lambda i, j: (i, 0)),
        pl.BlockSpec((None, N_HEADS, tm), lambda i, j: (i, 0, 0)),
    )
    return pl.pallas_call(
        functools.partial(_inproj_body, seg=seg, use_carry=use_carry, n_alias=n_alias),
        grid=(nm, nj),
        in_specs=in_specs,
        out_specs=out_specs,
        out_shape=out_shape,
        scratch_shapes=[pltpu.VMEM((tm, D_MODEL), BF16), pltpu.VMEM((1, LANES), F32)],
        input_output_aliases=aliases,
        compiler_params=_params(2),
        name="inproj",
    )(*args)


def _fox_plan_body(q_ref, k_ref, crow_ref, o_ref):
    tq = q_ref.shape[0]

    def max_norm(ref):
        cols = []
        for h in range(N_HEADS):
            x = ref[:, h * HEAD_DIM:(h + 1) * HEAD_DIM].astype(F32)
            cols.append(jnp.sqrt(jnp.max(jnp.sum(x * x, axis=1, keepdims=True), axis=0, keepdims=True)))
        return jnp.concatenate(cols, axis=0)

    lane = lax.broadcasted_iota(jnp.int32, (N_HEADS, LANES), 1)
    crow = crow_ref[...]
    o_ref[...] = jnp.where(lane == 0, max_norm(q_ref),
                           jnp.where(lane == 1, max_norm(k_ref),
                                     jnp.where(lane == 2, crow[:, 0:1],
                                               jnp.where(lane == 3, crow[:, tq - 1:tq], 0.0))))


def _fox_plan(z, crow, tq):
    rows = z.shape[0]
    nq = rows // tq
    return pl.pallas_call(
        _fox_plan_body,
        grid=(nq,),
        in_specs=[pl.BlockSpec((tq, ATTN_DIM), lambda i: (i, Q_OFF // ATTN_DIM)),
                  pl.BlockSpec((tq, ATTN_DIM), lambda i: (i, K_OFF // ATTN_DIM)),
                  pl.BlockSpec((None, N_HEADS, tq), lambda i: (i, 0, 0))],
        out_specs=pl.BlockSpec((None, N_HEADS, LANES), lambda i: (i, 0, 0)),
        out_shape=jax.ShapeDtypeStruct((nq, N_HEADS, LANES), F32),
        compiler_params=_params(1),
        name="fox_plan",
    )(z, z, crow)


def _tiles_to_visit(plan):
    qn, kn, c_first, c_last = (plan[:, :, i].T for i in range(4))
    nq = qn.shape[1]
    gap = (qn[:, :, None] * (kn[:, None, :] + kn[:, :, None])
           + c_first[:, :, None] - c_last[:, None, :])
    qi = jnp.arange(nq)[:, None]
    kb = jnp.arange(nq)[None, :]
    needed = (kb < qi)[None] & (gap >= -EXP_ZERO_BELOW)
    return 1 + jnp.max(jnp.where(needed, (qi - kb)[None], 0), axis=2).astype(jnp.int32)


def _fox_prompt_body(nvisit_ref, q_ref, k_ref, v_ref, ccol_ref, crow_ref, o_ref, m_scr, l_scr, acc_scr, cq_scr):
    h = pl.program_id(0)
    qi = pl.program_id(1)
    tq = q_ref.shape[0]
    tk = tq
    m_scr[...] = jnp.full_like(m_scr, NEG_INF)
    l_scr[...] = jnp.zeros_like(l_scr)
    acc_scr[...] = jnp.zeros_like(acc_scr)
    lane = lax.broadcasted_iota(jnp.int32, (tq, LANES), 1)
    cq_scr[...] = jnp.sum(jnp.where(lane == h, ccol_ref[...], 0.0), axis=1, keepdims=True)

    def tile(kb, diagonal):
        start = pl.multiple_of(kb * tk, ROW_CHUNK)
        k = k_ref[pl.ds(start, tk), :]
        v = v_ref[pl.ds(start, tk), :]
        ck = crow_ref[kb, pl.ds(h, 1), :]
        chunks = list(range(0, tq, ROW_CHUNK))

        def n_keys(r0):
            return r0 + ROW_CHUNK if diagonal else tk

        def scores(r0):
            nk = n_keys(r0)
            return _dot_nt(q_ref[r0:r0 + ROW_CHUNK, :], k[:nk])

        def softmax(r0, s):
            rows = slice(r0, r0 + ROW_CHUNK)
            nk = n_keys(r0)
            s = s + cq_scr[rows, :] - ck[:, :nk]
            if diagonal:
                r = lax.broadcasted_iota(jnp.int32, (ROW_CHUNK, nk), 0) + r0
                c = lax.broadcasted_iota(jnp.int32, (ROW_CHUNK, nk), 1)
                s = jnp.where(c <= r, s, NEG_INF)
            m_prev = m_scr[rows, :]
            m_new = jnp.maximum(m_prev, jnp.max(s, axis=1, keepdims=True))
            alpha = jnp.exp(m_prev - m_new)
            p = jnp.exp(s - m_new)
            l_scr[rows, :] = alpha * l_scr[rows, :] + jnp.sum(p, axis=1, keepdims=True)
            m_scr[rows, :] = m_new
            return alpha, p.astype(BF16)

        def values(r0, alpha, p):
            rows = slice(r0, r0 + ROW_CHUNK)
            acc_scr[rows, :] = alpha * acc_scr[rows, :] + _dot(p, v[:n_keys(r0)])

        s_next = scores(chunks[0])
        pending = None
        for idx, r0 in enumerate(chunks):
            s_cur = s_next
            if idx + 1 < len(chunks):
                s_next = scores(chunks[idx + 1])
            alpha, p = softmax(r0, s_cur)
            if pending is not None:
                values(*pending)
            pending = (r0, alpha, p)
        values(*pending)

    tile(qi, True)

    def older(kk, carry):
        tile(qi - kk, False)
        return carry

    lax.fori_loop(1, nvisit_ref[h, qi], older, 0)
    o_ref[...] = (acc_scr[...] / l_scr[...]).astype(BF16)


def _fox_prompt(z, ccol, crow, tq):
    rows = z.shape[0]
    nq = rows // tq
    qb, kb, vb = Q_OFF // HEAD_DIM, K_OFF // HEAD_DIM, V_OFF // HEAD_DIM
    nvisit = _tiles_to_visit(_fox_plan(z, crow, tq))
    return pl.pallas_call(
        _fox_prompt_body,
        grid_spec=pltpu.PrefetchScalarGridSpec(
            num_scalar_prefetch=1,
            grid=(N_HEADS, nq),
            in_specs=[
                pl.BlockSpec((tq, HEAD_DIM), lambda h, qi, nv: (qi, qb + h)),
                pl.BlockSpec((rows, HEAD_DIM), lambda h, qi, nv: (0, kb + h)),
                pl.BlockSpec((rows, HEAD_DIM), lambda h, qi, nv: (0, vb + h)),
                pl.BlockSpec((tq, LANES), lambda h, qi, nv: (qi, 0)),
                pl.BlockSpec((nq, N_HEADS, tq), lambda h, qi, nv: (0, 0, 0)),
            ],
            out_specs=pl.BlockSpec((tq, HEAD_DIM), lambda h, qi, nv: (qi, h)),
            scratch_shapes=[pltpu.VMEM((tq, 1), F32), pltpu.VMEM((tq, 1), F32),
                            pltpu.VMEM((tq, HEAD_DIM), F32), pltpu.VMEM((tq, 1), F32)]),
        out_shape=jax.ShapeDtypeStruct((rows, ATTN_DIM), BF16),
        compiler_params=_params(2),
        name="fox_prompt",
    )(nvisit, z, z, z, ccol, crow)


def _fox_sample_body(*refs):
    (q_ref, kn_ref, vn_ref, cn_ref, lc_ref, ck_ref, cv_ref, o_ref,
     qbd_scr, m_scr, l_scr, acc_scr, tail_scr) = refs
    c = pl.program_id(1)
    nc = pl.num_programs(1)
    t_new = q_ref.shape[0]
    nrow = t_new * N_HEADS
    kc = ck_ref.shape[0] // N_HEADS
    nblk = kc // SUFFIX_BLOCK
    cn = cn_ref[...]
    cq = jnp.concatenate([cn[:, t:t + 1] for t in range(t_new)], axis=0)
    col_head = lax.broadcasted_iota(jnp.int32, (N_HEADS, ATTN_DIM), 1) // HEAD_DIM
    row_head = lax.broadcasted_iota(jnp.int32, (N_HEADS, ATTN_DIM), 0)
    head_mask = col_head == row_head

    @pl.when(c == 0)
    def _():
        q = q_ref[...].astype(F32)
        q3 = jnp.where(head_mask[None], jnp.broadcast_to(q[:, None, :], (t_new, N_HEADS, ATTN_DIM)), 0.0)
        qbd_scr[...] = q3.reshape(nrow, ATTN_DIM).astype(BF16)
        m_scr[...] = jnp.full_like(m_scr, NEG_INF)
        l_scr[...] = jnp.zeros_like(l_scr)
        acc_scr[...] = jnp.zeros_like(acc_scr)
        tail_scr[...] = jnp.zeros_like(tail_scr)

    def online(s, vals):
        m_prev = m_scr[...]
        m_new = jnp.maximum(m_prev, jnp.max(s, axis=1, keepdims=True))
        alpha = jnp.exp(m_prev - m_new)
        p = jnp.exp(s - m_new)
        l_scr[...] = alpha * l_scr[...] + jnp.sum(p, axis=1, keepdims=True)
        acc_scr[...] = alpha * acc_scr[...] + _dot(p.astype(BF16), vals)
        m_scr[...] = m_new

    lc = lc_ref[...]
    x = jnp.concatenate([lc[:, b * SUFFIX_BLOCK:(b + 1) * SUFFIX_BLOCK] for b in range(nblk)], axis=0)
    jj = lax.broadcasted_iota(jnp.int32, (SUFFIX_BLOCK, SUFFIX_BLOCK), 0)
    ss = lax.broadcasted_iota(jnp.int32, (SUFFIX_BLOCK, SUFFIX_BLOCK), 1)
    excl = _dot_exact01(jnp.where(jj > ss, 1.0, 0.0).astype(BF16), x, nt=True)
    off = tail_scr[...]
    pieces = [None] * nblk
    for b in range(nblk - 1, -1, -1):
        eb = excl[b * N_HEADS:(b + 1) * N_HEADS, :]
        pieces[b] = eb + off
        off = off + (eb[:, 0:1] + x[b * N_HEADS:(b + 1) * N_HEADS, 0:1])
    tail_scr[...] = off
    suffix = jnp.concatenate(pieces, axis=1)

    def all_heads(ref):
        return jnp.concatenate([ref[pl.ds(h, kc, stride=N_HEADS), :].astype(BF16)
                                for h in range(N_HEADS)], axis=1)

    s = _dot_nt(qbd_scr[...], all_heads(ck_ref))
    s = (s.reshape(t_new, N_HEADS, kc) + cq.reshape(t_new, N_HEADS, 1) + suffix[None]).reshape(nrow, kc)
    online(s, all_heads(cv_ref))

    @pl.when(c == nc - 1)
    def _():
        sn = _dot_nt(qbd_scr[...], kn_ref[...])
        sn = sn.reshape(t_new, N_HEADS, t_new) + cq.reshape(t_new, N_HEADS, 1) + (-cn)[None]
        tq = lax.broadcasted_iota(jnp.int32, (t_new, N_HEADS, t_new), 0)
        tk = lax.broadcasted_iota(jnp.int32, (t_new, N_HEADS, t_new), 2)
        sn = jnp.where(tk <= tq, sn, NEG_INF).reshape(nrow, t_new)
        online(sn, vn_ref[...])
        o = (acc_scr[...] / l_scr[...]).reshape(t_new, N_HEADS, ATTN_DIM)
        o_ref[...] = jnp.sum(jnp.where(head_mask[None], o, 0.0), axis=1).astype(BF16)


def _fox_sample(z, cn_t, cache_k, cache_v, lc_t, layer, t_new):
    nb = cache_k.shape[1]
    past = cache_k.shape[2] // N_HEADS
    nc = past // KV_CHUNK
    nrow = t_new * N_HEADS
    qb, kb, vb = Q_OFF // ATTN_DIM, K_OFF // ATTN_DIM, V_OFF // ATTN_DIM
    cache_spec = pl.BlockSpec((None, None, KV_CHUNK * N_HEADS, HEAD_DIM),
                              lambda b, c: (layer, b, nc - 1 - c, 0))
    return pl.pallas_call(
        _fox_sample_body,
        grid=(nb, nc),
        in_specs=[
            pl.BlockSpec((t_new, ATTN_DIM), lambda b, c: (b, qb)),
            pl.BlockSpec((t_new, ATTN_DIM), lambda b, c: (b, kb)),
            pl.BlockSpec((t_new, ATTN_DIM), lambda b, c: (b, vb)),
            pl.BlockSpec((None, N_HEADS, t_new), lambda b, c: (b, 0, 0)),
            pl.BlockSpec((None, None, N_HEADS, KV_CHUNK), lambda b, c: (layer, b, 0, nc - 1 - c)),
            cache_spec, cache_spec,
        ],
        out_specs=pl.BlockSpec((t_new, ATTN_DIM), lambda b, c: (b, 0)),
        out_shape=jax.ShapeDtypeStruct((nb * t_new, ATTN_DIM), BF16),
        scratch_shapes=[pltpu.VMEM((nrow, ATTN_DIM), BF16), pltpu.VMEM((nrow, 1), F32),
                        pltpu.VMEM((nrow, 1), F32), pltpu.VMEM((nrow, ATTN_DIM), F32),
                        pltpu.VMEM((N_HEADS, 1), F32)],
        compiler_params=_params(2),
        name="fox_sample",
    )(z, z, z, cn_t, lc_t, cache_k, cache_v)


def _mix_body(*refs, seg, has_hist):
    zc_ref, ya_ref, gc_ref, ga_ref, wc_ref, wpc_ref, wpa_ref = refs[:7]
    n_in = 8 if has_hist else 7
    mg_ref, st_ref = refs[n_in:n_in + 2]
    i = pl.program_id(0)
    tm = zc_ref.shape[0]
    xc = zc_ref[:, 0:CONV_DIM].astype(F32)
    bc = zc_ref[:, CONV_DIM:2 * CONV_DIM].astype(F32)
    cc = zc_ref[:, 2 * CONV_DIM:3 * CONV_DIM].astype(F32)
    u = cc * xc
    if has_hist:
        hist = refs[7][...]
        h0 = _seg_rows(hist[:, 0, :], seg)
        h1 = _seg_rows(hist[:, 1, :], seg)
        nseg = tm // seg
        st_ref[...] = u.reshape(nseg, seg, CONV_DIM)[:, seg - 2:seg, :]
    else:
        carry_scr = refs[n_in + 2]

        @pl.when(i == 0)
        def _():
            carry_scr[...] = jnp.zeros_like(carry_scr)
        h0 = carry_scr[0:1, :]
        h1 = carry_scr[1:2, :]
        st_ref[...] = u[tm - 2:tm, :][None]
    yc = (bc * _conv3(u, h0, h1, wc_ref[...], seg)).astype(BF16)
    if not has_hist:
        carry_scr[0:2, :] = u[tm - 2:tm, :]
    t1 = _dot(yc, wpc_ref[...])
    t2 = _dot(ya_ref[...], wpa_ref[...])
    mg_ref[...] = (gc_ref[...].astype(F32) * t1 + ga_ref[...].astype(F32) * t2).astype(BF16)


def _mix(z, ya, w_conv, w_pc, w_pa, layer, tm, seg, hist):
    rows = z.shape[0]
    nm = rows // tm
    has_hist = hist is not None
    nseg = tm // seg if has_hist else 1
    in_specs = [
        pl.BlockSpec((tm, 3 * CONV_DIM), lambda i: (i, 0)),
        pl.BlockSpec((tm, ATTN_DIM), lambda i: (i, 0)),
        pl.BlockSpec((tm, D_MODEL), lambda i: (i, G_OFF // D_MODEL)),
        pl.BlockSpec((tm, D_MODEL), lambda i: (i, G_OFF // D_MODEL + 1)),
        pl.BlockSpec((None, 3, CONV_DIM), lambda i: (layer, 0, 0)),
        pl.BlockSpec((None, CONV_DIM, D_MODEL), lambda i: (layer, 0, 0)),
        pl.BlockSpec((None, ATTN_DIM, D_MODEL), lambda i: (layer, 0, 0)),
    ]
    args = [z, ya, z, z, w_conv, w_pc, w_pa]
    scratch = []
    if has_hist:
        in_specs.append(pl.BlockSpec((nseg, 2, CONV_DIM), lambda i: (0, 0, 0)))
        args.append(hist)
    else:
        scratch.append(pltpu.VMEM((8, CONV_DIM), F32))
    return pl.pallas_call(
        functools.partial(_mix_body, seg=seg, has_hist=has_hist),
        grid=(nm,),
        in_specs=in_specs,
        out_specs=(pl.BlockSpec((tm, D_MODEL), lambda i: (i, 0)),
                   pl.BlockSpec((nseg, 2, CONV_DIM), lambda i: (0, 0, 0))),
        out_shape=(jax.ShapeDtypeStruct((rows, D_MODEL), BF16),
                   jax.ShapeDtypeStruct((nseg, 2, CONV_DIM), F32)),
        scratch_shapes=scratch,
        compiler_params=_params(1),
        name="mix",
    )(*args)


def _oproj_body(x_ref, mg_ref, wo_ref, o_ref):
    o_ref[...] = x_ref[...] + _dot(mg_ref[...], wo_ref[...])


def _oproj(x, mg, w_o, layer, tm):
    rows = x.shape[0]
    return pl.pallas_call(
        _oproj_body,
        grid=(rows // tm,),
        in_specs=[pl.BlockSpec((tm, D_MODEL), lambda i: (i, 0)),
                  pl.BlockSpec((tm, D_MODEL), lambda i: (i, 0)),
                  pl.BlockSpec((None, D_MODEL, D_MODEL), lambda i: (layer, 0, 0))],
        out_specs=pl.BlockSpec((tm, D_MODEL), lambda i: (i, 0)),
        out_shape=jax.ShapeDtypeStruct((rows, D_MODEL), F32),
        compiler_params=_params(1),
        name="oproj",
    )(x, mg, w_o)


def _ffn_body(*refs, seg, has_hist):
    (x_ref, g_ref, wua_ref, wub_ref, wca_ref, wcb_ref, ba_ref, bb_ref, wd_ref) = refs[:9]
    n_in = 11 if has_hist else 9
    o_ref, st_ref, h_scr, acc_scr = refs[n_in:n_in + 4]
    i = pl.program_id(0)
    j = pl.program_id(1)
    nj = pl.num_programs(1)
    tm = x_ref.shape[0]
    tf = wua_ref.shape[1]

    @pl.when(j == 0)
    def _():
        h_scr[...] = _rmsnorm(x_ref[...], g_ref[...]).astype(BF16)
        acc_scr[...] = jnp.zeros_like(acc_scr)

    rc = ROW_CHUNK if (not has_hist and tm % ROW_CHUNK == 0) else tm
    chunks = list(range(0, tm, rc))

    def up(r0):
        hb = h_scr[r0:r0 + rc, :]
        return _dot(hb, wua_ref[...]), _dot(hb, wub_ref[...])

    def gate(u, hist, w_ref, bias_ref):
        return _conv3(u, hist[0], hist[1], w_ref[...], min(seg, rc)) + bias_ref[...]

    def down(r0, gated):
        acc_scr[r0:r0 + rc, :] += _dot(gated, wd_ref[...])

    if has_hist:
        nseg = tm // seg
        hist_a, hist_b = ((_seg_rows(r[:, 0, :], seg), _seg_rows(r[:, 1, :], seg))
                          for r in (refs[9][...], refs[10][...]))
    else:
        ca_scr, cb_scr = refs[n_in + 4:n_in + 6]

        @pl.when(i == 0)
        def _():
            ca_scr[j] = jnp.zeros((8, tf), F32)
            cb_scr[j] = jnp.zeros((8, tf), F32)
        hist_a = (ca_scr[j, 0:1, :], ca_scr[j, 1:2, :])
        hist_b = (cb_scr[j, 0:1, :], cb_scr[j, 1:2, :])

    u_next = up(chunks[0])
    pending = None
    for idx, r0 in enumerate(chunks):
        ua, ub = u_next
        if idx + 1 < len(chunks):
            u_next = up(chunks[idx + 1])
        a = gate(ua, hist_a, wca_ref, ba_ref)
        b = gate(ub, hist_b, wcb_ref, bb_ref)
        gated = ((a * _sigmoid(a)) * b).astype(BF16)
        hist_a = (ua[rc - 2:rc - 1, :], ua[rc - 1:rc, :])
        hist_b = (ub[rc - 2:rc - 1, :], ub[rc - 1:rc, :])
        if pending is not None:
            down(*pending)
        pending = (r0, gated)
    down(*pending)

    if has_hist:
        st_ref[:, :, 0, :] = ua.reshape(nseg, seg, tf)[:, seg - 2:seg, :]
        st_ref[:, :, 1, :] = ub.reshape(nseg, seg, tf)[:, seg - 2:seg, :]
    else:
        st_ref[0, :, 0, :] = ua[rc - 2:rc, :]
        st_ref[0, :, 1, :] = ub[rc - 2:rc, :]
        ca_scr[j, 0:2, :] = ua[rc - 2:rc, :]
        cb_scr[j, 0:2, :] = ub[rc - 2:rc, :]

    @pl.when(j == nj - 1)
    def _():
        o_ref[...] = x_ref[...] + acc_scr[...]


def _ffn(x, g, w_up, w_fconv, b_fconv, w_down, layer, tm, seg, hist):
    rows = x.shape[0]
    nm = rows // tm
    nf = D_FF // TF
    has_hist = hist is not None
    nseg = tm // seg if has_hist else 1
    in_specs = [
        pl.BlockSpec((tm, D_MODEL), lambda i, j: (i, 0), pipeline_mode=pl.Buffered(1)),
        pl.BlockSpec((None, 1, D_MODEL), lambda i, j: (layer, 0, 0)),
        pl.BlockSpec((None, D_MODEL, TF), lambda i, j: (layer, 0, j)),
        pl.BlockSpec((None, D_MODEL, TF), lambda i, j: (layer, 0, nf + j)),
        pl.BlockSpec((None, 3, TF), lambda i, j: (layer, 0, j)),
        pl.BlockSpec((None, 3, TF), lambda i, j: (layer, 0, nf + j)),
        pl.BlockSpec((None, 1, TF), lambda i, j: (layer, 0, j)),
        pl.BlockSpec((None, 1, TF), lambda i, j: (layer, 0, nf + j)),
        pl.BlockSpec((None, TF, D_MODEL), lambda i, j: (layer, j, 0)),
    ]
    args = [x, g, w_up, w_up, w_fconv, w_fconv, b_fconv, b_fconv, w_down]
    scratch = [pltpu.VMEM((tm, D_MODEL), BF16), pltpu.VMEM((tm, D_MODEL), F32)]
    if has_hist:
        in_specs += [pl.BlockSpec((nseg, 2, TF), lambda i, j: (0, 0, j)),
                     pl.BlockSpec((nseg, 2, TF), lambda i, j: (0, 0, nf + j))]
        args += [hist, hist]
    else:
        scratch += [pltpu.VMEM((nf, 8, TF), F32), pltpu.VMEM((nf, 8, TF), F32)]
    return pl.pallas_call(
        functools.partial(_ffn_body, seg=seg, has_hist=has_hist),
        grid=(nm, nf),
        in_specs=in_specs,
        out_specs=(pl.BlockSpec((tm, D_MODEL), lambda i, j: (i, 0), pipeline_mode=pl.Buffered(1)),
                   pl.BlockSpec((None, nseg, 2, 2, TF), lambda i, j: (i, 0, 0, 0, j))),
        out_shape=(jax.ShapeDtypeStruct((rows, D_MODEL), F32),
                   jax.ShapeDtypeStruct((nm, nseg, 2, 2, D_FF), F32)),
        scratch_shapes=scratch,
        compiler_params=_params(2),
        name="ffn",
    )(*args)


def _final_norm_body(x_ref, g_ref, o_ref):
    o_ref[...] = _rmsnorm(x_ref[...], g_ref[...])


def _final_norm(x, g, tm, skip_rows):
    rows = x.shape[0] - skip_rows
    return pl.pallas_call(
        _final_norm_body,
        grid=(rows // tm,),
        in_specs=[pl.BlockSpec((pl.Element(tm), pl.Element(D_MODEL)), lambda i: (pl.multiple_of(i * tm + skip_rows, 8), 0)),
                  pl.BlockSpec((1, D_MODEL), lambda i: (0, 0))],
        out_specs=pl.BlockSpec((tm, D_MODEL), lambda i: (i, 0)),
        out_shape=jax.ShapeDtypeStruct((rows, D_MODEL), F32),
        compiler_params=_params(1),
        name="final_norm",
    )(x, g)


def kernel(x_prompt, x_sample, cache_k, cache_v, cache_logf, state_conv, state_ffn_conv, meta_tokens,
           g_mix, w_in, b_in, w_conv, w_pc, w_pa, w_o, g_ffn, w_up, w_fconv, b_fconv, w_down, g_final):
    depth = w_in.shape[0]
    batch, seq, _ = x_prompt.shape
    nb, t_new, _ = x_sample.shape
    past = cache_k.shape[2]
    lp = N_META + seq
    assert batch == 1 and past % KV_CHUNK == 0 and lp % TM_BIG == 0 and lp % TM_SMALL == 0
    rows_s = nb * t_new

    w_main = jnp.concatenate([w_in[:, :, :F_OFF], w_in[:, :, F_OFF + N_HEADS:]], axis=2).astype(BF16)
    b_main = jnp.concatenate([b_in[:, :F_OFF], b_in[:, F_OFF + N_HEADS:]], axis=1)[:, None, :]
    w_f = jnp.pad(w_in[:, :, F_OFF:F_OFF + N_HEADS], ((0, 0), (0, 0), (0, LANES - N_HEADS))).astype(BF16)
    b_f = jnp.pad(b_in[:, F_OFF:F_OFF + N_HEADS], ((0, 0), (0, LANES - N_HEADS)))[:, None, :]
    w_pc_b, w_pa_b, w_o_b = w_pc.astype(BF16), w_pa.astype(BF16), w_o.astype(BF16)
    w_up_b, w_down_b = w_up.astype(BF16), w_down.astype(BF16)
    g_mix3, g_ffn3, b_fconv3 = g_mix[:, None, :], g_ffn[:, None, :], b_fconv[:, None, :]

    xp = jnp.concatenate([meta_tokens.astype(F32), x_prompt[0]], axis=0)
    xs = x_sample.reshape(rows_s, D_MODEL)
    ck = cache_k.reshape(depth, nb, past * N_HEADS, HEAD_DIM)
    cv = cache_v.reshape(depth, nb, past * N_HEADS, HEAD_DIM)
    lc_t = jnp.swapaxes(cache_logf.astype(F32), 2, 3)

    kv_p = kv_s = None
    lf_p, lf_s, cst_p, cst_s, fst_p, fst_s = [], [], [], [], [], []
    for l in range(depth):
        z, k_all, v_all, lf, ccol, crow = _inproj(xp, g_mix3, w_main, b_main, w_f, b_f, l, lp, TM_BIG,
                                                   TM_BIG, True, kv_p)
        kv_p = (k_all, v_all)
        ya = _fox_prompt(z, ccol, crow, TM_BIG)
        mg, cst = _mix(z, ya, w_conv, w_pc_b, w_pa_b, l, TM_SMALL, TM_SMALL, None)
        x1 = _oproj(xp, mg, w_o_b, l, TM_SMALL)
        xp, fst = _ffn(x1, g_ffn3, w_up_b, w_fconv, b_fconv3, w_down_b, l, TM_BIG, TM_BIG, None)
        lf_p.append(lf); cst_p.append(cst); fst_p.append(fst[-1].reshape(1, 2, 2 * D_FF))
        z, k_all, v_all, lf, _, crow = _inproj(xs, g_mix3, w_main, b_main, w_f, b_f, l, rows_s, rows_s,
                                               t_new, False, kv_s)
        kv_s = (k_all, v_all)
        cn_t = jnp.swapaxes(crow[0].reshape(N_HEADS, nb, t_new), 0, 1)
        ya = _fox_sample(z, cn_t, ck, cv, lc_t, l, t_new)
        mg, cst = _mix(z, ya, w_conv, w_pc_b, w_pa_b, l, rows_s, t_new, state_conv[l])
        x1 = _oproj(xs, mg, w_o_b, l, rows_s)
        xs, fst = _ffn(x1, g_ffn3, w_up_b, w_fconv, b_fconv3, w_down_b, l, rows_s, t_new,
                       state_ffn_conv[l])
        lf_s.append(lf); cst_s.append(cst); fst_s.append(fst[-1].reshape(nb, 2, 2 * D_FF))

    g_fin = g_final[None, :]
    y_prompt = _final_norm(xp, g_fin, TM_OUT, N_META)[None]
    y_sample = _final_norm(xs, g_fin, rows_s, 0).reshape(nb, t_new, D_MODEL)
    return (y_prompt, y_sample,
            kv_p[0].reshape(depth, 1, lp, N_HEADS, HEAD_DIM),
            kv_p[1].reshape(depth, 1, lp, N_HEADS, HEAD_DIM),
            jnp.stack(lf_p).reshape(depth, 1, lp, N_HEADS),
            jnp.stack(cst_p),
            jnp.stack(fst_p),
            kv_s[0].reshape(depth, nb, t_new, N_HEADS, HEAD_DIM),
            kv_s[1].reshape(depth, nb, t_new, N_HEADS, HEAD_DIM),
            jnp.stack(lf_s).reshape(depth, nb, t_new, N_HEADS),
            jnp.stack(cst_s),
            jnp.stack(fst_s))
```

```python
import functools

import jax
import jax.numpy as jnp
from jax import lax
from jax.experimental import pallas as pl
from jax.experimental.pallas import tpu as pltpu

F32 = jnp.float32
BF16 = jnp.bfloat16

D_MODEL = 2048
N_META = 16
CONV_DIM = D_MODEL // 2
HEAD_DIM = 128
ATTN_DIM = D_MODEL // 2
N_HEADS = ATTN_DIM // HEAD_DIM
D_FF = 2 * D_MODEL
EPS = 1e-6
ATTN_SCALE = HEAD_DIM ** -0.5
NEG_INF = -1e30
EXP_ZERO_BELOW = 106.0
F_OFF = 3 * CONV_DIM + 3 * ATTN_DIM
MAIN_DIM = F_OFF + 2 * D_MODEL

LANES = 128
VMEM_LIMIT = 60000 * 1024

Q_OFF = 3 * CONV_DIM
K_OFF = Q_OFF + ATTN_DIM
V_OFF = K_OFF + ATTN_DIM
G_OFF = V_OFF + ATTN_DIM

TN_IN = 1024
TF = 512
TM_BIG = 912
TM_SMALL = 432
TM_OUT = 512
ROW_CHUNK = 304
KV_CHUNK = 1024
SUFFIX_BLOCK = 256


def _params(n_axes):
    return pltpu.CompilerParams(dimension_semantics=("arbitrary",) * n_axes,
                                vmem_limit_bytes=VMEM_LIMIT)


def _dot(a, b):
    return jnp.dot(a, b, preferred_element_type=F32)


def _dot_nt(a, b):
    return lax.dot_general(a, b, (((1,), (1,)), ((), ())), preferred_element_type=F32)


def _dot_exact01(mask_bf16, x, nt=False):
    hi = x.astype(BF16)
    r1 = x - hi.astype(F32)
    mid = r1.astype(BF16)
    lo = (r1 - mid.astype(F32)).astype(BF16)
    if nt:
        return _dot(hi, mask_bf16) + _dot(mid, mask_bf16) + _dot(lo, mask_bf16)
    return _dot(mask_bf16, hi) + _dot(mask_bf16, mid) + _dot(mask_bf16, lo)


def _rmsnorm(x, g):
    ms = jnp.mean(x * x, axis=-1, keepdims=True)
    return (x * lax.rsqrt(ms + EPS)) * g


def _sigmoid(x):
    return 1.0 / (1.0 + jnp.exp(-x))


def _conv3(u, h0rows, h1rows, w, seg):
    tm = u.shape[0]
    r = lax.broadcasted_iota(jnp.int32, (tm, 1), 0)
    if seg < tm:
        r = r % seg
    um1 = jnp.where(r == 0, h1rows, pltpu.roll(u, 1, 0))
    um2 = jnp.where(r == 0, h0rows, jnp.where(r == 1, h1rows, pltpu.roll(u, 2, 0)))
    return um2 * w[0:1] + um1 * w[1:2] + u * w[2:3]


def _seg_rows(h, seg):
    nseg, c = h.shape
    return jnp.broadcast_to(h[:, None, :], (nseg, seg, c)).reshape(nseg * seg, c)


def _inproj_body(*refs, seg, use_carry, n_alias):
    x_ref, g_ref, wl_ref, wg_ref, b_ref, wf_ref, bf_ref = refs[:7]
    (z_ref, k_ref, v_ref, lf_ref, ccol_ref, crow_ref, h_scr, carry_scr) = refs[7 + n_alias:]
    i = pl.program_id(0)
    j = pl.program_id(1)
    tm = x_ref.shape[0]
    jq, jk, jv, jg = (Q_OFF // TN_IN, K_OFF // TN_IN, V_OFF // TN_IN, G_OFF // TN_IN)

    @pl.when(j == 0)
    def _():
        hb = _rmsnorm(x_ref[...], g_ref[...]).astype(BF16)
        h_scr[...] = hb
        fl = _dot(hb, wf_ref[...]) + bf_ref[...]
        lf = jnp.minimum(fl, 0.0) - jnp.log1p(jnp.exp(-jnp.abs(fl)))
        lf_ref[...] = lf[:, :N_HEADS]
        r = lax.broadcasted_iota(jnp.int32, (tm, tm), 0)
        c = lax.broadcasted_iota(jnp.int32, (tm, tm), 1)
        tri = c <= r
        if seg < tm:
            tri = tri & ((r // seg) == (c // seg))
        cum = _dot_exact01(jnp.where(tri, 1.0, 0.0).astype(BF16), lf)
        if use_carry:
            @pl.when(i == 0)
            def _():
                carry_scr[...] = jnp.zeros_like(carry_scr)
            cum = cum + carry_scr[...]
            carry_scr[...] = cum[tm - 1:tm, :]
        ccol_ref[...] = cum
        crow_ref[...] = cum.T[:N_HEADS, :]

    @pl.when(j < jg)
    def _():
        acc = _dot(h_scr[...], wl_ref[...]) + b_ref[...]
        is_q = (j >= jq) & (j < jk)
        z_ref[...] = jnp.where(is_q, acc * ATTN_SCALE, acc).astype(BF16)

        @pl.when((j >= jk) & (j < jv))
        def _():
            k_ref[...] = acc

        @pl.when((j >= jv) & (j < jg))
        def _():
            v_ref[...] = acc

    @pl.when(j >= jg)
    def _():
        z_ref[...] = _sigmoid(_dot(h_scr[...], wg_ref[...]) + b_ref[...]).astype(BF16)


def _inproj(x, g, w_left, w_gate, b_main, w_f, b_f, layer, n_valid, tm, seg, use_carry, kv_prev):
    rows = x.shape[0]
    nm = rows // tm
    nj = MAIN_DIM // TN_IN
    jk, jv, jg = K_OFF // TN_IN, V_OFF // TN_IN, G_OFF // TN_IN
    nkv = ATTN_DIM // TN_IN
    depth = w_left.shape[0]
    n_alias = 0 if kv_prev is None else 2
    in_specs = [
        pl.BlockSpec((tm, D_MODEL), lambda i, j: (i, 0)),
        pl.BlockSpec((None, 1, D_MODEL), lambda i, j: (layer, 0, 0)),
        pl.BlockSpec((None, D_MODEL, TN_IN), lambda i, j: (layer, 0, jnp.minimum(j, jg - 1))),
        pl.BlockSpec((None, D_MODEL, TN_IN), lambda i, j: (layer, 0, jnp.maximum(j - jg, 0))),
        pl.BlockSpec((None, 1, TN_IN), lambda i, j: (layer, 0, j)),
        pl.BlockSpec((None, D_MODEL, LANES), lambda i, j: (layer, 0, 0)),
        pl.BlockSpec((None, 1, LANES), lambda i, j: (layer, 0, 0)),
    ]
    args = [x, g, w_left, w_gate, b_main, w_f, b_f]
    aliases = {}
    if kv_prev is not None:
        in_specs += [pl.BlockSpec(memory_space=pl.ANY)] * 2
        args += list(kv_prev)
        aliases = {len(args) - 2: 1, len(args) - 1: 2}
    out_shape = (
        jax.ShapeDtypeStruct((rows, MAIN_DIM), BF16),
        jax.ShapeDtypeStruct((depth, n_valid, ATTN_DIM), F32),
        jax.ShapeDtypeStruct((depth, n_valid, ATTN_DIM), F32),
        jax.ShapeDtypeStruct((n_valid, N_HEADS), F32),
        jax.ShapeDtypeStruct((rows, LANES), F32),
        jax.ShapeDtypeStruct((nm, N_HEADS, tm), F32),
    )
    out_specs = (
        pl.BlockSpec((tm, TN_IN), lambda i, j: (i, j)),
        pl.BlockSpec((None, tm, TN_IN), lambda i, j: (layer, i, jnp.clip(j - jk, 0, nkv - 1)),
                     pipeline_mode=pl.Buffered(1)),
        pl.BlockSpec((None, tm, TN_IN), lambda i, j: (layer, i, jnp.clip(j - jv, 0, nkv - 1)),
                     pipeline_mode=pl.Buffered(1)),
        pl.BlockSpec((tm, N_HEADS), lambda i, j: (i, 0)),
        pl.BlockSpec((tm, LANES), lambda i, j: (i, 0)),
        pl.BlockSpec((None, N_HEADS, tm), lambda i, j: (i, 0, 0)),
    )
    return pl.pallas_call(
        functools.partial(_inproj_body, seg=seg, use_carry=use_carry, n_alias=n_alias),
        grid=(nm, nj),
        in_specs=in_specs,
        out_specs=out_specs,
        out_shape=out_shape,
        scratch_shapes=[pltpu.VMEM((tm, D_MODEL), BF16), pltpu.VMEM((1, LANES), F32)],
        input_output_aliases=aliases,
        compiler_params=_params(2),
        name="inproj",
    )(*args)


def _fox_plan_body(q_ref, k_ref, crow_ref, o_ref):
    tq = q_ref.shape[0]

    def max_norm(ref):
        cols = []
        for h in range(N_HEADS):
            x = ref[:, h * HEAD_DIM:(h + 1) * HEAD_DIM].astype(F32)
            cols.append(jnp.sqrt(jnp.max(jnp.sum(x * x, axis=1, keepdims=True), axis=0, keepdims=True)))
        return jnp.concatenate(cols, axis=0)

    lane = lax.broadcasted_iota(jnp.int32, (N_HEADS, LANES), 1)
    crow = crow_ref[...]
    o_ref[...] = jnp.where(lane == 0, max_norm(q_ref),
                           jnp.where(lane == 1, max_norm(k_ref),
                                     jnp.where(lane == 2, crow[:, 0:1],
                                               jnp.where(lane == 3, crow[:, tq - 1:tq], 0.0))))


def _fox_plan(z, crow, tq):
    rows = z.shape[0]
    nq = rows // tq
    return pl.pallas_call(
        _fox_plan_body,
        grid=(nq,),
        in_specs=[pl.BlockSpec((tq, ATTN_DIM), lambda i: (i, Q_OFF // ATTN_DIM)),
                  pl.BlockSpec((tq, ATTN_DIM), lambda i: (i, K_OFF // ATTN_DIM)),
                  pl.BlockSpec((None, N_HEADS, tq), lambda i: (i, 0, 0))],
        out_specs=pl.BlockSpec((None, N_HEADS, LANES), lambda i: (i, 0, 0)),
        out_shape=jax.ShapeDtypeStruct((nq, N_HEADS, LANES), F32),
        compiler_params=_params(1),
        name="fox_plan",
    )(z, z, crow)


def _tiles_to_visit(plan):
    qn, kn, c_first, c_last = (plan[:, :, i].T for i in range(4))
    nq = qn.shape[1]
    gap = (qn[:, :, None] * (kn[:, None, :] + kn[:, :, None])
           + c_first[:, :, None] - c_last[:, None, :])
    qi = jnp.arange(nq)[:, None]
    kb = jnp.arange(nq)[None, :]
    needed = (kb < qi)[None] & (gap >= -EXP_ZERO_BELOW)
    return 1 + jnp.max(jnp.where(needed, (qi - kb)[None], 0), axis=2).astype(jnp.int32)


def _fox_prompt_body(nvisit_ref, q_ref, k_ref, v_ref, ccol_ref, crow_ref, o_ref, m_scr, l_scr, acc_scr, cq_scr):
    h = pl.program_id(0)
    qi = pl.program_id(1)
    tq = q_ref.shape[0]
    tk = tq
    m_scr[...] = jnp.full_like(m_scr, NEG_INF)
    l_scr[...] = jnp.zeros_like(l_scr)
    acc_scr[...] = jnp.zeros_like(acc_scr)
    lane = lax.broadcasted_iota(jnp.int32, (tq, LANES), 1)
    cq_scr[...] = jnp.sum(jnp.where(lane == h, ccol_ref[...], 0.0), axis=1, keepdims=True)

    def tile(kb, diagonal):
        start = pl.multiple_of(kb * tk, ROW_CHUNK)
        k = k_ref[pl.ds(start, tk), :]
        v = v_ref[pl.ds(start, tk), :]
        ck = crow_ref[kb, pl.ds(h, 1), :]
        chunks = list(range(0, tq, ROW_CHUNK))

        def n_keys(r0):
            return r0 + ROW_CHUNK if diagonal else tk

        def scores(r0):
            nk = n_keys(r0)
            return _dot_nt(q_ref[r0:r0 + ROW_CHUNK, :], k[:nk])

        def softmax(r0, s):
            rows = slice(r0, r0 + ROW_CHUNK)
            nk = n_keys(r0)
            s = s + cq_scr[rows, :] - ck[:, :nk]
            if diagonal:
                r = lax.broadcasted_iota(jnp.int32, (ROW_CHUNK, nk), 0) + r0
                c = lax.broadcasted_iota(jnp.int32, (ROW_CHUNK, nk), 1)
                s = jnp.where(c <= r, s, NEG_INF)
            m_prev = m_scr[rows, :]
            m_new = jnp.maximum(m_prev, jnp.max(s, axis=1, keepdims=True))
            alpha = jnp.exp(m_prev - m_new)
            p = jnp.exp(s - m_new)
            l_scr[rows, :] = alpha * l_scr[rows, :] + jnp.sum(p, axis=1, keepdims=True)
            m_scr[rows, :] = m_new
            return alpha, p.astype(BF16)

        def values(r0, alpha, p):
            rows = slice(r0, r0 + ROW_CHUNK)
            acc_scr[rows, :] = alpha * acc_scr[rows, :] + _dot(p, v[:n_keys(r0)])

        s_next = scores(chunks[0])
        pending = None
        for idx, r0 in enumerate(chunks):
            s_cur = s_next
            if idx + 1 < len(chunks):
                s_next = scores(chunks[idx + 1])
            alpha, p = softmax(r0, s_cur)
            if pending is not None:
                values(*pending)
            pending = (r0, alpha, p)
        values(*pending)

    tile(qi, True)

    def older(kk, carry):
        tile(qi - kk, False)
        return carry

    lax.fori_loop(1, nvisit_ref[h, qi], older, 0)
    o_ref[...] = (acc_scr[...] / l_scr[...]).astype(BF16)


def _fox_prompt(z, ccol, crow, tq):
    rows = z.shape[0]
    nq = rows // tq
    qb, kb, vb = Q_OFF // HEAD_DIM, K_OFF // HEAD_DIM, V_OFF // HEAD_DIM
    nvisit = _tiles_to_visit(_fox_plan(z, crow, tq))
    return pl.pallas_call(
        _fox_prompt_body,
        grid_spec=pltpu.PrefetchScalarGridSpec(
            num_scalar_prefetch=1,
            grid=(N_HEADS, nq),
            in_specs=[
                pl.BlockSpec((tq, HEAD_DIM), lambda h, qi, nv: (qi, qb + h)),
                pl.BlockSpec((rows, HEAD_DIM), lambda h, qi, nv: (0, kb + h)),
                pl.BlockSpec((rows, HEAD_DIM), lambda h, qi, nv: (0, vb + h)),
                pl.BlockSpec((tq, LANES), lambda h, qi, nv: (qi, 0)),
                pl.BlockSpec((nq, N_HEADS, tq), lambda h, qi, nv: (0, 0, 0)),
            ],
            out_specs=pl.BlockSpec((tq, HEAD_DIM), lambda h, qi, nv: (qi, h)),
            scratch_shapes=[pltpu.VMEM((tq, 1), F32), pltpu.VMEM((tq, 1), F32),
                            pltpu.VMEM((tq, HEAD_DIM), F32), pltpu.VMEM((tq, 1), F32)]),
        out_shape=jax.ShapeDtypeStruct((rows, ATTN_DIM), BF16),
        compiler_params=_params(2),
        name="fox_prompt",
    )(nvisit, z, z, z, ccol, crow)


def _fox_sample_body(*refs):
    (q_ref, kn_ref, vn_ref, cn_ref, lc_ref, ck_ref, cv_ref, o_ref,
     qbd_scr, m_scr, l_scr, acc_scr, tail_scr) = refs
    c = pl.program_id(1)
    nc = pl.num_programs(1)
    t_new = q_ref.shape[0]
    nrow = t_new * N_HEADS
    kc = ck_ref.shape[0] // N_HEADS
    nblk = kc // SUFFIX_BLOCK
    cn = cn_ref[...]
    cq = jnp.concatenate([cn[:, t:t + 1] for t in range(t_new)], axis=0)
    col_head = lax.broadcasted_iota(jnp.int32, (N_HEADS, ATTN_DIM), 1) // HEAD_DIM
    row_head = lax.broadcasted_iota(jnp.int32, (N_HEADS, ATTN_DIM), 0)
    head_mask = col_head == row_head

    @pl.when(c == 0)
    def _():
        q = q_ref[...].astype(F32)
        q3 = jnp.where(head_mask[None], jnp.broadcast_to(q[:, None, :], (t_new, N_HEADS, ATTN_DIM)), 0.0)
        qbd_scr[...] = q3.reshape(nrow, ATTN_DIM).astype(BF16)
        m_scr[...] = jnp.full_like(m_scr, NEG_INF)
        l_scr[...] = jnp.zeros_like(l_scr)
        acc_scr[...] = jnp.zeros_like(acc_scr)
        tail_scr[...] = jnp.zeros_like(tail_scr)

    def online(s, vals):
        m_prev = m_scr[...]
        m_new = jnp.maximum(m_prev, jnp.max(s, axis=1, keepdims=True))
        alpha = jnp.exp(m_prev - m_new)
        p = jnp.exp(s - m_new)
        l_scr[...] = alpha * l_scr[...] + jnp.sum(p, axis=1, keepdims=True)
        acc_scr[...] = alpha * acc_scr[...] + _dot(p.astype(BF16), vals)
        m_scr[...] = m_new

    lc = lc_ref[...]
    x = jnp.concatenate([lc[:, b * SUFFIX_BLOCK:(b + 1) * SUFFIX_BLOCK] for b in range(nblk)], axis=0)
    jj = lax.broadcasted_iota(jnp.int32, (SUFFIX_BLOCK, SUFFIX_BLOCK), 0)
    ss = lax.broadcasted_iota(jnp.int32, (SUFFIX_BLOCK, SUFFIX_BLOCK), 1)
    excl = _dot_exact01(jnp.where(jj > ss, 1.0, 0.0).astype(BF16), x, nt=True)
    off = tail_scr[...]
    pieces = [None] * nblk
    for b in range(nblk - 1, -1, -1):
        eb = excl[b * N_HEADS:(b + 1) * N_HEADS, :]
        pieces[b] = eb + off
        off = off + (eb[:, 0:1] + x[b * N_HEADS:(b + 1) * N_HEADS, 0:1])
    tail_scr[...] = off
    suffix = jnp.concatenate(pieces, axis=1)

    def all_heads(ref):
        return jnp.concatenate([ref[pl.ds(h, kc, stride=N_HEADS), :].astype(BF16)
                                for h in range(N_HEADS)], axis=1)

    s = _dot_nt(qbd_scr[...], all_heads(ck_ref))
    s = (s.reshape(t_new, N_HEADS, kc) + cq.reshape(t_new, N_HEADS, 1) + suffix[None]).reshape(nrow, kc)
    online(s, all_heads(cv_ref))

    @pl.when(c == nc - 1)
    def _():
        sn = _dot_nt(qbd_scr[...], kn_ref[...])
        sn = sn.reshape(t_new, N_HEADS, t_new) + cq.reshape(t_new, N_HEADS, 1) + (-cn)[None]
        tq = lax.broadcasted_iota(jnp.int32, (t_new, N_HEADS, t_new), 0)
        tk = lax.broadcasted_iota(jnp.int32, (t_new, N_HEADS, t_new), 2)
        sn = jnp.where(tk <= tq, sn, NEG_INF).reshape(nrow, t_new)
        online(sn, vn_ref[...])
        o = (acc_scr[...] / l_scr[...]).reshape(t_new, N_HEADS, ATTN_DIM)
        o_ref[...] = jnp.sum(jnp.where(head_mask[None], o, 0.0), axis=1).astype(BF16)


def _fox_sample(z, cn_t, cache_k, cache_v, lc_t, layer, t_new):
    nb = cache_k.shape[1]
    past = cache_k.shape[2] // N_HEADS
    nc = past // KV_CHUNK
    nrow = t_new * N_HEADS
    qb, kb, vb = Q_OFF // ATTN_DIM, K_OFF // ATTN_DIM, V_OFF // ATTN_DIM
    cache_spec = pl.BlockSpec((None, None, KV_CHUNK * N_HEADS, HEAD_DIM),
                              lambda b, c: (layer, b, nc - 1 - c, 0))
    return pl.pallas_call(
        _fox_sample_body,
        grid=(nb, nc),
        in_specs=[
            pl.BlockSpec((t_new, ATTN_DIM), lambda b, c: (b, qb)),
            pl.BlockSpec((t_new, ATTN_DIM), lambda b, c: (b, kb)),
            pl.BlockSpec((t_new, ATTN_DIM), lambda b, c: (b, vb)),
            pl.BlockSpec((None, N_HEADS, t_new), lambda b, c: (b, 0, 0)),
            pl.BlockSpec((None, None, N_HEADS, KV_CHUNK), lambda b, c: (layer, b, 0, nc - 1 - c)),
            cache_spec, cache_spec,
        ],
        out_specs=pl.BlockSpec((t_new, ATTN_DIM), lambda b, c: (b, 0)),
        out_shape=jax.ShapeDtypeStruct((nb * t_new, ATTN_DIM), BF16),
        scratch_shapes=[pltpu.VMEM((nrow, ATTN_DIM), BF16), pltpu.VMEM((nrow, 1), F32),
                        pltpu.VMEM((nrow, 1), F32), pltpu.VMEM((nrow, ATTN_DIM), F32),
                        pltpu.VMEM((N_HEADS, 1), F32)],
        compiler_params=_params(2),
        name="fox_sample",
    )(z, z, z, cn_t, lc_t, cache_k, cache_v)


def _mix_body(*refs, seg, has_hist):
    zc_ref, ya_ref, gc_ref, ga_ref, wc_ref, wpc_ref, wpa_ref = refs[:7]
    n_in = 8 if has_hist else 7
    mg_ref, st_ref = refs[n_in:n_in + 2]
    i = pl.program_id(0)
    tm = zc_ref.shape[0]
    xc = zc_ref[:, 0:CONV_DIM].astype(F32)
    bc = zc_ref[:, CONV_DIM:2 * CONV_DIM].astype(F32)
    cc = zc_ref[:, 2 * CONV_DIM:3 * CONV_DIM].astype(F32)
    u = cc * xc
    if has_hist:
        hist = refs[7][...]
        h0 = _seg_rows(hist[:, 0, :], seg)
        h1 = _seg_rows(hist[:, 1, :], seg)
        nseg = tm // seg
        st_ref[...] = u.reshape(nseg, seg, CONV_DIM)[:, seg - 2:seg, :]
    else:
        carry_scr = refs[n_in + 2]

        @pl.when(i == 0)
        def _():
            carry_scr[...] = jnp.zeros_like(carry_scr)
        h0 = carry_scr[0:1, :]
        h1 = carry_scr[1:2, :]
        st_ref[...] = u[tm - 2:tm, :][None]
    yc = (bc * _conv3(u, h0, h1, wc_ref[...], seg)).astype(BF16)
    if not has_hist:
        carry_scr[0:2, :] = u[tm - 2:tm, :]
    t1 = _dot(yc, wpc_ref[...])
    t2 = _dot(ya_ref[...], wpa_ref[...])
    mg_ref[...] = (gc_ref[...].astype(F32) * t1 + ga_ref[...].astype(F32) * t2).astype(BF16)


def _mix(z, ya, w_conv, w_pc, w_pa, layer, tm, seg, hist):
    rows = z.shape[0]
    nm = rows // tm
    has_hist = hist is not None
    nseg = tm // seg if has_hist else 1
    in_specs = [
        pl.BlockSpec((tm, 3 * CONV_DIM), lambda i: (i, 0)),
        pl.BlockSpec((tm, ATTN_DIM), lambda i: (i, 0)),
        pl.BlockSpec((tm, D_MODEL), lambda i: (i, G_OFF // D_MODEL)),
        pl.BlockSpec((tm, D_MODEL), lambda i: (i, G_OFF // D_MODEL + 1)),
        pl.BlockSpec((None, 3, CONV_DIM), lambda i: (layer, 0, 0)),
        pl.BlockSpec((None, CONV_DIM, D_MODEL), lambda i: (layer, 0, 0)),
        pl.BlockSpec((None, ATTN_DIM, D_MODEL), lambda i: (layer, 0, 0)),
    ]
    args = [z, ya, z, z, w_conv, w_pc, w_pa]
    scratch = []
    if has_hist:
        in_specs.append(pl.BlockSpec((nseg, 2, CONV_DIM), lambda i: (0, 0, 0)))
        args.append(hist)
    else:
        scratch.append(pltpu.VMEM((8, CONV_DIM), F32))
    return pl.pallas_call(
        functools.partial(_mix_body, seg=seg, has_hist=has_hist),
        grid=(nm,),
        in_specs=in_specs,
        out_specs=(pl.BlockSpec((tm, D_MODEL), lambda i: (i, 0)),
                   pl.BlockSpec((nseg, 2, CONV_DIM), lambda i: (0, 0, 0))),
        out_shape=(jax.ShapeDtypeStruct((rows, D_MODEL), BF16),
                   jax.ShapeDtypeStruct((nseg, 2, CONV_DIM), F32)),
        scratch_shapes=scratch,
        compiler_params=_params(1),
        name="mix",
    )(*args)


def _oproj_body(x_ref, mg_ref, wo_ref, o_ref):
    o_ref[...] = x_ref[...] + _dot(mg_ref[...], wo_ref[...])


def _oproj(x, mg, w_o, layer, tm):
    rows = x.shape[0]
    return pl.pallas_call(
        _oproj_body,
        grid=(rows // tm,),
        in_specs=[pl.BlockSpec((tm, D_MODEL), lambda i: (i, 0)),
                  pl.BlockSpec((tm, D_MODEL), lambda i: (i, 0)),
                  pl.BlockSpec((None, D_MODEL, D_MODEL), lambda i: (layer, 0, 0))],
        out_specs=pl.BlockSpec((tm, D_MODEL), lambda i: (i, 0)),
        out_shape=jax.ShapeDtypeStruct((rows, D_MODEL), F32),
        compiler_params=_params(1),
        name="oproj",
    )(x, mg, w_o)


def _ffn_body(*refs, seg, has_hist):
    (x_ref, g_ref, wua_ref, wub_ref, wca_ref, wcb_ref, ba_ref, bb_ref, wd_ref) = refs[:9]
    n_in = 11 if has_hist else 9
    o_ref, st_ref, h_scr = refs[n_in:n_in + 3]
    i = pl.program_id(0)
    j = pl.program_id(1)
    tm = x_ref.shape[0]
    tf = wua_ref.shape[1]

    @pl.when(j == 0)
    def _():
        x = x_ref[...]
        h_scr[...] = _rmsnorm(x, g_ref[...]).astype(BF16)
        o_ref[...] = x

    rc = ROW_CHUNK if (not has_hist and tm % ROW_CHUNK == 0) else tm
    chunks = list(range(0, tm, rc))

    def up(r0):
        hb = h_scr[r0:r0 + rc, :]
        return _dot(hb, wua_ref[...]), _dot(hb, wub_ref[...])

    def gate(u, hist, w_ref, bias_ref):
        return _conv3(u, hist[0], hist[1], w_ref[...], min(seg, rc)) + bias_ref[...]

    def down(r0, gated):
        o_ref[r0:r0 + rc, :] += _dot(gated, wd_ref[...])

    if has_hist:
        nseg = tm // seg
        hist_a, hist_b = ((_seg_rows(r[:, 0, :], seg), _seg_rows(r[:, 1, :], seg))
                          for r in (refs[9][...], refs[10][...]))
    else:
        ca_scr, cb_scr = refs[n_in + 3:n_in + 5]

        @pl.when(i == 0)
        def _():
            ca_scr[j] = jnp.zeros((8, tf), F32)
            cb_scr[j] = jnp.zeros((8, tf), F32)
        hist_a = (ca_scr[j, 0:1, :], ca_scr[j, 1:2, :])
        hist_b = (cb_scr[j, 0:1, :], cb_scr[j, 1:2, :])

    u_next = up(chunks[0])
    pending = None
    for idx, r0 in enumerate(chunks):
        ua, ub = u_next
        if idx + 1 < len(chunks):
            u_next = up(chunks[idx + 1])
        a = gate(ua, hist_a, wca_ref, ba_ref)
        b = gate(ub, hist_b, wcb_ref, bb_ref)
        gated = ((a * _sigmoid(a)) * b).astype(BF16)
        hist_a = (ua[rc - 2:rc - 1, :], ua[rc - 1:rc, :])
        hist_b = (ub[rc - 2:rc - 1, :], ub[rc - 1:rc, :])
        if pending is not None:
            down(*pending)
        pending = (r0, gated)
    down(*pending)

    if has_hist:
        st_ref[:, :, 0, :] = ua.reshape(nseg, seg, tf)[:, seg - 2:seg, :]
        st_ref[:, :, 1, :] = ub.reshape(nseg, seg, tf)[:, seg - 2:seg, :]
    else:
        st_ref[0, :, 0, :] = ua[rc - 2:rc, :]
        st_ref[0, :, 1, :] = ub[rc - 2:rc, :]
        ca_scr[j, 0:2, :] = ua[rc - 2:rc, :]
        cb_scr[j, 0:2, :] = ub[rc - 2:rc, :]


def _ffn(x, g, w_up, w_fconv, b_fconv, w_down, layer, tm, seg, hist):
    rows = x.shape[0]
    nm = rows // tm
    nf = D_FF // TF
    has_hist = hist is not None
    nseg = tm // seg if has_hist else 1
    in_specs = [
        pl.BlockSpec((tm, D_MODEL), lambda i, j: (i, 0)),
        pl.BlockSpec((None, 1, D_MODEL), lambda i, j: (layer, 0, 0)),
        pl.BlockSpec((None, D_MODEL, TF), lambda i, j: (layer, 0, j)),
        pl.BlockSpec((None, D_MODEL, TF), lambda i, j: (layer, 0, nf + j)),
        pl.BlockSpec((None, 3, TF), lambda i, j: (layer, 0, j)),
        pl.BlockSpec((None, 3, TF), lambda i, j: (layer, 0, nf + j)),
        pl.BlockSpec((None, 1, TF), lambda i, j: (layer, 0, j)),
        pl.BlockSpec((None, 1, TF), lambda i, j: (layer, 0, nf + j)),
        pl.BlockSpec((None, TF, D_MODEL), lambda i, j: (layer, j, 0)),
    ]
    args = [x, g, w_up, w_up, w_fconv, w_fconv, b_fconv, b_fconv, w_down]
    scratch = [pltpu.VMEM((tm, D_MODEL), BF16)]
    if has_hist:
        in_specs += [pl.BlockSpec((nseg, 2, TF), lambda i, j: (0, 0, j)),
                     pl.BlockSpec((nseg, 2, TF), lambda i, j: (0, 0, nf + j))]
        args += [hist, hist]
    else:
        scratch += [pltpu.VMEM((nf, 8, TF), F32), pltpu.VMEM((nf, 8, TF), F32)]
    return pl.pallas_call(
        functools.partial(_ffn_body, seg=seg, has_hist=has_hist),
        grid=(nm, nf),
        in_specs=in_specs,
        out_specs=(pl.BlockSpec((tm, D_MODEL), lambda i, j: (i, 0)),
                   pl.BlockSpec((None, nseg, 2, 2, TF), lambda i, j: (i, 0, 0, 0, j))),
        out_shape=(jax.ShapeDtypeStruct((rows, D_MODEL), F32),
                   jax.ShapeDtypeStruct((nm, nseg, 2, 2, D_FF), F32)),
        scratch_shapes=scratch,
        compiler_params=_params(2),
        name="ffn",
    )(*args)


def _final_norm_body(x_ref, g_ref, o_ref):
    o_ref[...] = _rmsnorm(x_ref[...], g_ref[...])


def _final_norm(x, g, tm, skip_rows):
    rows = x.shape[0] - skip_rows
    return pl.pallas_call(
        _final_norm_body,
        grid=(rows // tm,),
        in_specs=[pl.BlockSpec((pl.Element(tm), pl.Element(D_MODEL)),
                               lambda i: (pl.multiple_of(i * tm + skip_rows, 8), 0)),
                  pl.BlockSpec((1, D_MODEL), lambda i: (0, 0))],
        out_specs=pl.BlockSpec((tm, D_MODEL), lambda i: (i, 0)),
        out_shape=jax.ShapeDtypeStruct((rows, D_MODEL), F32),
        compiler_params=_params(1),
        name="final_norm",
    )(x, g)


def kernel(x_prompt, x_sample, cache_k, cache_v, cache_logf, state_conv, state_ffn_conv, meta_tokens,
           g_mix, w_in, b_in, w_conv, w_pc, w_pa, w_o, g_ffn, w_up, w_fconv, b_fconv, w_down, g_final):
    depth = w_in.shape[0]
    batch, seq, _ = x_prompt.shape
    nb, t_new, _ = x_sample.shape
    past = cache_k.shape[2]
    lp = N_META + seq
    assert batch == 1 and past % KV_CHUNK == 0 and lp % TM_BIG == 0 and lp % TM_SMALL == 0
    rows_s = nb * t_new

    w_left = w_in[:, :, :F_OFF].astype(BF16)
    w_gate = w_in[:, :, F_OFF + N_HEADS:].astype(BF16)
    b_main = jnp.concatenate([b_in[:, :F_OFF], b_in[:, F_OFF + N_HEADS:]], axis=1)[:, None, :]
    w_f = jnp.pad(w_in[:, :, F_OFF:F_OFF + N_HEADS], ((0, 0), (0, 0), (0, LANES - N_HEADS))).astype(BF16)
    b_f = jnp.pad(b_in[:, F_OFF:F_OFF + N_HEADS], ((0, 0), (0, LANES - N_HEADS)))[:, None, :]
    w_pc_b, w_pa_b, w_o_b = w_pc.astype(BF16), w_pa.astype(BF16), w_o.astype(BF16)
    w_up_b, w_down_b = w_up.astype(BF16), w_down.astype(BF16)
    g_mix3, g_ffn3, b_fconv3 = g_mix[:, None, :], g_ffn[:, None, :], b_fconv[:, None, :]

    xp = jnp.concatenate([meta_tokens.astype(F32), x_prompt[0]], axis=0)
    xs = x_sample.reshape(rows_s, D_MODEL)
    ck = cache_k.reshape(depth, nb, past * N_HEADS, HEAD_DIM)
    cv = cache_v.reshape(depth, nb, past * N_HEADS, HEAD_DIM)
    lc_t = jnp.swapaxes(cache_logf.astype(F32), 2, 3)

    kv_p = kv_s = None
    lf_p, lf_s, cst_p, cst_s, fst_p, fst_s = [], [], [], [], [], []
    for l in range(depth):
        z, k_all, v_all, lf, ccol, crow = _inproj(xp, g_mix3, w_left, w_gate, b_main, w_f, b_f, l, lp,
                                                   TM_BIG, TM_BIG, True, kv_p)
        kv_p = (k_all, v_all)
        ya = _fox_prompt(z, ccol, crow, TM_BIG)
        mg, cst = _mix(z, ya, w_conv, w_pc_b, w_pa_b, l, TM_SMALL, TM_SMALL, None)
        x1 = _oproj(xp, mg, w_o_b, l, TM_SMALL)
        xp, fst = _ffn(x1, g_ffn3, w_up_b, w_fconv, b_fconv3, w_down_b, l, TM_BIG, TM_BIG, None)
        lf_p.append(lf); cst_p.append(cst); fst_p.append(fst[-1].reshape(1, 2, 2 * D_FF))
        z, k_all, v_all, lf, _, crow = _inproj(xs, g_mix3, w_left, w_gate, b_main, w_f, b_f, l, rows_s,
                                               rows_s, t_new, False, kv_s)
        kv_s = (k_all, v_all)
        cn_t = jnp.swapaxes(crow[0].reshape(N_HEADS, nb, t_new), 0, 1)
        ya = _fox_sample(z, cn_t, ck, cv, lc_t, l, t_new)
        mg, cst = _mix(z, ya, w_conv, w_pc_b, w_pa_b, l, rows_s, t_new, state_conv[l])
        x1 = _oproj(xs, mg, w_o_b, l, rows_s)
        xs, fst = _ffn(x1, g_ffn3, w_up_b, w_fconv, b_fconv3, w_down_b, l, rows_s, t_new,
                       state_ffn_conv[l])
        lf_s.append(lf); cst_s.append(cst); fst_s.append(fst[-1].reshape(nb, 2, 2 * D_FF))

    g_fin = g_final[None, :]
    y_prompt = _final_norm(xp, g_fin, TM_OUT, N_META)[None]
    y_sample = _final_norm(xs, g_fin, rows_s, 0).reshape(nb, t_new, D_MODEL)
    return (y_prompt, y_sample,
            kv_p[0].reshape(depth, 1, lp, N_HEADS, HEAD_DIM),
            kv_p[1].reshape(depth, 1, lp, N_HEADS, HEAD_DIM),
            jnp.stack(lf_p).reshape(depth, 1, lp, N_HEADS),
            jnp.stack(cst_p),
            jnp.stack(fst_p),
            kv_s[0].reshape(depth, nb, t_new, N_HEADS, HEAD_DIM),
            kv_s[1].reshape(depth, nb, t_new, N_HEADS, HEAD_DIM),
            jnp.stack(lf_s).reshape(depth, nb, t_new, N_HEADS),
            jnp.stack(cst_s),
            jnp.stack(fst_s))
```

```python
import functools

import jax
import jax.numpy as jnp
from jax import lax
from jax.experimental import pallas as pl
from jax.experimental.pallas import tpu as pltpu

F32 = jnp.float32
BF16 = jnp.bfloat16

D_MODEL = 2048
N_META = 16
CONV_DIM = D_MODEL // 2
HEAD_DIM = 128
ATTN_DIM = D_MODEL // 2
N_HEADS = ATTN_DIM // HEAD_DIM
D_FF = 2 * D_MODEL
EPS = 1e-6
ATTN_SCALE = HEAD_DIM ** -0.5
NEG_INF = -1e30
EXP_ZERO_BELOW = 106.0
F_OFF = 3 * CONV_DIM + 3 * ATTN_DIM
MAIN_DIM = F_OFF + 2 * D_MODEL

LANES = 128
VMEM_LIMIT = 60000 * 1024

Q_OFF = 3 * CONV_DIM
K_OFF = Q_OFF + ATTN_DIM
V_OFF = K_OFF + ATTN_DIM
G_OFF = V_OFF + ATTN_DIM

TN_IN = 1024
TF = 512
TM_BIG = 912
TM_SMALL = 432
TM_OUT = 512
ROW_CHUNK = 304
KV_CHUNK = 1024
SUFFIX_BLOCK = 256


def _params(n_axes):
    return pltpu.CompilerParams(dimension_semantics=("arbitrary",) * n_axes,
                                vmem_limit_bytes=VMEM_LIMIT)


def _dot(a, b):
    return jnp.dot(a, b, preferred_element_type=F32)


def _dot_nt(a, b):
    return lax.dot_general(a, b, (((1,), (1,)), ((), ())), preferred_element_type=F32)


def _dot_exact01(mask_bf16, x, nt=False):
    hi = x.astype(BF16)
    r1 = x - hi.astype(F32)
    mid = r1.astype(BF16)
    lo = (r1 - mid.astype(F32)).astype(BF16)
    if nt:
        return _dot(hi, mask_bf16) + _dot(mid, mask_bf16) + _dot(lo, mask_bf16)
    return _dot(mask_bf16, hi) + _dot(mask_bf16, mid) + _dot(mask_bf16, lo)


def _rmsnorm(x, g):
    ms = jnp.mean(x * x, axis=-1, keepdims=True)
    return (x * lax.rsqrt(ms + EPS)) * g


def _sigmoid(x):
    return 1.0 / (1.0 + jnp.exp(-x))


def _conv3(u, h0rows, h1rows, w, seg):
    tm = u.shape[0]
    r = lax.broadcasted_iota(jnp.int32, (tm, 1), 0)
    if seg < tm:
        r = r % seg
    um1 = jnp.where(r == 0, h1rows, pltpu.roll(u, 1, 0))
    um2 = jnp.where(r == 0, h0rows, jnp.where(r == 1, h1rows, pltpu.roll(u, 2, 0)))
    return um2 * w[0:1] + um1 * w[1:2] + u * w[2:3]


def _seg_rows(h, seg):
    nseg, c = h.shape
    return jnp.broadcast_to(h[:, None, :], (nseg, seg, c)).reshape(nseg * seg, c)


def _rows_after_meta(x_ref, meta_ref, i):
    x = x_ref[...]
    shifted = jnp.concatenate([meta_ref[...], x[:x.shape[0] - N_META]], axis=0)
    return jnp.where(i == 0, shifted, x)


def _meta_row_spec(tm):
    return pl.BlockSpec((pl.Element(tm), pl.Element(D_MODEL)),
                        lambda i, *_: (pl.multiple_of(jnp.maximum(i * tm - N_META, 0), 8), 0))


def _inproj_body(*refs, seg, use_carry, has_meta, n_alias):
    x_ref, g_ref, wl_ref, wg_ref, b_ref, wf_ref, bf_ref = refs[:7]
    (z_ref, k_ref, v_ref, lf_ref, ccol_ref, crow_ref, h_scr, carry_scr) = refs[7 + has_meta + n_alias:]
    i = pl.program_id(0)
    j = pl.program_id(1)
    tm = x_ref.shape[0]
    jq, jk, jv, jg = (Q_OFF // TN_IN, K_OFF // TN_IN, V_OFF // TN_IN, G_OFF // TN_IN)

    @pl.when(j == 0)
    def _():
        x = _rows_after_meta(x_ref, refs[7], i) if has_meta else x_ref[...]
        hb = _rmsnorm(x, g_ref[...]).astype(BF16)
        h_scr[...] = hb
        fl = _dot_nt(hb, wf_ref[...]) + bf_ref[...]
        lf = jnp.minimum(fl, 0.0) - jnp.log1p(jnp.exp(-jnp.abs(fl)))
        lf_ref[...] = lf[:, :N_HEADS]
        r = lax.broadcasted_iota(jnp.int32, (tm, tm), 0)
        c = lax.broadcasted_iota(jnp.int32, (tm, tm), 1)
        tri = c <= r
        if seg < tm:
            tri = tri & ((r // seg) == (c // seg))
        cum = _dot_exact01(jnp.where(tri, 1.0, 0.0).astype(BF16), lf)
        if use_carry:
            @pl.when(i == 0)
            def _():
                carry_scr[...] = jnp.zeros_like(carry_scr)
            cum = cum + carry_scr[...]
            carry_scr[...] = cum[tm - 1:tm, :]
        ccol_ref[...] = cum
        crow_ref[...] = cum.T[:N_HEADS, :]

    @pl.when(j < jg)
    def _():
        acc = _dot_nt(h_scr[...], wl_ref[...]) + b_ref[...]
        is_q = (j >= jq) & (j < jk)
        z_ref[...] = jnp.where(is_q, acc * ATTN_SCALE, acc).astype(BF16)

        @pl.when((j >= jk) & (j < jv))
        def _():
            k_ref[...] = acc

        @pl.when((j >= jv) & (j < jg))
        def _():
            v_ref[...] = acc

    @pl.when(j >= jg)
    def _():
        z_ref[...] = _sigmoid(_dot_nt(h_scr[...], wg_ref[...]) + b_ref[...]).astype(BF16)


def _inproj(x, meta, g, w_left, w_gate, b_main, w_f, b_f, layer, n_valid, tm, seg, use_carry, kv_prev):
    has_meta = meta is not None
    rows = x.shape[0] + (N_META if has_meta else 0)
    nm = rows // tm
    nj = MAIN_DIM // TN_IN
    jk, jv, jg = K_OFF // TN_IN, V_OFF // TN_IN, G_OFF // TN_IN
    nkv = ATTN_DIM // TN_IN
    depth = w_left.shape[0]
    n_alias = 0 if kv_prev is None else 2
    in_specs = [
        _meta_row_spec(tm) if has_meta else pl.BlockSpec((tm, D_MODEL), lambda i, j: (i, 0)),
        pl.BlockSpec((None, 1, D_MODEL), lambda i, j: (layer, 0, 0)),
        pl.BlockSpec((None, TN_IN, D_MODEL), lambda i, j: (layer, jnp.minimum(j, jg - 1), 0)),
        pl.BlockSpec((None, TN_IN, D_MODEL), lambda i, j: (layer, jnp.maximum(j - jg, 0), 0)),
        pl.BlockSpec((None, 1, TN_IN), lambda i, j: (layer, 0, j)),
        pl.BlockSpec((None, LANES, D_MODEL), lambda i, j: (layer, 0, 0)),
        pl.BlockSpec((None, 1, LANES), lambda i, j: (layer, 0, 0)),
    ]
    args = [x, g, w_left, w_gate, b_main, w_f, b_f]
    if has_meta:
        in_specs.append(pl.BlockSpec((N_META, D_MODEL), lambda i, j: (0, 0)))
        args.append(meta)
    aliases = {}
    if kv_prev is not None:
        in_specs += [pl.BlockSpec(memory_space=pl.ANY)] * 2
        args += list(kv_prev)
        aliases = {len(args) - 2: 1, len(args) - 1: 2}
    out_shape = (
        jax.ShapeDtypeStruct((rows, MAIN_DIM), BF16),
        jax.ShapeDtypeStruct((depth, n_valid, ATTN_DIM), F32),
        jax.ShapeDtypeStruct((depth, n_valid, ATTN_DIM), F32),
        jax.ShapeDtypeStruct((n_valid, N_HEADS), F32),
        jax.ShapeDtypeStruct((rows, LANES), F32),
        jax.ShapeDtypeStruct((nm, N_HEADS, tm), F32),
    )
    out_specs = (
        pl.BlockSpec((tm, TN_IN), lambda i, j: (i, j)),
        pl.BlockSpec((None, tm, TN_IN), lambda i, j: (layer, i, jnp.clip(j - jk, 0, nkv - 1)),
                     pipeline_mode=pl.Buffered(1)),
        pl.BlockSpec((None, tm, TN_IN), lambda i, j: (layer, i, jnp.clip(j - jv, 0, nkv - 1)),
                     pipeline_mode=pl.Buffered(1)),
        pl.BlockSpec((tm, N_HEADS), lambda i, j: (i, 0)),
        pl.BlockSpec((tm, LANES), lambda i, j: (i, 0)),
        pl.BlockSpec((None, N_HEADS, tm), lambda i, j: (i, 0, 0)),
    )
    return pl.pallas_call(
        functools.partial(_inproj_body, seg=seg, use_carry=use_carry, has_meta=has_meta, n_alias=n_alias),
        grid=(nm, nj),
        in_specs=in_specs,
        out_specs=out_specs,
        out_shape=out_shape,
        scratch_shapes=[pltpu.VMEM((tm, D_MODEL), BF16), pltpu.VMEM((1, LANES), F32)],
        input_output_aliases=aliases,
        compiler_params=_params(2),
        name="inproj",
    )(*args)


def _fox_plan_body(q_ref, k_ref, crow_ref, o_ref):
    tq = q_ref.shape[0]

    def max_norm(ref):
        cols = []
        for h in range(N_HEADS):
            x = ref[:, h * HEAD_DIM:(h + 1) * HEAD_DIM].astype(F32)
            cols.append(jnp.sqrt(jnp.max(jnp.sum(x * x, axis=1, keepdims=True), axis=0, keepdims=True)))
        return jnp.concatenate(cols, axis=0)

    lane = lax.broadcasted_iota(jnp.int32, (N_HEADS, LANES), 1)
    crow = crow_ref[...]
    o_ref[...] = jnp.where(lane == 0, max_norm(q_ref),
                           jnp.where(lane == 1, max_norm(k_ref),
                                     jnp.where(lane == 2, crow[:, 0:1],
                                               jnp.where(lane == 3, crow[:, tq - 1:tq], 0.0))))


def _fox_plan(z, crow, tq):
    rows = z.shape[0]
    nq = rows // tq
    return pl.pallas_call(
        _fox_plan_body,
        grid=(nq,),
        in_specs=[pl.BlockSpec((tq, ATTN_DIM), lambda i: (i, Q_OFF // ATTN_DIM)),
                  pl.BlockSpec((tq, ATTN_DIM), lambda i: (i, K_OFF // ATTN_DIM)),
                  pl.BlockSpec((None, N_HEADS, tq), lambda i: (i, 0, 0))],
        out_specs=pl.BlockSpec((None, N_HEADS, LANES), lambda i: (i, 0, 0)),
        out_shape=jax.ShapeDtypeStruct((nq, N_HEADS, LANES), F32),
        compiler_params=_params(1),
        name="fox_plan",
    )(z, z, crow)


def _tiles_to_visit(plan):
    qn, kn, c_first, c_last = (plan[:, :, i].T for i in range(4))
    nq = qn.shape[1]
    gap = (qn[:, :, None] * (kn[:, None, :] + kn[:, :, None])
           + c_first[:, :, None] - c_last[:, None, :])
    qi = jnp.arange(nq)[:, None]
    kb = jnp.arange(nq)[None, :]
    needed = (kb < qi)[None] & (gap >= -EXP_ZERO_BELOW)
    return 1 + jnp.max(jnp.where(needed, (qi - kb)[None], 0), axis=2).astype(jnp.int32)


def _fox_prompt_body(nvisit_ref, q_ref, k_ref, v_ref, ccol_ref, crow_ref, o_ref, m_scr, l_scr, acc_scr, cq_scr):
    h = pl.program_id(0)
    qi = pl.program_id(1)
    tq = q_ref.shape[0]
    tk = tq
    m_scr[...] = jnp.full_like(m_scr, NEG_INF)
    l_scr[...] = jnp.zeros_like(l_scr)
    acc_scr[...] = jnp.zeros_like(acc_scr)
    lane = lax.broadcasted_iota(jnp.int32, (tq, LANES), 1)
    cq_scr[...] = jnp.sum(jnp.where(lane == h, ccol_ref[...], 0.0), axis=1, keepdims=True)

    def tile(kb, diagonal):
        start = pl.multiple_of(kb * tk, ROW_CHUNK)
        k = k_ref[pl.ds(start, tk), :]
        v = v_ref[pl.ds(start, tk), :]
        ck = crow_ref[kb, pl.ds(h, 1), :]
        chunks = list(range(0, tq, ROW_CHUNK))

        def n_keys(r0):
            return r0 + ROW_CHUNK if diagonal else tk

        def scores(r0):
            nk = n_keys(r0)
            return _dot_nt(q_ref[r0:r0 + ROW_CHUNK, :], k[:nk])

        def softmax(r0, s):
            rows = slice(r0, r0 + ROW_CHUNK)
            nk = n_keys(r0)
            s = s + cq_scr[rows, :] - ck[:, :nk]
            if diagonal:
                r = lax.broadcasted_iota(jnp.int32, (ROW_CHUNK, nk), 0) + r0
                c = lax.broadcasted_iota(jnp.int32, (ROW_CHUNK, nk), 1)
                s = jnp.where(c <= r, s, NEG_INF)
            m_prev = m_scr[rows, :]
            m_new = jnp.maximum(m_prev, jnp.max(s, axis=1, keepdims=True))
            alpha = jnp.exp(m_prev - m_new)
            p = jnp.exp(s - m_new)
            l_scr[rows, :] = alpha * l_scr[rows, :] + jnp.sum(p, axis=1, keepdims=True)
            m_scr[rows, :] = m_new
            return alpha, p.astype(BF16)

        def values(r0, alpha, p):
            rows = slice(r0, r0 + ROW_CHUNK)
            acc_scr[rows, :] = alpha * acc_scr[rows, :] + _dot(p, v[:n_keys(r0)])

        s_next = scores(chunks[0])
        pending = None
        for idx, r0 in enumerate(chunks):
            s_cur = s_next
            if idx + 1 < len(chunks):
                s_next = scores(chunks[idx + 1])
            alpha, p = softmax(r0, s_cur)
            if pending is not None:
                values(*pending)
            pending = (r0, alpha, p)
        values(*pending)

    tile(qi, True)

    def older(kk, carry):
        tile(qi - kk, False)
        return carry

    lax.fori_loop(1, nvisit_ref[h, qi], older, 0)
    o_ref[...] = (acc_scr[...] / l_scr[...]).astype(BF16)


def _fox_prompt(z, ccol, crow, tq):
    rows = z.shape[0]
    nq = rows // tq
    qb, kb, vb = Q_OFF // HEAD_DIM, K_OFF // HEAD_DIM, V_OFF // HEAD_DIM
    nvisit = _tiles_to_visit(_fox_plan(z, crow, tq))
    return pl.pallas_call(
        _fox_prompt_body,
        grid_spec=pltpu.PrefetchScalarGridSpec(
            num_scalar_prefetch=1,
            grid=(N_HEADS, nq),
            in_specs=[
                pl.BlockSpec((tq, HEAD_DIM), lambda h, qi, nv: (qi, qb + h)),
                pl.BlockSpec((rows, HEAD_DIM), lambda h, qi, nv: (0, kb + h)),
                pl.BlockSpec((rows, HEAD_DIM), lambda h, qi, nv: (0, vb + h)),
                pl.BlockSpec((tq, LANES), lambda h, qi, nv: (qi, 0)),
                pl.BlockSpec((nq, N_HEADS, tq), lambda h, qi, nv: (0, 0, 0)),
            ],
            out_specs=pl.BlockSpec((tq, HEAD_DIM), lambda h, qi, nv: (qi, h)),
            scratch_shapes=[pltpu.VMEM((tq, 1), F32), pltpu.VMEM((tq, 1), F32),
                            pltpu.VMEM((tq, HEAD_DIM), F32), pltpu.VMEM((tq, 1), F32)]),
        out_shape=jax.ShapeDtypeStruct((rows, ATTN_DIM), BF16),
        compiler_params=_params(2),
        name="fox_prompt",
    )(nvisit, z, z, z, ccol, crow)


def _fox_sample_body(*refs):
    (q_ref, kn_ref, vn_ref, cn_ref, lc_ref, ck_ref, cv_ref, o_ref,
     qbd_scr, m_scr, l_scr, acc_scr, tail_scr) = refs
    c = pl.program_id(1)
    nc = pl.num_programs(1)
    t_new = q_ref.shape[0]
    nrow = t_new * N_HEADS
    kc = ck_ref.shape[0] // N_HEADS
    nblk = kc // SUFFIX_BLOCK
    cn = cn_ref[...]
    cq = jnp.concatenate([cn[:, t:t + 1] for t in range(t_new)], axis=0)
    col_head = lax.broadcasted_iota(jnp.int32, (N_HEADS, ATTN_DIM), 1) // HEAD_DIM
    row_head = lax.broadcasted_iota(jnp.int32, (N_HEADS, ATTN_DIM), 0)
    head_mask = col_head == row_head

    @pl.when(c == 0)
    def _():
        q = q_ref[...].astype(F32)
        q3 = jnp.where(head_mask[None], jnp.broadcast_to(q[:, None, :], (t_new, N_HEADS, ATTN_DIM)), 0.0)
        qbd_scr[...] = q3.reshape(nrow, ATTN_DIM).astype(BF16)
        m_scr[...] = jnp.full_like(m_scr, NEG_INF)
        l_scr[...] = jnp.zeros_like(l_scr)
        acc_scr[...] = jnp.zeros_like(acc_scr)
        tail_scr[...] = jnp.zeros_like(tail_scr)

    def online(s, vals):
        m_prev = m_scr[...]
        m_new = jnp.maximum(m_prev, jnp.max(s, axis=1, keepdims=True))
        alpha = jnp.exp(m_prev - m_new)
        p = jnp.exp(s - m_new)
        l_scr[...] = alpha * l_scr[...] + jnp.sum(p, axis=1, keepdims=True)
        acc_scr[...] = alpha * acc_scr[...] + _dot(p.astype(BF16), vals)
        m_scr[...] = m_new

    lc = lc_ref[...]
    x = jnp.concatenate([lc[:, b * SUFFIX_BLOCK:(b + 1) * SUFFIX_BLOCK] for b in range(nblk)], axis=0)
    jj = lax.broadcasted_iota(jnp.int32, (SUFFIX_BLOCK, SUFFIX_BLOCK), 0)
    ss = lax.broadcasted_iota(jnp.int32, (SUFFIX_BLOCK, SUFFIX_BLOCK), 1)
    excl = _dot_exact01(jnp.where(jj > ss, 1.0, 0.0).astype(BF16), x, nt=True)
    off = tail_scr[...]
    pieces = [None] * nblk
    for b in range(nblk - 1, -1, -1):
        eb = excl[b * N_HEADS:(b + 1) * N_HEADS, :]
        pieces[b] = eb + off
        off = off + (eb[:, 0:1] + x[b * N_HEADS:(b + 1) * N_HEADS, 0:1])
    tail_scr[...] = off
    suffix = jnp.concatenate(pieces, axis=1)

    def all_heads(ref):
        return jnp.concatenate([ref[pl.ds(h, kc, stride=N_HEADS), :].astype(BF16)
                                for h in range(N_HEADS)], axis=1)

    s = _dot_nt(qbd_scr[...], all_heads(ck_ref))
    s = (s.reshape(t_new, N_HEADS, kc) + cq.reshape(t_new, N_HEADS, 1) + suffix[None]).reshape(nrow, kc)
    online(s, all_heads(cv_ref))

    @pl.when(c == nc - 1)
    def _():
        sn = _dot_nt(qbd_scr[...], kn_ref[...])
        sn = sn.reshape(t_new, N_HEADS, t_new) + cq.reshape(t_new, N_HEADS, 1) + (-cn)[None]
        tq = lax.broadcasted_iota(jnp.int32, (t_new, N_HEADS, t_new), 0)
        tk = lax.broadcasted_iota(jnp.int32, (t_new, N_HEADS, t_new), 2)
        sn = jnp.where(tk <= tq, sn, NEG_INF).reshape(nrow, t_new)
        online(sn, vn_ref[...])
        o = (acc_scr[...] / l_scr[...]).reshape(t_new, N_HEADS, ATTN_DIM)
        o_ref[...] = jnp.sum(jnp.where(head_mask[None], o, 0.0), axis=1).astype(BF16)


def _fox_sample(z, cn_t, cache_k, cache_v, lc_t, layer, t_new):
    nb = cache_k.shape[1]
    past = cache_k.shape[2] // N_HEADS
    nc = past // KV_CHUNK
    nrow = t_new * N_HEADS
    qb, kb, vb = Q_OFF // ATTN_DIM, K_OFF // ATTN_DIM, V_OFF // ATTN_DIM
    cache_spec = pl.BlockSpec((None, None, KV_CHUNK * N_HEADS, HEAD_DIM),
                              lambda b, c: (layer, b, nc - 1 - c, 0))
    return pl.pallas_call(
        _fox_sample_body,
        grid=(nb, nc),
        in_specs=[
            pl.BlockSpec((t_new, ATTN_DIM), lambda b, c: (b, qb)),
            pl.BlockSpec((t_new, ATTN_DIM), lambda b, c: (b, kb)),
            pl.BlockSpec((t_new, ATTN_DIM), lambda b, c: (b, vb)),
            pl.BlockSpec((None, N_HEADS, t_new), lambda b, c: (b, 0, 0)),
            pl.BlockSpec((None, None, N_HEADS, KV_CHUNK), lambda b, c: (layer, b, 0, nc - 1 - c)),
            cache_spec, cache_spec,
        ],
        out_specs=pl.BlockSpec((t_new, ATTN_DIM), lambda b, c: (b, 0)),
        out_shape=jax.ShapeDtypeStruct((nb * t_new, ATTN_DIM), BF16),
        scratch_shapes=[pltpu.VMEM((nrow, ATTN_DIM), BF16), pltpu.VMEM((nrow, 1), F32),
                        pltpu.VMEM((nrow, 1), F32), pltpu.VMEM((nrow, ATTN_DIM), F32),
                        pltpu.VMEM((N_HEADS, 1), F32)],
        compiler_params=_params(2),
        name="fox_sample",
    )(z, z, z, cn_t, lc_t, cache_k, cache_v)


def _mix_body(*refs, seg, has_hist):
    zc_ref, ya_ref, gc_ref, ga_ref, wc_ref, wpc_ref, wpa_ref = refs[:7]
    n_in = 8 if has_hist else 7
    mg_ref, st_ref = refs[n_in:n_in + 2]
    i = pl.program_id(0)
    tm = zc_ref.shape[0]
    xc = zc_ref[:, 0:CONV_DIM].astype(F32)
    bc = zc_ref[:, CONV_DIM:2 * CONV_DIM].astype(F32)
    cc = zc_ref[:, 2 * CONV_DIM:3 * CONV_DIM].astype(F32)
    u = cc * xc
    if has_hist:
        hist = refs[7][...]
        h0 = _seg_rows(hist[:, 0, :], seg)
        h1 = _seg_rows(hist[:, 1, :], seg)
        nseg = tm // seg
        st_ref[...] = u.reshape(nseg, seg, CONV_DIM)[:, seg - 2:seg, :]
    else:
        carry_scr = refs[n_in + 2]

        @pl.when(i == 0)
        def _():
            carry_scr[...] = jnp.zeros_like(carry_scr)
        h0 = carry_scr[0:1, :]
        h1 = carry_scr[1:2, :]
        st_ref[...] = u[tm - 2:tm, :][None]
    yc = (bc * _conv3(u, h0, h1, wc_ref[...], seg)).astype(BF16)
    if not has_hist:
        carry_scr[0:2, :] = u[tm - 2:tm, :]
    t1 = _dot(yc, wpc_ref[...])
    t2 = _dot(ya_ref[...], wpa_ref[...])
    mg_ref[...] = (gc_ref[...].astype(F32) * t1 + ga_ref[...].astype(F32) * t2).astype(BF16)


def _mix(z, ya, w_conv, w_pc, w_pa, layer, tm, seg, hist):
    rows = z.shape[0]
    nm = rows // tm
    has_hist = hist is not None
    nseg = tm // seg if has_hist else 1
    in_specs = [
        pl.BlockSpec((tm, 3 * CONV_DIM), lambda i: (i, 0)),
        pl.BlockSpec((tm, ATTN_DIM), lambda i: (i, 0)),
        pl.BlockSpec((tm, D_MODEL), lambda i: (i, G_OFF // D_MODEL)),
        pl.BlockSpec((tm, D_MODEL), lambda i: (i, G_OFF // D_MODEL + 1)),
        pl.BlockSpec((None, 3, CONV_DIM), lambda i: (layer, 0, 0)),
        pl.BlockSpec((None, CONV_DIM, D_MODEL), lambda i: (layer, 0, 0)),
        pl.BlockSpec((None, ATTN_DIM, D_MODEL), lambda i: (layer, 0, 0)),
    ]
    args = [z, ya, z, z, w_conv, w_pc, w_pa]
    scratch = []
    if has_hist:
        in_specs.append(pl.BlockSpec((nseg, 2, CONV_DIM), lambda i: (0, 0, 0)))
        args.append(hist)
    else:
        scratch.append(pltpu.VMEM((8, CONV_DIM), F32))
    return pl.pallas_call(
        functools.partial(_mix_body, seg=seg, has_hist=has_hist),
        grid=(nm,),
        in_specs=in_specs,
        out_specs=(pl.BlockSpec((tm, D_MODEL), lambda i: (i, 0)),
                   pl.BlockSpec((nseg, 2, CONV_DIM), lambda i: (0, 0, 0))),
        out_shape=(jax.ShapeDtypeStruct((rows, D_MODEL), BF16),
                   jax.ShapeDtypeStruct((nseg, 2, CONV_DIM), F32)),
        scratch_shapes=scratch,
        compiler_params=_params(1),
        name="mix",
    )(*args)


def _oproj_body(x_ref, mg_ref, wo_ref, *rest):
    o_ref = rest[-1]
    x = _rows_after_meta(x_ref, rest[0], pl.program_id(0)) if len(rest) == 2 else x_ref[...]
    o_ref[...] = x + _dot(mg_ref[...], wo_ref[...])


def _oproj(x, meta, mg, w_o, layer, tm):
    has_meta = meta is not None
    rows = mg.shape[0]
    in_specs = [_meta_row_spec(tm) if has_meta else pl.BlockSpec((tm, D_MODEL), lambda i: (i, 0)),
                pl.BlockSpec((tm, D_MODEL), lambda i: (i, 0)),
                pl.BlockSpec((None, D_MODEL, D_MODEL), lambda i: (layer, 0, 0))]
    args = [x, mg, w_o]
    if has_meta:
        in_specs.append(pl.BlockSpec((N_META, D_MODEL), lambda i: (0, 0)))
        args.append(meta)
    return pl.pallas_call(
        _oproj_body,
        grid=(rows // tm,),
        in_specs=in_specs,
        out_specs=pl.BlockSpec((tm, D_MODEL), lambda i: (i, 0)),
        out_shape=jax.ShapeDtypeStruct((rows, D_MODEL), F32),
        compiler_params=_params(1),
        name="oproj",
    )(*args)


def _ffn_body(*refs, seg, has_hist):
    (x_ref, g_ref, wua_ref, wub_ref, wca_ref, wcb_ref, ba_ref, bb_ref, wd_ref) = refs[:9]
    n_in = 11 if has_hist else 9
    o_ref, st_ref, h_scr = refs[n_in:n_in + 3]
    i = pl.program_id(0)
    j = pl.program_id(1)
    tm = x_ref.shape[0]
    tf = wua_ref.shape[1]

    @pl.when(j == 0)
    def _():
        x = x_ref[...]
        h_scr[...] = _rmsnorm(x, g_ref[...]).astype(BF16)
        o_ref[...] = x

    rc = ROW_CHUNK if (not has_hist and tm % ROW_CHUNK == 0) else tm
    chunks = list(range(0, tm, rc))

    def up(r0):
        hb = h_scr[r0:r0 + rc, :]
        return _dot(hb, wua_ref[...]), _dot(hb, wub_ref[...])

    def gate(u, hist, w_ref, bias_ref):
        return _conv3(u, hist[0], hist[1], w_ref[...], min(seg, rc)) + bias_ref[...]

    def down(r0, gated):
        o_ref[r0:r0 + rc, :] += _dot(gated, wd_ref[...])

    if has_hist:
        nseg = tm // seg
        hist_a, hist_b = ((_seg_rows(r[:, 0, :], seg), _seg_rows(r[:, 1, :], seg))
                          for r in (refs[9][...], refs[10][...]))
    else:
        ca_scr, cb_scr = refs[n_in + 3:n_in + 5]

        @pl.when(i == 0)
        def _():
            ca_scr[j] = jnp.zeros((8, tf), F32)
            cb_scr[j] = jnp.zeros((8, tf), F32)
        hist_a = (ca_scr[j, 0:1, :], ca_scr[j, 1:2, :])
        hist_b = (cb_scr[j, 0:1, :], cb_scr[j, 1:2, :])

    u_next = up(chunks[0])
    pending = None
    for idx, r0 in enumerate(chunks):
        ua, ub = u_next
        if idx + 1 < len(chunks):
            u_next = up(chunks[idx + 1])
        a = gate(ua, hist_a, wca_ref, ba_ref)
        b = gate(ub, hist_b, wcb_ref, bb_ref)
        gated = ((a * _sigmoid(a)) * b).astype(BF16)
        hist_a = (ua[rc - 2:rc - 1, :], ua[rc - 1:rc, :])
        hist_b = (ub[rc - 2:rc - 1, :], ub[rc - 1:rc, :])
        if pending is not None:
            down(*pending)
        pending = (r0, gated)
    down(*pending)

    if has_hist:
        st_ref[:, :, 0, :] = ua.reshape(nseg, seg, tf)[:, seg - 2:seg, :]
        st_ref[:, :, 1, :] = ub.reshape(nseg, seg, tf)[:, seg - 2:seg, :]
    else:
        st_ref[0, :, 0, :] = ua[rc - 2:rc, :]
        st_ref[0, :, 1, :] = ub[rc - 2:rc, :]
        ca_scr[j, 0:2, :] = ua[rc - 2:rc, :]
        cb_scr[j, 0:2, :] = ub[rc - 2:rc, :]


def _ffn(x, g, w_up, w_fconv, b_fconv, w_down, layer, tm, seg, hist):
    rows = x.shape[0]
    nm = rows // tm
    nf = D_FF // TF
    has_hist = hist is not None
    nseg = tm // seg if has_hist else 1
    in_specs = [
        pl.BlockSpec((tm, D_MODEL), lambda i, j: (i, 0)),
        pl.BlockSpec((None, 1, D_MODEL), lambda i, j: (layer, 0, 0)),
        pl.BlockSpec((None, D_MODEL, TF), lambda i, j: (layer, 0, j)),
        pl.BlockSpec((None, D_MODEL, TF), lambda i, j: (layer, 0, nf + j)),
        pl.BlockSpec((None, 3, TF), lambda i, j: (layer, 0, j)),
        pl.BlockSpec((None, 3, TF), lambda i, j: (layer, 0, nf + j)),
        pl.BlockSpec((None, 1, TF), lambda i, j: (layer, 0, j)),
        pl.BlockSpec((None, 1, TF), lambda i, j: (layer, 0, nf + j)),
        pl.BlockSpec((None, TF, D_MODEL), lambda i, j: (layer, j, 0)),
    ]
    args = [x, g, w_up, w_up, w_fconv, w_fconv, b_fconv, b_fconv, w_down]
    scratch = [pltpu.VMEM((tm, D_MODEL), BF16)]
    if has_hist:
        in_specs += [pl.BlockSpec((nseg, 2, TF), lambda i, j: (0, 0, j)),
                     pl.BlockSpec((nseg, 2, TF), lambda i, j: (0, 0, nf + j))]
        args += [hist, hist]
    else:
        scratch += [pltpu.VMEM((nf, 8, TF), F32), pltpu.VMEM((nf, 8, TF), F32)]
    return pl.pallas_call(
        functools.partial(_ffn_body, seg=seg, has_hist=has_hist),
        grid=(nm, nf),
        in_specs=in_specs,
        out_specs=(pl.BlockSpec((tm, D_MODEL), lambda i, j: (i, 0)),
                   pl.BlockSpec((None, nseg, 2, 2, TF), lambda i, j: (i, 0, 0, 0, j))),
        out_shape=(jax.ShapeDtypeStruct((rows, D_MODEL), F32),
                   jax.ShapeDtypeStruct((nm, nseg, 2, 2, D_FF), F32)),
        scratch_shapes=scratch,
        compiler_params=_params(2),
        name="ffn",
    )(*args)


def _final_norm_body(x_ref, g_ref, o_ref):
    o_ref[...] = _rmsnorm(x_ref[...], g_ref[...])


def _final_norm(x, g, tm, skip_rows):
    rows = x.shape[0] - skip_rows
    return pl.pallas_call(
        _final_norm_body,
        grid=(rows // tm,),
        in_specs=[pl.BlockSpec((pl.Element(tm), pl.Element(D_MODEL)),
                               lambda i: (pl.multiple_of(i * tm + skip_rows, 8), 0)),
                  pl.BlockSpec((1, D_MODEL), lambda i: (0, 0))],
        out_specs=pl.BlockSpec((tm, D_MODEL), lambda i: (i, 0)),
        out_shape=jax.ShapeDtypeStruct((rows, D_MODEL), F32),
        compiler_params=_params(1),
        name="final_norm",
    )(x, g)


def kernel(x_prompt, x_sample, cache_k, cache_v, cache_logf, state_conv, state_ffn_conv, meta_tokens,
           g_mix, w_in, b_in, w_conv, w_pc, w_pa, w_o, g_ffn, w_up, w_fconv, b_fconv, w_down, g_final):
    depth = w_in.shape[0]
    batch, seq, _ = x_prompt.shape
    nb, t_new, _ = x_sample.shape
    past = cache_k.shape[2]
    lp = N_META + seq
    assert batch == 1 and past % KV_CHUNK == 0 and lp % TM_BIG == 0 and lp % TM_SMALL == 0
    rows_s = nb * t_new

    w_in_t = jnp.swapaxes(w_in, 1, 2)
    w_left = w_in_t[:, :F_OFF].astype(BF16)
    w_gate = w_in_t[:, F_OFF + N_HEADS:].astype(BF16)
    w_f = jnp.pad(w_in_t[:, F_OFF:F_OFF + N_HEADS], ((0, 0), (0, LANES - N_HEADS), (0, 0))).astype(BF16)
    b_main = jnp.concatenate([b_in[:, :F_OFF], b_in[:, F_OFF + N_HEADS:]], axis=1)[:, None, :]
    b_f = jnp.pad(b_in[:, F_OFF:F_OFF + N_HEADS], ((0, 0), (0, LANES - N_HEADS)))[:, None, :]
    w_pc_b, w_pa_b, w_o_b = w_pc.astype(BF16), w_pa.astype(BF16), w_o.astype(BF16)
    w_up_b, w_down_b = w_up.astype(BF16), w_down.astype(BF16)
    g_mix3, g_ffn3, b_fconv3 = g_mix[:, None, :], g_ffn[:, None, :], b_fconv[:, None, :]

    xp, meta = x_prompt[0], meta_tokens.astype(F32)
    xs = x_sample.reshape(rows_s, D_MODEL)
    ck = cache_k.reshape(depth, nb, past * N_HEADS, HEAD_DIM)
    cv = cache_v.reshape(depth, nb, past * N_HEADS, HEAD_DIM)
    lc_t = jnp.swapaxes(cache_logf.astype(F32), 2, 3)

    kv_p = kv_s = None
    lf_p, lf_s, cst_p, cst_s, fst_p, fst_s = [], [], [], [], [], []
    for l in range(depth):
        z, k_all, v_all, lf, ccol, crow = _inproj(xp, meta, g_mix3, w_left, w_gate, b_main, w_f, b_f, l,
                                                   lp, TM_BIG, TM_BIG, True, kv_p)
        kv_p = (k_all, v_all)
        ya = _fox_prompt(z, ccol, crow, TM_BIG)
        mg, cst = _mix(z, ya, w_conv, w_pc_b, w_pa_b, l, TM_SMALL, TM_SMALL, None)
        x1 = _oproj(xp, meta, mg, w_o_b, l, TM_SMALL)
        xp, fst = _ffn(x1, g_ffn3, w_up_b, w_fconv, b_fconv3, w_down_b, l, TM_BIG, TM_BIG, None)
        meta = None
        lf_p.append(lf); cst_p.append(cst); fst_p.append(fst[-1].reshape(1, 2, 2 * D_FF))
        z, k_all, v_all, lf, _, crow = _inproj(xs, None, g_mix3, w_left, w_gate, b_main, w_f, b_f, l,
                                               rows_s, rows_s, t_new, False, kv_s)
        kv_s = (k_all, v_all)
        cn_t = jnp.swapaxes(crow[0].reshape(N_HEADS, nb, t_new), 0, 1)
        ya = _fox_sample(z, cn_t, ck, cv, lc_t, l, t_new)
        mg, cst = _mix(z, ya, w_conv, w_pc_b, w_pa_b, l, rows_s, t_new, state_conv[l])
        x1 = _oproj(xs, None, mg, w_o_b, l, rows_s)
        xs, fst = _ffn(x1, g_ffn3, w_up_b, w_fconv, b_fconv3, w_down_b, l, rows_s, t_new,
                       state_ffn_conv[l])
        lf_s.append(lf); cst_s.append(cst); fst_s.append(fst[-1].reshape(nb, 2, 2 * D_FF))

    g_fin = g_final[None, :]
    y_prompt = _final_norm(xp, g_fin, TM_OUT, N_META)[None]
    y_sample = _final_norm(xs, g_fin, rows_s, 0).reshape(nb, t_new, D_MODEL)
    return (y_prompt, y_sample,
            kv_p[0].reshape(depth, 1, lp, N_HEADS, HEAD_DIM),
            kv_p[1].reshape(depth, 1, lp, N_HEADS, HEAD_DIM),
            jnp.stack(lf_p).reshape(depth, 1, lp, N_HEADS),
            jnp.stack(cst_p),
            jnp.stack(fst_p),
            kv_s[0].reshape(depth, nb, t_new, N_HEADS, HEAD_DIM),
            kv_s[1].reshape(depth, nb, t_new, N_HEADS, HEAD_DIM),
            jnp.stack(lf_s).reshape(depth, nb, t_new, N_HEADS),
            jnp.stack(cst_s),
            jnp.stack(fst_s))
```

```python
import functools

import jax
import jax.numpy as jnp
from jax import lax
from jax.experimental import pallas as pl
from jax.experimental.pallas import tpu as pltpu

F32 = jnp.float32
BF16 = jnp.bfloat16

D_MODEL = 2048
N_META = 16
CONV_DIM = D_MODEL // 2
HEAD_DIM = 128
ATTN_DIM = D_MODEL // 2
N_HEADS = ATTN_DIM // HEAD_DIM
D_FF = 2 * D_MODEL
EPS = 1e-6
ATTN_SCALE = HEAD_DIM ** -0.5
NEG_INF = -1e30
EXP_ZERO_BELOW = 106.0
F_OFF = 3 * CONV_DIM + 3 * ATTN_DIM
MAIN_DIM = F_OFF + 2 * D_MODEL

LANES = 128
VMEM_LIMIT = 60000 * 1024

Q_OFF = 3 * CONV_DIM
K_OFF = Q_OFF + ATTN_DIM
V_OFF = K_OFF + ATTN_DIM
G_OFF = V_OFF + ATTN_DIM

TN_IN = 1024
TF = 512
TM_BIG = 912
TM_SMALL = 432
TM_OUT = 512
ROW_CHUNK = 304
KV_CHUNK = 2048
SUFFIX_BLOCK = 256


def _params(n_axes):
    return pltpu.CompilerParams(dimension_semantics=("arbitrary",) * n_axes,
                                vmem_limit_bytes=VMEM_LIMIT)


def _dot(a, b):
    return jnp.dot(a, b, preferred_element_type=F32)


def _dot_nt(a, b):
    return lax.dot_general(a, b, (((1,), (1,)), ((), ())), preferred_element_type=F32)


def _dot_exact01(mask_bf16, x, nt=False):
    hi = x.astype(BF16)
    r1 = x - hi.astype(F32)
    mid = r1.astype(BF16)
    lo = (r1 - mid.astype(F32)).astype(BF16)
    if nt:
        return _dot(hi, mask_bf16) + _dot(mid, mask_bf16) + _dot(lo, mask_bf16)
    return _dot(mask_bf16, hi) + _dot(mask_bf16, mid) + _dot(mask_bf16, lo)


def _rmsnorm(x, g):
    ms = jnp.mean(x * x, axis=-1, keepdims=True)
    return (x * lax.rsqrt(ms + EPS)) * g


def _sigmoid(x):
    return 1.0 / (1.0 + jnp.exp(-x))


def _conv3(u, h0rows, h1rows, w, seg):
    tm = u.shape[0]
    r = lax.broadcasted_iota(jnp.int32, (tm, 1), 0)
    if seg < tm:
        r = r % seg
    um1 = jnp.where(r == 0, h1rows, pltpu.roll(u, 1, 0))
    um2 = jnp.where(r == 0, h0rows, jnp.where(r == 1, h1rows, pltpu.roll(u, 2, 0)))
    return um2 * w[0:1] + um1 * w[1:2] + u * w[2:3]


def _seg_rows(h, seg):
    nseg, c = h.shape
    return jnp.broadcast_to(h[:, None, :], (nseg, seg, c)).reshape(nseg * seg, c)


def _rows_after_meta(x_ref, meta_ref, i):
    x = x_ref[...]
    shifted = jnp.concatenate([meta_ref[...], x[:x.shape[0] - N_META]], axis=0)
    return jnp.where(i == 0, shifted, x)


def _meta_row_spec(tm):
    return pl.BlockSpec((pl.Element(tm), pl.Element(D_MODEL)),
                        lambda i, *_: (pl.multiple_of(jnp.maximum(i * tm - N_META, 0), 8), 0))


def _max_row_norms(x):
    cols = []
    for h in range(N_HEADS):
        xh = x[:, h * HEAD_DIM:(h + 1) * HEAD_DIM].astype(F32)
        cols.append(jnp.sqrt(jnp.max(jnp.sum(xh * xh, axis=1, keepdims=True), axis=0, keepdims=True)))
    return jnp.concatenate(cols, axis=0)


def _inproj_body(*refs, seg, use_carry, has_meta, n_alias):
    x_ref, g_ref, wl_ref, wg_ref, b_ref, wf_ref, bf_ref = refs[:7]
    (z_ref, k_ref, v_ref, lf_ref, ccol_ref, crow_ref, plan_ref,
     h_scr, carry_scr, tri_scr) = refs[7 + has_meta + n_alias:]
    i = pl.program_id(0)
    j = pl.program_id(1)
    tm = x_ref.shape[0]
    jq, jk, jv, jg = (Q_OFF // TN_IN, K_OFF // TN_IN, V_OFF // TN_IN, G_OFF // TN_IN)
    plan_lane = lax.broadcasted_iota(jnp.int32, (N_HEADS, LANES), 1)

    @pl.when(j == 0)
    def _():
        x = _rows_after_meta(x_ref, refs[7], i) if has_meta else x_ref[...]
        hb = _rmsnorm(x, g_ref[...]).astype(BF16)
        h_scr[...] = hb
        fl = _dot_nt(hb, wf_ref[...]) + bf_ref[...]
        lf = jnp.minimum(fl, 0.0) - jnp.log1p(jnp.exp(-jnp.abs(fl)))
        lf_ref[...] = lf[:, :N_HEADS]

        @pl.when(i == 0)
        def _():
            r = lax.broadcasted_iota(jnp.int32, (tm, tm), 0)
            c = lax.broadcasted_iota(jnp.int32, (tm, tm), 1)
            tri = c <= r
            if seg < tm:
                tri = tri & ((r // seg) == (c // seg))
            tri_scr[...] = jnp.where(tri, 1.0, 0.0).astype(BF16)

        lane = lax.broadcasted_iota(jnp.int32, (tm, LANES), 1)
        hi = jnp.where(lane < N_HEADS, lf, 0.0)
        hi_b = hi.astype(BF16).astype(F32)
        mid = hi - hi_b
        mid_b = mid.astype(BF16).astype(F32)
        lo_b = (mid - mid_b).astype(BF16).astype(F32)
        packed = hi_b + pltpu.roll(mid_b, N_HEADS, 1) + pltpu.roll(lo_b, 2 * N_HEADS, 1)
        part = _dot(tri_scr[...], packed.astype(BF16))
        cum = part + pltpu.roll(part, LANES - N_HEADS, 1) + pltpu.roll(part, LANES - 2 * N_HEADS, 1)
        if use_carry:
            @pl.when(i == 0)
            def _():
                carry_scr[...] = jnp.zeros_like(carry_scr)
            cum = cum + carry_scr[...]
            carry_scr[...] = cum[tm - 1:tm, :]
        ccol_ref[...] = cum
        crow = cum.T[:N_HEADS, :]
        crow_ref[...] = crow
        plan_ref[...] = jnp.where(plan_lane == 2, crow[:, 0:1],
                                  jnp.where(plan_lane == 3, crow[:, tm - 1:tm], 0.0))

    @pl.when(j < jg)
    def _():
        acc = _dot_nt(h_scr[...], wl_ref[...]) + b_ref[...]
        is_q = (j >= jq) & (j < jk)
        zb = jnp.where(is_q, acc * ATTN_SCALE, acc).astype(BF16)
        z_ref[...] = zb

        @pl.when(is_q)
        def _():
            plan_ref[...] = jnp.where(plan_lane == 0, _max_row_norms(zb), plan_ref[...])

        @pl.when((j >= jk) & (j < jv))
        def _():
            k_ref[...] = acc
            plan_ref[...] = jnp.where(plan_lane == 1, _max_row_norms(zb), plan_ref[...])

        @pl.when((j >= jv) & (j < jg))
        def _():
            v_ref[...] = acc

    @pl.when(j >= jg)
    def _():
        z_ref[...] = _sigmoid(_dot_nt(h_scr[...], wg_ref[...]) + b_ref[...]).astype(BF16)


def _inproj(x, meta, g, w_left, w_gate, b_main, w_f, b_f, layer, n_valid, tm, seg, use_carry, kv_prev):
    has_meta = meta is not None
    rows = x.shape[0] + (N_META if has_meta else 0)
    nm = rows // tm
    nj = MAIN_DIM // TN_IN
    jk, jv, jg = K_OFF // TN_IN, V_OFF // TN_IN, G_OFF // TN_IN
    assert TN_IN == ATTN_DIM
    nkv = ATTN_DIM // TN_IN
    depth = w_left.shape[0]
    n_alias = 0 if kv_prev is None else 2
    in_specs = [
        _meta_row_spec(tm) if has_meta else pl.BlockSpec((tm, D_MODEL), lambda i, j: (i, 0)),
        pl.BlockSpec((None, 1, D_MODEL), lambda i, j: (layer, 0, 0)),
        pl.BlockSpec((None, TN_IN, D_MODEL), lambda i, j: (layer, jnp.minimum(j, jg - 1), 0)),
        pl.BlockSpec((None, TN_IN, D_MODEL), lambda i, j: (layer, jnp.maximum(j - jg, 0), 0)),
        pl.BlockSpec((None, 1, TN_IN), lambda i, j: (layer, 0, j)),
        pl.BlockSpec((None, LANES, D_MODEL), lambda i, j: (layer, 0, 0)),
        pl.BlockSpec((None, 1, LANES), lambda i, j: (layer, 0, 0)),
    ]
    args = [x, g, w_left, w_gate, b_main, w_f, b_f]
    if has_meta:
        in_specs.append(pl.BlockSpec((N_META, D_MODEL), lambda i, j: (0, 0)))
        args.append(meta)
    aliases = {}
    if kv_prev is not None:
        in_specs += [pl.BlockSpec(memory_space=pl.ANY)] * 2
        args += list(kv_prev)
        aliases = {len(args) - 2: 1, len(args) - 1: 2}
    out_shape = (
        jax.ShapeDtypeStruct((rows, MAIN_DIM), BF16),
        jax.ShapeDtypeStruct((depth, n_valid, ATTN_DIM), F32),
        jax.ShapeDtypeStruct((depth, n_valid, ATTN_DIM), F32),
        jax.ShapeDtypeStruct((n_valid, N_HEADS), F32),
        jax.ShapeDtypeStruct((rows, LANES), F32),
        jax.ShapeDtypeStruct((nm, N_HEADS, tm), F32),
        jax.ShapeDtypeStruct((nm, N_HEADS, LANES), F32),
    )
    out_specs = (
        pl.BlockSpec((tm, TN_IN), lambda i, j: (i, j)),
        pl.BlockSpec((None, tm, TN_IN), lambda i, j: (layer, i, jnp.clip(j - jk, 0, nkv - 1)),
                     pipeline_mode=pl.Buffered(1)),
        pl.BlockSpec((None, tm, TN_IN), lambda i, j: (layer, i, jnp.clip(j - jv, 0, nkv - 1)),
                     pipeline_mode=pl.Buffered(1)),
        pl.BlockSpec((tm, N_HEADS), lambda i, j: (i, 0)),
        pl.BlockSpec((tm, LANES), lambda i, j: (i, 0)),
        pl.BlockSpec((None, N_HEADS, tm), lambda i, j: (i, 0, 0)),
        pl.BlockSpec((None, N_HEADS, LANES), lambda i, j: (i, 0, 0)),
    )
    return pl.pallas_call(
        functools.partial(_inproj_body, seg=seg, use_carry=use_carry, has_meta=has_meta, n_alias=n_alias),
        grid=(nm, nj),
        in_specs=in_specs,
        out_specs=out_specs,
        out_shape=out_shape,
        scratch_shapes=[pltpu.VMEM((tm, D_MODEL), BF16), pltpu.VMEM((1, LANES), F32),
                        pltpu.VMEM((tm, tm), BF16)],
        input_output_aliases=aliases,
        compiler_params=_params(2),
        name="inproj",
    )(*args)


def _tiles_to_visit(plan):
    qn, kn, c_first, c_last = (plan[:, :, i].T for i in range(4))
    nq = qn.shape[1]
    gap = (qn[:, :, None] * (kn[:, None, :] + kn[:, :, None])
           + c_first[:, :, None] - c_last[:, None, :])
    qi = jnp.arange(nq)[:, None]
    kb = jnp.arange(nq)[None, :]
    needed = (kb < qi)[None] & (gap >= -EXP_ZERO_BELOW)
    return 1 + jnp.max(jnp.where(needed, (qi - kb)[None], 0), axis=2).astype(jnp.int32)


def _fox_prompt_body(nvisit_ref, q_ref, k_ref, v_ref, ccol_ref, crow_ref, o_ref, m_scr, l_scr, acc_scr, cq_scr):
    h = pl.program_id(0)
    qi = pl.program_id(1)
    tq = q_ref.shape[0]
    tk = tq
    m_scr[...] = jnp.full_like(m_scr, NEG_INF)
    l_scr[...] = jnp.zeros_like(l_scr)
    acc_scr[...] = jnp.zeros_like(acc_scr)
    lane = lax.broadcasted_iota(jnp.int32, (tq, LANES), 1)
    cq_scr[...] = jnp.sum(jnp.where(lane == h, ccol_ref[...], 0.0), axis=1, keepdims=True)

    def tile(kb, diagonal):
        start = pl.multiple_of(kb * tk, ROW_CHUNK)
        k = k_ref[pl.ds(start, tk), :]
        v = v_ref[pl.ds(start, tk), :]
        ck = crow_ref[kb, pl.ds(h, 1), :]
        chunks = list(range(0, tq, ROW_CHUNK))

        def n_keys(r0):
            return r0 + ROW_CHUNK if diagonal else tk

        def scores(r0):
            nk = n_keys(r0)
            return _dot_nt(q_ref[r0:r0 + ROW_CHUNK, :], k[:nk])

        def softmax(r0, s):
            rows = slice(r0, r0 + ROW_CHUNK)
            nk = n_keys(r0)
            s = s + cq_scr[rows, :] - ck[:, :nk]
            if diagonal:
                r = lax.broadcasted_iota(jnp.int32, (ROW_CHUNK, nk), 0) + r0
                c = lax.broadcasted_iota(jnp.int32, (ROW_CHUNK, nk), 1)
                s = jnp.where(c <= r, s, NEG_INF)
            m_prev = m_scr[rows, :]
            m_new = jnp.maximum(m_prev, jnp.max(s, axis=1, keepdims=True))
            alpha = jnp.exp(m_prev - m_new)
            p = jnp.exp(s - m_new)
            l_scr[rows, :] = alpha * l_scr[rows, :] + jnp.sum(p, axis=1, keepdims=True)
            m_scr[rows, :] = m_new
            return alpha, p.astype(BF16)

        def values(r0, alpha, p):
            rows = slice(r0, r0 + ROW_CHUNK)
            acc_scr[rows, :] = alpha * acc_scr[rows, :] + _dot(p, v[:n_keys(r0)])

        s_next = scores(chunks[0])
        pending = None
        for idx, r0 in enumerate(chunks):
            s_cur = s_next
            if idx + 1 < len(chunks):
                s_next = scores(chunks[idx + 1])
            alpha, p = softmax(r0, s_cur)
            if pending is not None:
                values(*pending)
            pending = (r0, alpha, p)
        values(*pending)

    tile(qi, True)

    def older(kk, carry):
        tile(qi - kk, False)
        return carry

    lax.fori_loop(1, nvisit_ref[h, qi], older, 0)
    o_ref[...] = (acc_scr[...] / l_scr[...]).astype(BF16)


def _fox_prompt(z, ccol, crow, plan, tq):
    rows = z.shape[0]
    nq = rows // tq
    qb, kb, vb = Q_OFF // HEAD_DIM, K_OFF // HEAD_DIM, V_OFF // HEAD_DIM
    nvisit = _tiles_to_visit(plan)
    return pl.pallas_call(
        _fox_prompt_body,
        grid_spec=pltpu.PrefetchScalarGridSpec(
            num_scalar_prefetch=1,
            grid=(N_HEADS, nq),
            in_specs=[
                pl.BlockSpec((tq, HEAD_DIM), lambda h, qi, nv: (qi, qb + h)),
                pl.BlockSpec((rows, HEAD_DIM), lambda h, qi, nv: (0, kb + h)),
                pl.BlockSpec((rows, HEAD_DIM), lambda h, qi, nv: (0, vb + h)),
                pl.BlockSpec((tq, LANES), lambda h, qi, nv: (qi, 0)),
                pl.BlockSpec((nq, N_HEADS, tq), lambda h, qi, nv: (0, 0, 0)),
            ],
            out_specs=pl.BlockSpec((tq, HEAD_DIM), lambda h, qi, nv: (qi, h)),
            scratch_shapes=[pltpu.VMEM((tq, 1), F32), pltpu.VMEM((tq, 1), F32),
                            pltpu.VMEM((tq, HEAD_DIM), F32), pltpu.VMEM((tq, 1), F32)]),
        out_shape=jax.ShapeDtypeStruct((rows, ATTN_DIM), BF16),
        compiler_params=_params(2),
        name="fox_prompt",
    )(nvisit, z, z, z, ccol, crow)


def _fox_sample_body(*refs):
    (q_ref, kn_ref, vn_ref, cn_ref, lc_ref, ck_ref, cv_ref, o_ref,
     qbd_scr, m_scr, l_scr, acc_scr, tail_scr) = refs
    c = pl.program_id(1)
    nc = pl.num_programs(1)
    t_new = q_ref.shape[0]
    nrow = t_new * N_HEADS
    kc = ck_ref.shape[0] // N_HEADS
    nblk = kc // SUFFIX_BLOCK
    cn = cn_ref[...]
    cq = jnp.concatenate([cn[:, t:t + 1] for t in range(t_new)], axis=0)
    col_head = lax.broadcasted_iota(jnp.int32, (N_HEADS, ATTN_DIM), 1) // HEAD_DIM
    row_head = lax.broadcasted_iota(jnp.int32, (N_HEADS, ATTN_DIM), 0)
    head_mask = col_head == row_head

    @pl.when(c == 0)
    def _():
        q = q_ref[...].astype(F32)
        q3 = jnp.where(head_mask[None], jnp.broadcast_to(q[:, None, :], (t_new, N_HEADS, ATTN_DIM)), 0.0)
        qbd_scr[...] = q3.reshape(nrow, ATTN_DIM).astype(BF16)
        m_scr[...] = jnp.full_like(m_scr, NEG_INF)
        l_scr[...] = jnp.zeros_like(l_scr)
        acc_scr[...] = jnp.zeros_like(acc_scr)
        tail_scr[...] = jnp.zeros_like(tail_scr)

    def online(s, vals):
        m_prev = m_scr[...]
        m_new = jnp.maximum(m_prev, jnp.max(s, axis=1, keepdims=True))
        alpha = jnp.exp(m_prev - m_new)
        p = jnp.exp(s - m_new)
        l_scr[...] = alpha * l_scr[...] + jnp.sum(p, axis=1, keepdims=True)
        acc_scr[...] = alpha * acc_scr[...] + _dot(p.astype(BF16), vals)
        m_scr[...] = m_new

    lc = lc_ref[...]
    x = jnp.concatenate([lc[:, b * SUFFIX_BLOCK:(b + 1) * SUFFIX_BLOCK] for b in range(nblk)], axis=0)
    jj = lax.broadcasted_iota(jnp.int32, (SUFFIX_BLOCK, SUFFIX_BLOCK), 0)
    ss = lax.broadcasted_iota(jnp.int32, (SUFFIX_BLOCK, SUFFIX_BLOCK), 1)
    excl = _dot_exact01(jnp.where(jj > ss, 1.0, 0.0).astype(BF16), x, nt=True)
    off = tail_scr[...]
    pieces = [None] * nblk
    for b in range(nblk - 1, -1, -1):
        eb = excl[b * N_HEADS:(b + 1) * N_HEADS, :]
        pieces[b] = eb + off
        off = off + (eb[:, 0:1] + x[b * N_HEADS:(b + 1) * N_HEADS, 0:1])
    tail_scr[...] = off
    suffix = jnp.concatenate(pieces, axis=1)

    def all_heads(ref):
        return jnp.concatenate([ref[pl.ds(h, kc, stride=N_HEADS), :].astype(BF16)
                                for h in range(N_HEADS)], axis=1)

    s = _dot_nt(qbd_scr[...], all_heads(ck_ref))
    s = (s.reshape(t_new, N_HEADS, kc) + cq.reshape(t_new, N_HEADS, 1) + suffix[None]).reshape(nrow, kc)
    online(s, all_heads(cv_ref))

    @pl.when(c == nc - 1)
    def _():
        sn = _dot_nt(qbd_scr[...], kn_ref[...])
        sn = sn.reshape(t_new, N_HEADS, t_new) + cq.reshape(t_new, N_HEADS, 1) + (-cn)[None]
        tq = lax.broadcasted_iota(jnp.int32, (t_new, N_HEADS, t_new), 0)
        tk = lax.broadcasted_iota(jnp.int32, (t_new, N_HEADS, t_new), 2)
        sn = jnp.where(tk <= tq, sn, NEG_INF).reshape(nrow, t_new)
        online(sn, vn_ref[...])
        o = (acc_scr[...] / l_scr[...]).reshape(t_new, N_HEADS, ATTN_DIM)
        o_ref[...] = jnp.sum(jnp.where(head_mask[None], o, 0.0), axis=1).astype(BF16)


def _fox_sample(z, cn_t, cache_k, cache_v, lc_t, layer, t_new):
    nb = cache_k.shape[1]
    past = cache_k.shape[2] // N_HEADS
    nc = past // KV_CHUNK
    nrow = t_new * N_HEADS
    qb, kb, vb = Q_OFF // ATTN_DIM, K_OFF // ATTN_DIM, V_OFF // ATTN_DIM
    cache_spec = pl.BlockSpec((None, None, KV_CHUNK * N_HEADS, HEAD_DIM),
                              lambda b, c: (layer, b, nc - 1 - c, 0))
    return pl.pallas_call(
        _fox_sample_body,
        grid=(nb, nc),
        in_specs=[
            pl.BlockSpec((t_new, ATTN_DIM), lambda b, c: (b, qb)),
            pl.BlockSpec((t_new, ATTN_DIM), lambda b, c: (b, kb)),
            pl.BlockSpec((t_new, ATTN_DIM), lambda b, c: (b, vb)),
            pl.BlockSpec((None, N_HEADS, t_new), lambda b, c: (b, 0, 0)),
            pl.BlockSpec((None, None, N_HEADS, KV_CHUNK), lambda b, c: (layer, b, 0, nc - 1 - c)),
            cache_spec, cache_spec,
        ],
        out_specs=pl.BlockSpec((t_new, ATTN_DIM), lambda b, c: (b, 0)),
        out_shape=jax.ShapeDtypeStruct((nb * t_new, ATTN_DIM), BF16),
        scratch_shapes=[pltpu.VMEM((nrow, ATTN_DIM), BF16), pltpu.VMEM((nrow, 1), F32),
                        pltpu.VMEM((nrow, 1), F32), pltpu.VMEM((nrow, ATTN_DIM), F32),
                        pltpu.VMEM((N_HEADS, 1), F32)],
        compiler_params=_params(2),
        name="fox_sample",
    )(z, z, z, cn_t, lc_t, cache_k, cache_v)


def _mix_body(*refs, seg, has_hist):
    zc_ref, ya_ref, gc_ref, ga_ref, wc_ref, wpc_ref, wpa_ref = refs[:7]
    n_in = 8 if has_hist else 7
    mg_ref, st_ref = refs[n_in:n_in + 2]
    i = pl.program_id(0)
    tm = zc_ref.shape[0]
    xc = zc_ref[:, 0:CONV_DIM].astype(F32)
    bc = zc_ref[:, CONV_DIM:2 * CONV_DIM].astype(F32)
    cc = zc_ref[:, 2 * CONV_DIM:3 * CONV_DIM].astype(F32)
    u = cc * xc
    if has_hist:
        hist = refs[7][...]
        h0 = _seg_rows(hist[:, 0, :], seg)
        h1 = _seg_rows(hist[:, 1, :], seg)
        nseg = tm // seg
        st_ref[...] = u.reshape(nseg, seg, CONV_DIM)[:, seg - 2:seg, :]
    else:
        carry_scr = refs[n_in + 2]

        @pl.when(i == 0)
        def _():
            carry_scr[...] = jnp.zeros_like(carry_scr)
        h0 = carry_scr[0:1, :]
        h1 = carry_scr[1:2, :]
        st_ref[...] = u[tm - 2:tm, :][None]
    yc = (bc * _conv3(u, h0, h1, wc_ref[...], seg)).astype(BF16)
    if not has_hist:
        carry_scr[0:2, :] = u[tm - 2:tm, :]
    t1 = _dot(yc, wpc_ref[...])
    t2 = _dot(ya_ref[...], wpa_ref[...])
    mg_ref[...] = (gc_ref[...].astype(F32) * t1 + ga_ref[...].astype(F32) * t2).astype(BF16)


def _mix(z, ya, w_conv, w_pc, w_pa, layer, tm, seg, hist):
    rows = z.shape[0]
    nm = rows // tm
    has_hist = hist is not None
    nseg = tm // seg if has_hist else 1
    in_specs = [
        pl.BlockSpec((tm, 3 * CONV_DIM), lambda i: (i, 0)),
        pl.BlockSpec((tm, ATTN_DIM), lambda i: (i, 0)),
        pl.BlockSpec((tm, D_MODEL), lambda i: (i, G_OFF // D_MODEL)),
        pl.BlockSpec((tm, D_MODEL), lambda i: (i, G_OFF // D_MODEL + 1)),
        pl.BlockSpec((None, 3, CONV_DIM), lambda i: (layer, 0, 0)),
        pl.BlockSpec((None, CONV_DIM, D_MODEL), lambda i: (layer, 0, 0)),
        pl.BlockSpec((None, ATTN_DIM, D_MODEL), lambda i: (layer, 0, 0)),
    ]
    args = [z, ya, z, z, w_conv, w_pc, w_pa]
    scratch = []
    if has_hist:
        in_specs.append(pl.BlockSpec((nseg, 2, CONV_DIM), lambda i: (0, 0, 0)))
        args.append(hist)
    else:
        scratch.append(pltpu.VMEM((8, CONV_DIM), F32))
    return pl.pallas_call(
        functools.partial(_mix_body, seg=seg, has_hist=has_hist),
        grid=(nm,),
        in_specs=in_specs,
        out_specs=(pl.BlockSpec((tm, D_MODEL), lambda i: (i, 0)),
                   pl.BlockSpec((nseg, 2, CONV_DIM), lambda i: (0, 0, 0))),
        out_shape=(jax.ShapeDtypeStruct((rows, D_MODEL), BF16),
                   jax.ShapeDtypeStruct((nseg, 2, CONV_DIM), F32)),
        scratch_shapes=scratch,
        compiler_params=_params(1),
        name="mix",
    )(*args)


def _oproj_body(x_ref, mg_ref, wo_ref, *rest):
    o_ref = rest[-1]
    x = _rows_after_meta(x_ref, rest[0], pl.program_id(0)) if len(rest) == 2 else x_ref[...]
    o_ref[...] = x + _dot(mg_ref[...], wo_ref[...])


def _oproj(x, meta, mg, w_o, layer, tm):
    has_meta = meta is not None
    rows = mg.shape[0]
    in_specs = [_meta_row_spec(tm) if has_meta else pl.BlockSpec((tm, D_MODEL), lambda i: (i, 0)),
                pl.BlockSpec((tm, D_MODEL), lambda i: (i, 0)),
                pl.BlockSpec((None, D_MODEL, D_MODEL), lambda i: (layer, 0, 0))]
    args = [x, mg, w_o]
    if has_meta:
        in_specs.append(pl.BlockSpec((N_META, D_MODEL), lambda i: (0, 0)))
        args.append(meta)
    return pl.pallas_call(
        _oproj_body,
        grid=(rows // tm,),
        in_specs=in_specs,
        out_specs=pl.BlockSpec((tm, D_MODEL), lambda i: (i, 0)),
        out_shape=jax.ShapeDtypeStruct((rows, D_MODEL), F32),
        compiler_params=_params(1),
        name="oproj",
    )(*args)


def _ffn_body(*refs, seg, has_hist):
    (x_ref, g_ref, wua_ref, wub_ref, wca_ref, wcb_ref, ba_ref, bb_ref, wd_ref) = refs[:9]
    n_in = 11 if has_hist else 9
    o_ref, st_ref, h_scr = refs[n_in:n_in + 3]
    i = pl.program_id(0)
    j = pl.program_id(1)
    tm = x_ref.shape[0]
    tf = wua_ref.shape[1]

    @pl.when(j == 0)
    def _():
        x = x_ref[...]
        h_scr[...] = _rmsnorm(x, g_ref[...]).astype(BF16)
        o_ref[...] = x

    rc = ROW_CHUNK if (not has_hist and tm % ROW_CHUNK == 0) else tm
    chunks = list(range(0, tm, rc))

    def up(r0):
        hb = h_scr[r0:r0 + rc, :]
        return _dot(hb, wua_ref[...]), _dot(hb, wub_ref[...])

    def gate(u, hist, w_ref, bias_ref):
        return _conv3(u, hist[0], hist[1], w_ref[...], min(seg, rc)) + bias_ref[...]

    def down(r0, gated):
        o_ref[r0:r0 + rc, :] += _dot(gated, wd_ref[...])

    if has_hist:
        nseg = tm // seg
        hist_a, hist_b = ((_seg_rows(r[:, 0, :], seg), _seg_rows(r[:, 1, :], seg))
                          for r in (refs[9][...], refs[10][...]))
    else:
        ca_scr, cb_scr = refs[n_in + 3:n_in + 5]

        @pl.when(i == 0)
        def _():
            ca_scr[j] = jnp.zeros((8, tf), F32)
            cb_scr[j] = jnp.zeros((8, tf), F32)
        hist_a = (ca_scr[j, 0:1, :], ca_scr[j, 1:2, :])
        hist_b = (cb_scr[j, 0:1, :], cb_scr[j, 1:2, :])

    u_next = up(chunks[0])
    pending = None
    for idx, r0 in enumerate(chunks):
        ua, ub = u_next
        if idx + 1 < len(chunks):
            u_next = up(chunks[idx + 1])
        a = gate(ua, hist_a, wca_ref, ba_ref)
        b = gate(ub, hist_b, wcb_ref, bb_ref)
        gated = ((a * _sigmoid(a)) * b).astype(BF16)
        hist_a = (ua[rc - 2:rc - 1, :], ua[rc - 1:rc, :])
        hist_b = (ub[rc - 2:rc - 1, :], ub[rc - 1:rc, :])
        if pending is not None:
            down(*pending)
        pending = (r0, gated)
    down(*pending)

    if has_hist:
        st_ref[:, :, 0, :] = ua.reshape(nseg, seg, tf)[:, seg - 2:seg, :]
        st_ref[:, :, 1, :] = ub.reshape(nseg, seg, tf)[:, seg - 2:seg, :]
    else:
        st_ref[0, :, 0, :] = ua[rc - 2:rc, :]
        st_ref[0, :, 1, :] = ub[rc - 2:rc, :]
        ca_scr[j, 0:2, :] = ua[rc - 2:rc, :]
        cb_scr[j, 0:2, :] = ub[rc - 2:rc, :]


def _ffn(x, g, w_up, w_fconv, b_fconv, w_down, layer, tm, seg, hist):
    rows = x.shape[0]
    nm = rows // tm
    nf = D_FF // TF
    has_hist = hist is not None
    nseg = tm // seg if has_hist else 1
    in_specs = [
        pl.BlockSpec((tm, D_MODEL), lambda i, j: (i, 0)),
        pl.BlockSpec((None, 1, D_MODEL), lambda i, j: (layer, 0, 0)),
        pl.BlockSpec((None, D_MODEL, TF), lambda i, j: (layer, 0, j)),
        pl.BlockSpec((None, D_MODEL, TF), lambda i, j: (layer, 0, nf + j)),
        pl.BlockSpec((None, 3, TF), lambda i, j: (layer, 0, j)),
        pl.BlockSpec((None, 3, TF), lambda i, j: (layer, 0, nf + j)),
        pl.BlockSpec((None, 1, TF), lambda i, j: (layer, 0, j)),
        pl.BlockSpec((None, 1, TF), lambda i, j: (layer, 0, nf + j)),
        pl.BlockSpec((None, TF, D_MODEL), lambda i, j: (layer, j, 0)),
    ]
    args = [x, g, w_up, w_up, w_fconv, w_fconv, b_fconv, b_fconv, w_down]
    scratch = [pltpu.VMEM((tm, D_MODEL), BF16)]
    if has_hist:
        in_specs += [pl.BlockSpec((nseg, 2, TF), lambda i, j: (0, 0, j)),
                     pl.BlockSpec((nseg, 2, TF), lambda i, j: (0, 0, nf + j))]
        args += [hist, hist]
    else:
        scratch += [pltpu.VMEM((nf, 8, TF), F32), pltpu.VMEM((nf, 8, TF), F32)]
    return pl.pallas_call(
        functools.partial(_ffn_body, seg=seg, has_hist=has_hist),
        grid=(nm, nf),
        in_specs=in_specs,
        out_specs=(pl.BlockSpec((tm, D_MODEL), lambda i, j: (i, 0)),
                   pl.BlockSpec((None, nseg, 2, 2, TF), lambda i, j: (i, 0, 0, 0, j))),
        out_shape=(jax.ShapeDtypeStruct((rows, D_MODEL), F32),
                   jax.ShapeDtypeStruct((nm, nseg, 2, 2, D_FF), F32)),
        scratch_shapes=scratch,
        compiler_params=_params(2),
        name="ffn",
    )(*args)


def _final_norm_body(x_ref, g_ref, o_ref):
    o_ref[...] = _rmsnorm(x_ref[...], g_ref[...])


def _final_norm(x, g, tm, skip_rows):
    rows = x.shape[0] - skip_rows
    return pl.pallas_call(
        _final_norm_body,
        grid=(rows // tm,),
        in_specs=[pl.BlockSpec((pl.Element(tm), pl.Element(D_MODEL)),
                               lambda i: (pl.multiple_of(i * tm + skip_rows, 8), 0)),
                  pl.BlockSpec((1, D_MODEL), lambda i: (0, 0))],
        out_specs=pl.BlockSpec((tm, D_MODEL), lambda i: (i, 0)),
        out_shape=jax.ShapeDtypeStruct((rows, D_MODEL), F32),
        compiler_params=_params(1),
        name="final_norm",
    )(x, g)


def kernel(x_prompt, x_sample, cache_k, cache_v, cache_logf, state_conv, state_ffn_conv, meta_tokens,
           g_mix, w_in, b_in, w_conv, w_pc, w_pa, w_o, g_ffn, w_up, w_fconv, b_fconv, w_down, g_final):
    depth = w_in.shape[0]
    batch, seq, _ = x_prompt.shape
    nb, t_new, _ = x_sample.shape
    past = cache_k.shape[2]
    lp = N_META + seq
    assert batch == 1 and past % KV_CHUNK == 0 and lp % TM_BIG == 0 and lp % TM_SMALL == 0
    rows_s = nb * t_new

    w_left = jnp.swapaxes(w_in, 1, 2).astype(BF16)
    w_gate = w_left[:, F_OFF + N_HEADS:]
    w_f = jnp.pad(w_left[:, F_OFF:F_OFF + N_HEADS], ((0, 0), (0, LANES - N_HEADS), (0, 0)))
    b_main = jnp.concatenate([b_in[:, :F_OFF], b_in[:, F_OFF + N_HEADS:]], axis=1)[:, None, :]
    b_f = jnp.pad(b_in[:, F_OFF:F_OFF + N_HEADS], ((0, 0), (0, LANES - N_HEADS)))[:, None, :]
    w_pc_b, w_pa_b, w_o_b = w_pc.astype(BF16), w_pa.astype(BF16), w_o.astype(BF16)
    w_up_b, w_down_b = w_up.astype(BF16), w_down.astype(BF16)
    g_mix3, g_ffn3, b_fconv3 = g_mix[:, None, :], g_ffn[:, None, :], b_fconv[:, None, :]

    xp, meta = x_prompt[0], meta_tokens.astype(F32)
    xs = x_sample.reshape(rows_s, D_MODEL)
    ck = cache_k.reshape(depth, nb, past * N_HEADS, HEAD_DIM)
    cv = cache_v.reshape(depth, nb, past * N_HEADS, HEAD_DIM)
    lc_t = jnp.swapaxes(cache_logf.astype(F32), 2, 3)

    kv_p = kv_s = None
    lf_p, lf_s, cst_p, cst_s, fst_p, fst_s = [], [], [], [], [], []
    for l in range(depth):
        z, k_all, v_all, lf, ccol, crow, plan = _inproj(xp, meta, g_mix3, w_left, w_gate, b_main, w_f, b_f,
                                                         l, lp, TM_BIG, TM_BIG, True, kv_p)
        kv_p = (k_all, v_all)
        ya = _fox_prompt(z, ccol, crow, plan, TM_BIG)
        mg, cst = _mix(z, ya, w_conv, w_pc_b, w_pa_b, l, TM_SMALL, TM_SMALL, None)
        x1 = _oproj(xp, meta, mg, w_o_b, l, TM_SMALL)
        xp, fst = _ffn(x1, g_ffn3, w_up_b, w_fconv, b_fconv3, w_down_b, l, TM_BIG, TM_BIG, None)
        meta = None
        lf_p.append(lf); cst_p.append(cst); fst_p.append(fst[-1].reshape(1, 2, 2 * D_FF))
        z, k_all, v_all, lf, _, crow, _ = _inproj(xs, None, g_mix3, w_left, w_gate, b_main, w_f, b_f, l,
                                                  rows_s, rows_s, t_new, False, kv_s)
        kv_s = (k_all, v_all)
        cn_t = jnp.swapaxes(crow[0].reshape(N_HEADS, nb, t_new), 0, 1)
        ya = _fox_sample(z, cn_t, ck, cv, lc_t, l, t_new)
        mg, cst = _mix(z, ya, w_conv, w_pc_b, w_pa_b, l, rows_s, t_new, state_conv[l])
        x1 = _oproj(xs, None, mg, w_o_b, l, rows_s)
        xs, fst = _ffn(x1, g_ffn3, w_up_b, w_fconv, b_fconv3, w_down_b, l, rows_s, t_new,
                       state_ffn_conv[l])
        lf_s.append(lf); cst_s.append(cst); fst_s.append(fst[-1].reshape(nb, 2, 2 * D_FF))

    g_fin = g_final[None, :]
    y_prompt = _final_norm(xp, g_fin, TM_OUT, N_META)[None]
    y_sample = _final_norm(xs, g_fin, rows_s, 0).reshape(nb, t_new, D_MODEL)
    return (y_prompt, y_sample,
            kv_p[0].reshape(depth, 1, lp, N_HEADS, HEAD_DIM),
            kv_p[1].reshape(depth, 1, lp, N_HEADS, HEAD_DIM),
            jnp.stack(lf_p).reshape(depth, 1, lp, N_HEADS),
            jnp.stack(cst_p),
            jnp.stack(fst_p),
            kv_s[0].reshape(depth, nb, t_new, N_HEADS, HEAD_DIM),
            kv_s[1].reshape(depth, nb, t_new, N_HEADS, HEAD_DIM),
            jnp.stack(lf_s).reshape(depth, nb, t_new, N_HEADS),
            jnp.stack(cst_s),
            jnp.stack(fst_s))
```

```python
import functools

import jax
import jax.numpy as jnp
from jax import lax
from jax.experimental import pallas as pl
from jax.experimental.pallas import tpu as pltpu

F32 = jnp.float32
BF16 = jnp.bfloat16

D_MODEL = 2048
N_META = 16
CONV_DIM = D_MODEL // 2
HEAD_DIM = 128
ATTN_DIM = D_MODEL // 2
N_HEADS = ATTN_DIM // HEAD_DIM
D_FF = 2 * D_MODEL
EPS = 1e-6
ATTN_SCALE = HEAD_DIM ** -0.5
NEG_INF = -1e30
EXP_ZERO_BELOW = 106.0
F_OFF = 3 * CONV_DIM + 3 * ATTN_DIM
MAIN_DIM = F_OFF + 2 * D_MODEL

LANES = 128
VMEM_LIMIT = 60000 * 1024

Q_OFF = 3 * CONV_DIM
K_OFF = Q_OFF + ATTN_DIM
V_OFF = K_OFF + ATTN_DIM
G_OFF = V_OFF + ATTN_DIM

TN_IN = 1024
TF = 512
TM_BIG = 912
TM_SMALL = 432
TM_OUT = 512
ROW_CHUNK = 304
HEADS_PER_STEP = 2
KV_CHUNK = 2048
SUFFIX_BLOCK = 256


def _params(n_axes):
    return pltpu.CompilerParams(dimension_semantics=("arbitrary",) * n_axes,
                                vmem_limit_bytes=VMEM_LIMIT)


def _dot(a, b):
    return jnp.dot(a, b, preferred_element_type=F32)


def _dot_nt(a, b):
    return lax.dot_general(a, b, (((1,), (1,)), ((), ())), preferred_element_type=F32)


def _dot_exact01(mask_bf16, x, nt=False):
    hi = x.astype(BF16)
    r1 = x - hi.astype(F32)
    mid = r1.astype(BF16)
    lo = (r1 - mid.astype(F32)).astype(BF16)
    if nt:
        return _dot(hi, mask_bf16) + _dot(mid, mask_bf16) + _dot(lo, mask_bf16)
    return _dot(mask_bf16, hi) + _dot(mask_bf16, mid) + _dot(mask_bf16, lo)


def _rmsnorm(x, g):
    ms = jnp.mean(x * x, axis=-1, keepdims=True)
    return (x * lax.rsqrt(ms + EPS)) * g


def _sigmoid(x):
    return 1.0 / (1.0 + jnp.exp(-x))


def _conv3(u, h0rows, h1rows, w, seg):
    tm = u.shape[0]
    r = lax.broadcasted_iota(jnp.int32, (tm, 1), 0)
    if seg < tm:
        r = r % seg
    um1 = jnp.where(r == 0, h1rows, pltpu.roll(u, 1, 0))
    um2 = jnp.where(r == 0, h0rows, jnp.where(r == 1, h1rows, pltpu.roll(u, 2, 0)))
    return um2 * w[0:1] + um1 * w[1:2] + u * w[2:3]


def _seg_rows(h, seg):
    nseg, c = h.shape
    return jnp.broadcast_to(h[:, None, :], (nseg, seg, c)).reshape(nseg * seg, c)


def _rows_after_meta(x_ref, meta_ref, i):
    x = x_ref[...]
    shifted = jnp.concatenate([meta_ref[...], x[:x.shape[0] - N_META]], axis=0)
    return jnp.where(i == 0, shifted, x)


def _meta_row_spec(tm):
    return pl.BlockSpec((pl.Element(tm), pl.Element(D_MODEL)),
                        lambda i, *_: (pl.multiple_of(jnp.maximum(i * tm - N_META, 0), 8), 0))


def _max_row_norms(x):
    cols = []
    for h in range(N_HEADS):
        xh = x[:, h * HEAD_DIM:(h + 1) * HEAD_DIM].astype(F32)
        cols.append(jnp.sqrt(jnp.max(jnp.sum(xh * xh, axis=1, keepdims=True), axis=0, keepdims=True)))
    return jnp.concatenate(cols, axis=0)


def _inproj_body(*refs, seg, use_carry, has_meta, n_alias):
    x_ref, g_ref, wl_ref, wg_ref, b_ref, wf_ref, bf_ref = refs[:7]
    (z_ref, k_ref, v_ref, lf_ref, ccol_ref, crow_ref, plan_ref,
     h_scr, carry_scr, tri_scr) = refs[7 + has_meta + n_alias:]
    i = pl.program_id(0)
    j = pl.program_id(1)
    tm = x_ref.shape[0]
    jq, jk, jv, jg = (Q_OFF // TN_IN, K_OFF // TN_IN, V_OFF // TN_IN, G_OFF // TN_IN)
    plan_lane = lax.broadcasted_iota(jnp.int32, (N_HEADS, LANES), 1)

    @pl.when(j == 0)
    def _():
        x = _rows_after_meta(x_ref, refs[7], i) if has_meta else x_ref[...]
        hb = _rmsnorm(x, g_ref[...]).astype(BF16)
        h_scr[...] = hb
        fl = _dot_nt(hb, wf_ref[...]) + bf_ref[...]
        lf = jnp.minimum(fl, 0.0) - jnp.log1p(jnp.exp(-jnp.abs(fl)))
        lf_ref[...] = lf[:, :N_HEADS]

        @pl.when(i == 0)
        def _():
            r = lax.broadcasted_iota(jnp.int32, (tm, tm), 0)
            c = lax.broadcasted_iota(jnp.int32, (tm, tm), 1)
            tri = c <= r
            if seg < tm:
                tri = tri & ((r // seg) == (c // seg))
            tri_scr[...] = jnp.where(tri, 1.0, 0.0).astype(BF16)

        lane = lax.broadcasted_iota(jnp.int32, (tm, LANES), 1)
        hi = jnp.where(lane < N_HEADS, lf, 0.0)
        hi_b = hi.astype(BF16).astype(F32)
        mid = hi - hi_b
        mid_b = mid.astype(BF16).astype(F32)
        lo_b = (mid - mid_b).astype(BF16).astype(F32)
        packed = hi_b + pltpu.roll(mid_b, N_HEADS, 1) + pltpu.roll(lo_b, 2 * N_HEADS, 1)
        part = _dot(tri_scr[...], packed.astype(BF16))
        cum = part + pltpu.roll(part, LANES - N_HEADS, 1) + pltpu.roll(part, LANES - 2 * N_HEADS, 1)
        if use_carry:
            @pl.when(i == 0)
            def _():
                carry_scr[...] = jnp.zeros_like(carry_scr)
            cum = cum + carry_scr[...]
            carry_scr[...] = cum[tm - 1:tm, :]
        ccol_ref[...] = cum
        crow = cum.T[:N_HEADS, :]
        crow_ref[...] = crow
        plan_ref[...] = jnp.where(plan_lane == 2, crow[:, 0:1],
                                  jnp.where(plan_lane == 3, crow[:, tm - 1:tm], 0.0))

    @pl.when(j < jg)
    def _():
        acc = _dot_nt(h_scr[...], wl_ref[...]) + b_ref[...]
        is_q = (j >= jq) & (j < jk)
        zb = jnp.where(is_q, acc * ATTN_SCALE, acc).astype(BF16)
        z_ref[...] = zb

        @pl.when(is_q)
        def _():
            plan_ref[...] = jnp.where(plan_lane == 0, _max_row_norms(zb), plan_ref[...])

        @pl.when((j >= jk) & (j < jv))
        def _():
            k_ref[...] = acc
            plan_ref[...] = jnp.where(plan_lane == 1, _max_row_norms(zb), plan_ref[...])

        @pl.when((j >= jv) & (j < jg))
        def _():
            v_ref[...] = acc

    @pl.when(j >= jg)
    def _():
        z_ref[...] = _sigmoid(_dot_nt(h_scr[...], wg_ref[...]) + b_ref[...]).astype(BF16)


def _inproj(x, meta, g, w_left, w_gate, b_main, w_f, b_f, layer, n_valid, tm, seg, use_carry, kv_prev):
    has_meta = meta is not None
    rows = x.shape[0] + (N_META if has_meta else 0)
    nm = rows // tm
    nj = MAIN_DIM // TN_IN
    jk, jv, jg = K_OFF // TN_IN, V_OFF // TN_IN, G_OFF // TN_IN
    assert TN_IN == ATTN_DIM
    nkv = ATTN_DIM // TN_IN
    depth = w_left.shape[0]
    n_alias = 0 if kv_prev is None else 2
    in_specs = [
        _meta_row_spec(tm) if has_meta else pl.BlockSpec((tm, D_MODEL), lambda i, j: (i, 0)),
        pl.BlockSpec((None, 1, D_MODEL), lambda i, j: (layer, 0, 0)),
        pl.BlockSpec((None, TN_IN, D_MODEL), lambda i, j: (layer, jnp.minimum(j, jg - 1), 0)),
        pl.BlockSpec((None, TN_IN, D_MODEL), lambda i, j: (layer, jnp.maximum(j - jg, 0), 0)),
        pl.BlockSpec((None, 1, TN_IN), lambda i, j: (layer, 0, j)),
        pl.BlockSpec((None, LANES, D_MODEL), lambda i, j: (layer, 0, 0)),
        pl.BlockSpec((None, 1, LANES), lambda i, j: (layer, 0, 0)),
    ]
    args = [x, g, w_left, w_gate, b_main, w_f, b_f]
    if has_meta:
        in_specs.append(pl.BlockSpec((N_META, D_MODEL), lambda i, j: (0, 0)))
        args.append(meta)
    aliases = {}
    if kv_prev is not None:
        in_specs += [pl.BlockSpec(memory_space=pl.ANY)] * 2
        args += list(kv_prev)
        aliases = {len(args) - 2: 1, len(args) - 1: 2}
    out_shape = (
        jax.ShapeDtypeStruct((rows, MAIN_DIM), BF16),
        jax.ShapeDtypeStruct((depth, n_valid, ATTN_DIM), F32),
        jax.ShapeDtypeStruct((depth, n_valid, ATTN_DIM), F32),
        jax.ShapeDtypeStruct((n_valid, N_HEADS), F32),
        jax.ShapeDtypeStruct((rows, LANES), F32),
        jax.ShapeDtypeStruct((nm, N_HEADS, tm), F32),
        jax.ShapeDtypeStruct((nm, N_HEADS, LANES), F32),
    )
    out_specs = (
        pl.BlockSpec((tm, TN_IN), lambda i, j: (i, j)),
        pl.BlockSpec((None, tm, TN_IN), lambda i, j: (layer, i, jnp.clip(j - jk, 0, nkv - 1)),
                     pipeline_mode=pl.Buffered(1)),
        pl.BlockSpec((None, tm, TN_IN), lambda i, j: (layer, i, jnp.clip(j - jv, 0, nkv - 1)),
                     pipeline_mode=pl.Buffered(1)),
        pl.BlockSpec((tm, N_HEADS), lambda i, j: (i, 0)),
        pl.BlockSpec((tm, LANES), lambda i, j: (i, 0)),
        pl.BlockSpec((None, N_HEADS, tm), lambda i, j: (i, 0, 0)),
        pl.BlockSpec((None, N_HEADS, LANES), lambda i, j: (i, 0, 0)),
    )
    return pl.pallas_call(
        functools.partial(_inproj_body, seg=seg, use_carry=use_carry, has_meta=has_meta, n_alias=n_alias),
        grid=(nm, nj),
        in_specs=in_specs,
        out_specs=out_specs,
        out_shape=out_shape,
        scratch_shapes=[pltpu.VMEM((tm, D_MODEL), BF16), pltpu.VMEM((1, LANES), F32),
                        pltpu.VMEM((tm, tm), BF16)],
        input_output_aliases=aliases,
        compiler_params=_params(2),
        name="inproj",
    )(*args)


def _tiles_to_visit(plan):
    qn, kn, c_first, c_last = (plan[:, :, i].T for i in range(4))
    nq = qn.shape[1]
    gap = (qn[:, :, None] * (kn[:, None, :] + kn[:, :, None])
           + c_first[:, :, None] - c_last[:, None, :])
    qi = jnp.arange(nq)[:, None]
    kb = jnp.arange(nq)[None, :]
    needed = (kb < qi)[None] & (gap >= -EXP_ZERO_BELOW)
    return 1 + jnp.max(jnp.where(needed, (qi - kb)[None], 0), axis=2).astype(jnp.int32)


def _interleave(chains):
    chains = list(chains)
    while chains:
        for c in list(chains):
            try:
                next(c)
            except StopIteration:
                chains.remove(c)


def _fox_prompt_body(nvisit_ref, q_ref, k_ref, v_ref, ccol_ref, crow_ref, o_ref, m_scr, l_scr, acc_scr, cq_scr):
    hg = pl.program_id(0)
    qi = pl.program_id(1)
    tq = q_ref.shape[0]
    tk = tq
    m_scr[...] = jnp.full_like(m_scr, NEG_INF)
    l_scr[...] = jnp.zeros_like(l_scr)
    acc_scr[...] = jnp.zeros_like(acc_scr)
    lane = lax.broadcasted_iota(jnp.int32, (tq, LANES), 1)
    heads = [hg * HEADS_PER_STEP + sub for sub in range(HEADS_PER_STEP)]
    for sub, h in enumerate(heads):
        cq_scr[sub] = jnp.sum(jnp.where(lane == h, ccol_ref[...], 0.0), axis=1, keepdims=True)

    def tile(sub, kb, diagonal):
        h = heads[sub]
        cols = slice(sub * HEAD_DIM, (sub + 1) * HEAD_DIM)
        start = pl.multiple_of(kb * tk, ROW_CHUNK)
        k = k_ref[pl.ds(start, tk), cols]
        v = v_ref[pl.ds(start, tk), cols]
        ck = crow_ref[kb, pl.ds(h, 1), :]
        chunks = list(range(0, tq, ROW_CHUNK))

        def n_keys(r0):
            return r0 + ROW_CHUNK if diagonal else tk

        def scores(r0):
            nk = n_keys(r0)
            return _dot_nt(q_ref[r0:r0 + ROW_CHUNK, cols], k[:nk])

        def softmax(r0, s):
            rows = slice(r0, r0 + ROW_CHUNK)
            nk = n_keys(r0)
            s = s + cq_scr[sub, rows, :] - ck[:, :nk]
            if diagonal:
                r = lax.broadcasted_iota(jnp.int32, (ROW_CHUNK, nk), 0) + r0
                c = lax.broadcasted_iota(jnp.int32, (ROW_CHUNK, nk), 1)
                s = jnp.where(c <= r, s, NEG_INF)
            m_prev = m_scr[sub, rows, :]
            m_new = jnp.maximum(m_prev, jnp.max(s, axis=1, keepdims=True))
            alpha = jnp.exp(m_prev - m_new)
            p = jnp.exp(s - m_new)
            l_scr[sub, rows, :] = alpha * l_scr[sub, rows, :] + jnp.sum(p, axis=1, keepdims=True)
            m_scr[sub, rows, :] = m_new
            return alpha, p.astype(BF16)

        def values(r0, alpha, p):
            rows = slice(r0, r0 + ROW_CHUNK)
            acc_scr[sub, rows, :] = alpha * acc_scr[sub, rows, :] + _dot(p, v[:n_keys(r0)])

        s_next = scores(chunks[0])
        yield
        pending = None
        for idx, r0 in enumerate(chunks):
            s_cur = s_next
            if idx + 1 < len(chunks):
                s_next = scores(chunks[idx + 1])
                yield
            alpha, p = softmax(r0, s_cur)
            yield
            if pending is not None:
                values(*pending)
                yield
            pending = (r0, alpha, p)
        values(*pending)

    subs = range(HEADS_PER_STEP)
    _interleave(tile(sub, qi, True) for sub in subs)

    visits = [nvisit_ref[h, qi] for h in heads]
    common = functools.reduce(jnp.minimum, visits)

    def together(kk, carry):
        _interleave(tile(sub, qi - kk, False) for sub in subs)
        return carry

    lax.fori_loop(1, common, together, 0)
    for sub in subs:
        def alone(kk, carry, sub=sub):
            _interleave([tile(sub, qi - kk, False)])
            return carry

        lax.fori_loop(common, visits[sub], alone, 0)
    for sub in subs:
        o_ref[:, sub * HEAD_DIM:(sub + 1) * HEAD_DIM] = (acc_scr[sub] / l_scr[sub]).astype(BF16)


def _fox_prompt(z, ccol, crow, plan, tq):
    rows = z.shape[0]
    nq = rows // tq
    width = HEADS_PER_STEP * HEAD_DIM
    qb, kb, vb = Q_OFF // width, K_OFF // width, V_OFF // width
    nvisit = _tiles_to_visit(plan)
    return pl.pallas_call(
        _fox_prompt_body,
        grid_spec=pltpu.PrefetchScalarGridSpec(
            num_scalar_prefetch=1,
            grid=(N_HEADS // HEADS_PER_STEP, nq),
            in_specs=[
                pl.BlockSpec((tq, width), lambda hg, qi, nv: (qi, qb + hg)),
                pl.BlockSpec((rows, width), lambda hg, qi, nv: (0, kb + hg)),
                pl.BlockSpec((rows, width), lambda hg, qi, nv: (0, vb + hg)),
                pl.BlockSpec((tq, LANES), lambda hg, qi, nv: (qi, 0)),
                pl.BlockSpec((nq, N_HEADS, tq), lambda hg, qi, nv: (0, 0, 0)),
            ],
            out_specs=pl.BlockSpec((tq, width), lambda hg, qi, nv: (qi, hg)),
            scratch_shapes=[pltpu.VMEM((HEADS_PER_STEP, tq, 1), F32),
                            pltpu.VMEM((HEADS_PER_STEP, tq, 1), F32),
                            pltpu.VMEM((HEADS_PER_STEP, tq, HEAD_DIM), F32),
                            pltpu.VMEM((HEADS_PER_STEP, tq, 1), F32)]),
        out_shape=jax.ShapeDtypeStruct((rows, ATTN_DIM), BF16),
        compiler_params=_params(2),
        name="fox_prompt",
    )(nvisit, z, z, z, ccol, crow)


def _fox_sample_body(*refs):
    (q_ref, kn_ref, vn_ref, cn_ref, lc_ref, ck_ref, cv_ref, o_ref,
     qbd_scr, m_scr, l_scr, acc_scr, tail_scr) = refs
    c = pl.program_id(1)
    nc = pl.num_programs(1)
    t_new = q_ref.shape[0]
    nrow = t_new * N_HEADS
    kc = ck_ref.shape[0] // N_HEADS
    nblk = kc // SUFFIX_BLOCK
    cn = cn_ref[...]
    cq = jnp.concatenate([cn[:, t:t + 1] for t in range(t_new)], axis=0)
    col_head = lax.broadcasted_iota(jnp.int32, (N_HEADS, ATTN_DIM), 1) // HEAD_DIM
    row_head = lax.broadcasted_iota(jnp.int32, (N_HEADS, ATTN_DIM), 0)
    head_mask = col_head == row_head

    @pl.when(c == 0)
    def _():
        q = q_ref[...].astype(F32)
        q3 = jnp.where(head_mask[None], jnp.broadcast_to(q[:, None, :], (t_new, N_HEADS, ATTN_DIM)), 0.0)
        qbd_scr[...] = q3.reshape(nrow, ATTN_DIM).astype(BF16)
        m_scr[...] = jnp.full_like(m_scr, NEG_INF)
        l_scr[...] = jnp.zeros_like(l_scr)
        acc_scr[...] = jnp.zeros_like(acc_scr)
        tail_scr[...] = jnp.zeros_like(tail_scr)

    def online(s, vals):
        m_prev = m_scr[...]
        m_new = jnp.maximum(m_prev, jnp.max(s, axis=1, keepdims=True))
        alpha = jnp.exp(m_prev - m_new)
        p = jnp.exp(s - m_new)
        l_scr[...] = alpha * l_scr[...] + jnp.sum(p, axis=1, keepdims=True)
        acc_scr[...] = alpha * acc_scr[...] + _dot(p.astype(BF16), vals)
        m_scr[...] = m_new

    lc = lc_ref[...]
    x = jnp.concatenate([lc[:, b * SUFFIX_BLOCK:(b + 1) * SUFFIX_BLOCK] for b in range(nblk)], axis=0)
    jj = lax.broadcasted_iota(jnp.int32, (SUFFIX_BLOCK, SUFFIX_BLOCK), 0)
    ss = lax.broadcasted_iota(jnp.int32, (SUFFIX_BLOCK, SUFFIX_BLOCK), 1)
    excl = _dot_exact01(jnp.where(jj > ss, 1.0, 0.0).astype(BF16), x, nt=True)
    off = tail_scr[...]
    pieces = [None] * nblk
    for b in range(nblk - 1, -1, -1):
        eb = excl[b * N_HEADS:(b + 1) * N_HEADS, :]
        pieces[b] = eb + off
        off = off + (eb[:, 0:1] + x[b * N_HEADS:(b + 1) * N_HEADS, 0:1])
    tail_scr[...] = off
    suffix = jnp.concatenate(pieces, axis=1)

    def all_heads(ref):
        return jnp.concatenate([ref[pl.ds(h, kc, stride=N_HEADS), :].astype(BF16)
                                for h in range(N_HEADS)], axis=1)

    s = _dot_nt(qbd_scr[...], all_heads(ck_ref))
    s = (s.reshape(t_new, N_HEADS, kc) + cq.reshape(t_new, N_HEADS, 1) + suffix[None]).reshape(nrow, kc)
    online(s, all_heads(cv_ref))

    @pl.when(c == nc - 1)
    def _():
        sn = _dot_nt(qbd_scr[...], kn_ref[...])
        sn = sn.reshape(t_new, N_HEADS, t_new) + cq.reshape(t_new, N_HEADS, 1) + (-cn)[None]
        tq = lax.broadcasted_iota(jnp.int32, (t_new, N_HEADS, t_new), 0)
        tk = lax.broadcasted_iota(jnp.int32, (t_new, N_HEADS, t_new), 2)
        sn = jnp.where(tk <= tq, sn, NEG_INF).reshape(nrow, t_new)
        online(sn, vn_ref[...])
        o = (acc_scr[...] / l_scr[...]).reshape(t_new, N_HEADS, ATTN_DIM)
        o_ref[...] = jnp.sum(jnp.where(head_mask[None], o, 0.0), axis=1).astype(BF16)


def _fox_sample(z, cn_t, cache_k, cache_v, lc_t, layer, t_new):
    nb = cache_k.shape[1]
    past = cache_k.shape[2] // N_HEADS
    nc = past // KV_CHUNK
    nrow = t_new * N_HEADS
    qb, kb, vb = Q_OFF // ATTN_DIM, K_OFF // ATTN_DIM, V_OFF // ATTN_DIM
    cache_spec = pl.BlockSpec((None, None, KV_CHUNK * N_HEADS, HEAD_DIM),
                              lambda b, c: (layer, b, nc - 1 - c, 0))
    return pl.pallas_call(
        _fox_sample_body,
        grid=(nb, nc),
        in_specs=[
            pl.BlockSpec((t_new, ATTN_DIM), lambda b, c: (b, qb)),
            pl.BlockSpec((t_new, ATTN_DIM), lambda b, c: (b, kb)),
            pl.BlockSpec((t_new, ATTN_DIM), lambda b, c: (b, vb)),
            pl.BlockSpec((None, N_HEADS, t_new), lambda b, c: (b, 0, 0)),
            pl.BlockSpec((None, None, N_HEADS, KV_CHUNK), lambda b, c: (layer, b, 0, nc - 1 - c)),
            cache_spec, cache_spec,
        ],
        out_specs=pl.BlockSpec((t_new, ATTN_DIM), lambda b, c: (b, 0)),
        out_shape=jax.ShapeDtypeStruct((nb * t_new, ATTN_DIM), BF16),
        scratch_shapes=[pltpu.VMEM((nrow, ATTN_DIM), BF16), pltpu.VMEM((nrow, 1), F32),
                        pltpu.VMEM((nrow, 1), F32), pltpu.VMEM((nrow, ATTN_DIM), F32),
                        pltpu.VMEM((N_HEADS, 1), F32)],
        compiler_params=_params(2),
        name="fox_sample",
    )(z, z, z, cn_t, lc_t, cache_k, cache_v)


def _mix_body(*refs, seg, has_hist):
    zc_ref, ya_ref, gc_ref, ga_ref, wc_ref, wpc_ref, wpa_ref = refs[:7]
    n_in = 8 if has_hist else 7
    mg_ref, st_ref = refs[n_in:n_in + 2]
    i = pl.program_id(0)
    tm = zc_ref.shape[0]
    xc = zc_ref[:, 0:CONV_DIM].astype(F32)
    bc = zc_ref[:, CONV_DIM:2 * CONV_DIM].astype(F32)
    cc = zc_ref[:, 2 * CONV_DIM:3 * CONV_DIM].astype(F32)
    u = cc * xc
    if has_hist:
        hist = refs[7][...]
        h0 = _seg_rows(hist[:, 0, :], seg)
        h1 = _seg_rows(hist[:, 1, :], seg)
        nseg = tm // seg
        st_ref[...] = u.reshape(nseg, seg, CONV_DIM)[:, seg - 2:seg, :]
    else:
        carry_scr = refs[n_in + 2]

        @pl.when(i == 0)
        def _():
            carry_scr[...] = jnp.zeros_like(carry_scr)
        h0 = carry_scr[0:1, :]
        h1 = carry_scr[1:2, :]
        st_ref[...] = u[tm - 2:tm, :][None]
    yc = (bc * _conv3(u, h0, h1, wc_ref[...], seg)).astype(BF16)
    if not has_hist:
        carry_scr[0:2, :] = u[tm - 2:tm, :]
    t1 = _dot(yc, wpc_ref[...])
    t2 = _dot(ya_ref[...], wpa_ref[...])
    mg_ref[...] = (gc_ref[...].astype(F32) * t1 + ga_ref[...].astype(F32) * t2).astype(BF16)


def _mix(z, ya, w_conv, w_pc, w_pa, layer, tm, seg, hist):
    rows = z.shape[0]
    nm = rows // tm
    has_hist = hist is not None
    nseg = tm // seg if has_hist else 1
    in_specs = [
        pl.BlockSpec((tm, 3 * CONV_DIM), lambda i: (i, 0)),
        pl.BlockSpec((tm, ATTN_DIM), lambda i: (i, 0)),
        pl.BlockSpec((tm, D_MODEL), lambda i: (i, G_OFF // D_MODEL)),
        pl.BlockSpec((tm, D_MODEL), lambda i: (i, G_OFF // D_MODEL + 1)),
        pl.BlockSpec((None, 3, CONV_DIM), lambda i: (layer, 0, 0)),
        pl.BlockSpec((None, CONV_DIM, D_MODEL), lambda i: (layer, 0, 0)),
        pl.BlockSpec((None, ATTN_DIM, D_MODEL), lambda i: (layer, 0, 0)),
    ]
    args = [z, ya, z, z, w_conv, w_pc, w_pa]
    scratch = []
    if has_hist:
        in_specs.append(pl.BlockSpec((nseg, 2, CONV_DIM), lambda i: (0, 0, 0)))
        args.append(hist)
    else:
        scratch.append(pltpu.VMEM((8, CONV_DIM), F32))
    return pl.pallas_call(
        functools.partial(_mix_body, seg=seg, has_hist=has_hist),
        grid=(nm,),
        in_specs=in_specs,
        out_specs=(pl.BlockSpec((tm, D_MODEL), lambda i: (i, 0)),
                   pl.BlockSpec((nseg, 2, CONV_DIM), lambda i: (0, 0, 0))),
        out_shape=(jax.ShapeDtypeStruct((rows, D_MODEL), BF16),
                   jax.ShapeDtypeStruct((nseg, 2, CONV_DIM), F32)),
        scratch_shapes=scratch,
        compiler_params=_params(1),
        name="mix",
    )(*args)


def _oproj_body(x_ref, mg_ref, wo_ref, *rest):
    o_ref = rest[-1]
    x = _rows_after_meta(x_ref, rest[0], pl.program_id(0)) if len(rest) == 2 else x_ref[...]
    o_ref[...] = x + _dot(mg_ref[...], wo_ref[...])


def _oproj(x, meta, mg, w_o, layer, tm):
    has_meta = meta is not None
    rows = mg.shape[0]
    in_specs = [_meta_row_spec(tm) if has_meta else pl.BlockSpec((tm, D_MODEL), lambda i: (i, 0)),
                pl.BlockSpec((tm, D_MODEL), lambda i: (i, 0)),
                pl.BlockSpec((None, D_MODEL, D_MODEL), lambda i: (layer, 0, 0))]
    args = [x, mg, w_o]
    if has_meta:
        in_specs.append(pl.BlockSpec((N_META, D_MODEL), lambda i: (0, 0)))
        args.append(meta)
    return pl.pallas_call(
        _oproj_body,
        grid=(rows // tm,),
        in_specs=in_specs,
        out_specs=pl.BlockSpec((tm, D_MODEL), lambda i: (i, 0)),
        out_shape=jax.ShapeDtypeStruct((rows, D_MODEL), F32),
        compiler_params=_params(1),
        name="oproj",
    )(*args)


def _ffn_body(*refs, seg, has_hist):
    (x_ref, g_ref, wua_ref, wub_ref, wca_ref, wcb_ref, ba_ref, bb_ref, wd_ref) = refs[:9]
    n_in = 11 if has_hist else 9
    o_ref, st_ref, h_scr = refs[n_in:n_in + 3]
    i = pl.program_id(0)
    j = pl.program_id(1)
    tm = x_ref.shape[0]
    tf = wua_ref.shape[1]

    @pl.when(j == 0)
    def _():
        x = x_ref[...]
        h_scr[...] = _rmsnorm(x, g_ref[...]).astype(BF16)
        o_ref[...] = x

    rc = ROW_CHUNK if (not has_hist and tm % ROW_CHUNK == 0) else tm
    chunks = list(range(0, tm, rc))

    def up(r0):
        hb = h_scr[r0:r0 + rc, :]
        return _dot(hb, wua_ref[...]), _dot(hb, wub_ref[...])

    def gate(u, hist, w_ref, bias_ref):
        return _conv3(u, hist[0], hist[1], w_ref[...], min(seg, rc)) + bias_ref[...]

    def down(r0, gated):
        o_ref[r0:r0 + rc, :] += _dot(gated, wd_ref[...])

    if has_hist:
        nseg = tm // seg
        hist_a, hist_b = ((_seg_rows(r[:, 0, :], seg), _seg_rows(r[:, 1, :], seg))
                          for r in (refs[9][...], refs[10][...]))
    else:
        ca_scr, cb_scr = refs[n_in + 3:n_in + 5]

        @pl.when(i == 0)
        def _():
            ca_scr[j] = jnp.zeros((8, tf), F32)
            cb_scr[j] = jnp.zeros((8, tf), F32)
        hist_a = (ca_scr[j, 0:1, :], ca_scr[j, 1:2, :])
        hist_b = (cb_scr[j, 0:1, :], cb_scr[j, 1:2, :])

    u_next = up(chunks[0])
    pending = None
    for idx, r0 in enumerate(chunks):
        ua, ub = u_next
        if idx + 1 < len(chunks):
            u_next = up(chunks[idx + 1])
        a = gate(ua, hist_a, wca_ref, ba_ref)
        b = gate(ub, hist_b, wcb_ref, bb_ref)
        gated = ((a * _sigmoid(a)) * b).astype(BF16)
        hist_a = (ua[rc - 2:rc - 1, :], ua[rc - 1:rc, :])
        hist_b = (ub[rc - 2:rc - 1, :], ub[rc - 1:rc, :])
        if pending is not None:
            down(*pending)
        pending = (r0, gated)
    down(*pending)

    if has_hist:
        st_ref[:, :, 0, :] = ua.reshape(nseg, seg, tf)[:, seg - 2:seg, :]
        st_ref[:, :, 1, :] = ub.reshape(nseg, seg, tf)[:, seg - 2:seg, :]
    else:
        st_ref[0, :, 0, :] = ua[rc - 2:rc, :]
        st_ref[0, :, 1, :] = ub[rc - 2:rc, :]
        ca_scr[j, 0:2, :] = ua[rc - 2:rc, :]
        cb_scr[j, 0:2, :] = ub[rc - 2:rc, :]


def _ffn(x, g, w_up, w_fconv, b_fconv, w_down, layer, tm, seg, hist):
    rows = x.shape[0]
    nm = rows // tm
    nf = D_FF // TF
    has_hist = hist is not None
    nseg = tm // seg if has_hist else 1
    in_specs = [
        pl.BlockSpec((tm, D_MODEL), lambda i, j: (i, 0)),
        pl.BlockSpec((None, 1, D_MODEL), lambda i, j: (layer, 0, 0)),
        pl.BlockSpec((None, D_MODEL, TF), lambda i, j: (layer, 0, j)),
        pl.BlockSpec((None, D_MODEL, TF), lambda i, j: (layer, 0, nf + j)),
        pl.BlockSpec((None, 3, TF), lambda i, j: (layer, 0, j)),
        pl.BlockSpec((None, 3, TF), lambda i, j: (layer, 0, nf + j)),
        pl.BlockSpec((None, 1, TF), lambda i, j: (layer, 0, j)),
        pl.BlockSpec((None, 1, TF), lambda i, j: (layer, 0, nf + j)),
        pl.BlockSpec((None, TF, D_MODEL), lambda i, j: (layer, j, 0)),
    ]
    args = [x, g, w_up, w_up, w_fconv, w_fconv, b_fconv, b_fconv, w_down]
    scratch = [pltpu.VMEM((tm, D_MODEL), BF16)]
    if has_hist:
        in_specs += [pl.BlockSpec((nseg, 2, TF), lambda i, j: (0, 0, j)),
                     pl.BlockSpec((nseg, 2, TF), lambda i, j: (0, 0, nf + j))]
        args += [hist, hist]
    else:
        scratch += [pltpu.VMEM((nf, 8, TF), F32), pltpu.VMEM((nf, 8, TF), F32)]
    return pl.pallas_call(
        functools.partial(_ffn_body, seg=seg, has_hist=has_hist),
        grid=(nm, nf),
        in_specs=in_specs,
        out_specs=(pl.BlockSpec((tm, D_MODEL), lambda i, j: (i, 0)),
                   pl.BlockSpec((None, nseg, 2, 2, TF), lambda i, j: (i, 0, 0, 0, j))),
        out_shape=(jax.ShapeDtypeStruct((rows, D_MODEL), F32),
                   jax.ShapeDtypeStruct((nm, nseg, 2, 2, D_FF), F32)),
        scratch_shapes=scratch,
        compiler_params=_params(2),
        name="ffn",
    )(*args)


def _final_norm_body(x_ref, g_ref, o_ref):
    o_ref[...] = _rmsnorm(x_ref[...], g_ref[...])


def _final_norm(x, g, tm, skip_rows):
    rows = x.shape[0] - skip_rows
    return pl.pallas_call(
        _final_norm_body,
        grid=(rows // tm,),
        in_specs=[pl.BlockSpec((pl.Element(tm), pl.Element(D_MODEL)),
                               lambda i: (pl.multiple_of(i * tm + skip_rows, 8), 0)),
                  pl.BlockSpec((1, D_MODEL), lambda i: (0, 0))],
        out_specs=pl.BlockSpec((tm, D_MODEL), lambda i: (i, 0)),
        out_shape=jax.ShapeDtypeStruct((rows, D_MODEL), F32),
        compiler_params=_params(1),
        name="final_norm",
    )(x, g)


def kernel(x_prompt, x_sample, cache_k, cache_v, cache_logf, state_conv, state_ffn_conv, meta_tokens,
           g_mix, w_in, b_in, w_conv, w_pc, w_pa, w_o, g_ffn, w_up, w_fconv, b_fconv, w_down, g_final):
    depth = w_in.shape[0]
    batch, seq, _ = x_prompt.shape
    nb, t_new, _ = x_sample.shape
    past = cache_k.shape[2]
    lp = N_META + seq
    assert batch == 1 and past % KV_CHUNK == 0 and lp % TM_BIG == 0 and lp % TM_SMALL == 0
    rows_s = nb * t_new

    w_left = jnp.swapaxes(w_in, 1, 2).astype(BF16)
    w_gate = w_left[:, F_OFF + N_HEADS:]
    w_f = jnp.pad(w_left[:, F_OFF:F_OFF + N_HEADS], ((0, 0), (0, LANES - N_HEADS), (0, 0)))
    b_main = jnp.concatenate([b_in[:, :F_OFF], b_in[:, F_OFF + N_HEADS:]], axis=1)[:, None, :]
    b_f = jnp.pad(b_in[:, F_OFF:F_OFF + N_HEADS], ((0, 0), (0, LANES - N_HEADS)))[:, None, :]
    w_pc_b, w_pa_b, w_o_b = w_pc.astype(BF16), w_pa.astype(BF16), w_o.astype(BF16)
    w_up_b, w_down_b = w_up.astype(BF16), w_down.astype(BF16)
    g_mix3, g_ffn3, b_fconv3 = g_mix[:, None, :], g_ffn[:, None, :], b_fconv[:, None, :]

    xp, meta = x_prompt[0], meta_tokens.astype(F32)
    xs = x_sample.reshape(rows_s, D_MODEL)
    ck = cache_k.reshape(depth, nb, past * N_HEADS, HEAD_DIM)
    cv = cache_v.reshape(depth, nb, past * N_HEADS, HEAD_DIM)
    lc_t = jnp.swapaxes(cache_logf.astype(F32), 2, 3)

    kv_p = kv_s = None
    lf_p, lf_s, cst_p, cst_s, fst_p, fst_s = [], [], [], [], [], []
    for l in range(depth):
        z, k_all, v_all, lf, ccol, crow, plan = _inproj(xp, meta, g_mix3, w_left, w_gate, b_main, w_f, b_f,
                                                         l, lp, TM_BIG, TM_BIG, True, kv_p)
        kv_p = (k_all, v_all)
        ya = _fox_prompt(z, ccol, crow, plan, TM_BIG)
        mg, cst = _mix(z, ya, w_conv, w_pc_b, w_pa_b, l, TM_SMALL, TM_SMALL, None)
        x1 = _oproj(xp, meta, mg, w_o_b, l, TM_SMALL)
        xp, fst = _ffn(x1, g_ffn3, w_up_b, w_fconv, b_fconv3, w_down_b, l, TM_BIG, TM_BIG, None)
        meta = None
        lf_p.append(lf); cst_p.append(cst); fst_p.append(fst[-1].reshape(1, 2, 2 * D_FF))
        z, k_all, v_all, lf, _, crow, _ = _inproj(xs, None, g_mix3, w_left, w_gate, b_main, w_f, b_f, l,
                                                  rows_s, rows_s, t_new, False, kv_s)
        kv_s = (k_all, v_all)
        cn_t = jnp.swapaxes(crow[0].reshape(N_HEADS, nb, t_new), 0, 1)
        ya = _fox_sample(z, cn_t, ck, cv, lc_t, l, t_new)
        mg, cst = _mix(z, ya, w_conv, w_pc_b, w_pa_b, l, rows_s, t_new, state_conv[l])
        x1 = _oproj(xs, None, mg, w_o_b, l, rows_s)
        xs, fst = _ffn(x1, g_ffn3, w_up_b, w_fconv, b_fconv3, w_down_b, l, rows_s, t_new,
                       state_ffn_conv[l])
        lf_s.append(lf); cst_s.append(cst); fst_s.append(fst[-1].reshape(nb, 2, 2 * D_FF))

    g_fin = g_final[None, :]
    y_prompt = _final_norm(xp, g_fin, TM_OUT, N_META)[None]
    y_sample = _final_norm(xs, g_fin, rows_s, 0).reshape(nb, t_new, D_MODEL)
    return (y_prompt, y_sample,
            kv_p[0].reshape(depth, 1, lp, N_HEADS, HEAD_DIM),
            kv_p[1].reshape(depth, 1, lp, N_HEADS, HEAD_DIM),
            jnp.stack(lf_p).reshape(depth, 1, lp, N_HEADS),
            jnp.stack(cst_p),
            jnp.stack(fst_p),
            kv_s[0].reshape(depth, nb, t_new, N_HEADS, HEAD_DIM),
            kv_s[1].reshape(depth, nb, t_new, N_HEADS, HEAD_DIM),
            jnp.stack(lf_s).reshape(depth, nb, t_new, N_HEADS),
            jnp.stack(cst_s),
            jnp.stack(fst_s))
```

```python
import functools

import jax
import jax.numpy as jnp
from jax import lax
from jax.experimental import pallas as pl
from jax.experimental.pallas import tpu as pltpu

F32 = jnp.float32
BF16 = jnp.bfloat16

D_MODEL = 2048
N_META = 16
CONV_DIM = D_MODEL // 2
HEAD_DIM = 128
ATTN_DIM = D_MODEL // 2
N_HEADS = ATTN_DIM // HEAD_DIM
D_FF = 2 * D_MODEL
EPS = 1e-6
ATTN_SCALE = HEAD_DIM ** -0.5
NEG_INF = -1e30
EXP_ZERO_BELOW = 106.0
F_OFF = 3 * CONV_DIM + 3 * ATTN_DIM
MAIN_DIM = F_OFF + 2 * D_MODEL

LANES = 128
VMEM_LIMIT = 60000 * 1024

Q_OFF = 3 * CONV_DIM
K_OFF = Q_OFF + ATTN_DIM
V_OFF = K_OFF + ATTN_DIM
G_OFF = V_OFF + ATTN_DIM

TN_IN = 1024
TF = 512
TM_BIG = 912
TM_SMALL = 432
TM_OUT = 512
ROW_CHUNK = 304
HEADS_PER_STEP = 2
KV_CHUNK = 2048
SUFFIX_BLOCK = 256


def _params(n_axes):
    return pltpu.CompilerParams(dimension_semantics=("arbitrary",) * n_axes,
                                vmem_limit_bytes=VMEM_LIMIT)


def _dot(a, b):
    return jnp.dot(a, b, preferred_element_type=F32)


def _dot_nt(a, b):
    return lax.dot_general(a, b, (((1,), (1,)), ((), ())), preferred_element_type=F32)


def _dot_exact01(mask_bf16, x, nt=False):
    hi = x.astype(BF16)
    r1 = x - hi.astype(F32)
    mid = r1.astype(BF16)
    lo = (r1 - mid.astype(F32)).astype(BF16)
    if nt:
        return _dot(hi, mask_bf16) + _dot(mid, mask_bf16) + _dot(lo, mask_bf16)
    return _dot(mask_bf16, hi) + _dot(mask_bf16, mid) + _dot(mask_bf16, lo)


def _rmsnorm(x, g):
    ms = jnp.mean(x * x, axis=-1, keepdims=True)
    return (x * lax.rsqrt(ms + EPS)) * g


def _sigmoid(x):
    return 1.0 / (1.0 + jnp.exp(-x))


def _conv3(u, h0rows, h1rows, w, seg):
    tm = u.shape[0]
    r = lax.broadcasted_iota(jnp.int32, (tm, 1), 0)
    if seg < tm:
        r = r % seg
    um1 = jnp.where(r == 0, h1rows, pltpu.roll(u, 1, 0))
    um2 = jnp.where(r == 0, h0rows, jnp.where(r == 1, h1rows, pltpu.roll(u, 2, 0)))
    return um2 * w[0:1] + um1 * w[1:2] + u * w[2:3]


def _seg_rows(h, seg):
    nseg, c = h.shape
    return jnp.broadcast_to(h[:, None, :], (nseg, seg, c)).reshape(nseg * seg, c)


def _rows_after_meta(x_ref, meta_ref, i):
    x = x_ref[...]
    shifted = jnp.concatenate([meta_ref[...], x[:x.shape[0] - N_META]], axis=0)
    return jnp.where(i == 0, shifted, x)


def _meta_row_spec(tm):
    return pl.BlockSpec((pl.Element(tm), pl.Element(D_MODEL)),
                        lambda i, *_: (pl.multiple_of(jnp.maximum(i * tm - N_META, 0), 8), 0))


def _max_row_norms(x):
    cols = []
    for h in range(N_HEADS):
        xh = x[:, h * HEAD_DIM:(h + 1) * HEAD_DIM].astype(F32)
        cols.append(jnp.sqrt(jnp.max(jnp.sum(xh * xh, axis=1, keepdims=True), axis=0, keepdims=True)))
    return jnp.concatenate(cols, axis=0)


def _inproj_body(*refs, seg, use_carry, has_meta):
    x_ref, g_ref, wl_ref, wg_ref, b_ref, wf_ref, bf_ref = refs[:7]
    (z_ref, k_ref, v_ref, lf_ref, ccol_ref, crow_ref, plan_ref,
     h_scr, carry_scr, tri_scr) = refs[7 + has_meta:]
    i = pl.program_id(0)
    j = pl.program_id(1)
    tm = x_ref.shape[0]
    jq, jk, jv, jg = (Q_OFF // TN_IN, K_OFF // TN_IN, V_OFF // TN_IN, G_OFF // TN_IN)
    plan_lane = lax.broadcasted_iota(jnp.int32, (N_HEADS, LANES), 1)

    @pl.when(j == 0)
    def _():
        x = _rows_after_meta(x_ref, refs[7], i) if has_meta else x_ref[...]
        hb = _rmsnorm(x, g_ref[...]).astype(BF16)
        h_scr[...] = hb
        fl = _dot_nt(hb, wf_ref[...]) + bf_ref[...]
        lf = jnp.minimum(fl, 0.0) - jnp.log1p(jnp.exp(-jnp.abs(fl)))
        lf_ref[...] = lf[:, :N_HEADS]

        @pl.when(i == 0)
        def _():
            r = lax.broadcasted_iota(jnp.int32, (tm, tm), 0)
            c = lax.broadcasted_iota(jnp.int32, (tm, tm), 1)
            tri = c <= r
            if seg < tm:
                tri = tri & ((r // seg) == (c // seg))
            tri_scr[...] = jnp.where(tri, 1.0, 0.0).astype(BF16)

        lane = lax.broadcasted_iota(jnp.int32, (tm, LANES), 1)
        hi = jnp.where(lane < N_HEADS, lf, 0.0)
        hi_b = hi.astype(BF16).astype(F32)
        mid = hi - hi_b
        mid_b = mid.astype(BF16).astype(F32)
        lo_b = (mid - mid_b).astype(BF16).astype(F32)
        packed = hi_b + pltpu.roll(mid_b, N_HEADS, 1) + pltpu.roll(lo_b, 2 * N_HEADS, 1)
        part = _dot(tri_scr[...], packed.astype(BF16))
        cum = part + pltpu.roll(part, LANES - N_HEADS, 1) + pltpu.roll(part, LANES - 2 * N_HEADS, 1)
        if use_carry:
            @pl.when(i == 0)
            def _():
                carry_scr[...] = jnp.zeros_like(carry_scr)
            cum = cum + carry_scr[...]
            carry_scr[...] = cum[tm - 1:tm, :]
        ccol_ref[...] = cum
        crow = cum.T[:N_HEADS, :]
        crow_ref[...] = crow
        plan_ref[...] = jnp.where(plan_lane == 2, crow[:, 0:1],
                                  jnp.where(plan_lane == 3, crow[:, tm - 1:tm], 0.0))

    @pl.when(j < jg)
    def _():
        acc = _dot_nt(h_scr[...], wl_ref[...]) + b_ref[...]
        is_q = (j >= jq) & (j < jk)
        zb = jnp.where(is_q, acc * ATTN_SCALE, acc).astype(BF16)
        z_ref[...] = zb

        @pl.when(is_q)
        def _():
            plan_ref[...] = jnp.where(plan_lane == 0, _max_row_norms(zb), plan_ref[...])

        @pl.when((j >= jk) & (j < jv))
        def _():
            k_ref[...] = acc
            plan_ref[...] = jnp.where(plan_lane == 1, _max_row_norms(zb), plan_ref[...])

        @pl.when((j >= jv) & (j < jg))
        def _():
            v_ref[...] = acc

    @pl.when(j >= jg)
    def _():
        z_ref[...] = _sigmoid(_dot_nt(h_scr[...], wg_ref[...]) + b_ref[...]).astype(BF16)


def _inproj(x, meta, g, w_left, w_gate, b_main, w_f, b_f, layer, n_valid, tm, seg, use_carry):
    has_meta = meta is not None
    rows = x.shape[0] + (N_META if has_meta else 0)
    nm = rows // tm
    nj = MAIN_DIM // TN_IN
    jk, jv, jg = K_OFF // TN_IN, V_OFF // TN_IN, G_OFF // TN_IN
    assert TN_IN == ATTN_DIM
    nkv = ATTN_DIM // TN_IN
    in_specs = [
        _meta_row_spec(tm) if has_meta else pl.BlockSpec((tm, D_MODEL), lambda i, j: (i, 0)),
        pl.BlockSpec((None, 1, D_MODEL), lambda i, j: (layer, 0, 0)),
        pl.BlockSpec((None, TN_IN, D_MODEL), lambda i, j: (layer, jnp.minimum(j, jg - 1), 0)),
        pl.BlockSpec((None, TN_IN, D_MODEL), lambda i, j: (layer, jnp.maximum(j - jg, 0), 0)),
        pl.BlockSpec((None, 1, TN_IN), lambda i, j: (layer, 0, j)),
        pl.BlockSpec((None, LANES, D_MODEL), lambda i, j: (layer, 0, 0)),
        pl.BlockSpec((None, 1, LANES), lambda i, j: (layer, 0, 0)),
    ]
    args = [x, g, w_left, w_gate, b_main, w_f, b_f]
    if has_meta:
        in_specs.append(pl.BlockSpec((N_META, D_MODEL), lambda i, j: (0, 0)))
        args.append(meta)
    out_shape = (
        jax.ShapeDtypeStruct((rows, MAIN_DIM), BF16),
        jax.ShapeDtypeStruct((n_valid, ATTN_DIM), F32),
        jax.ShapeDtypeStruct((n_valid, ATTN_DIM), F32),
        jax.ShapeDtypeStruct((n_valid, N_HEADS), F32),
        jax.ShapeDtypeStruct((rows, LANES), F32),
        jax.ShapeDtypeStruct((nm, N_HEADS, tm), F32),
        jax.ShapeDtypeStruct((nm, N_HEADS, LANES), F32),
    )
    out_specs = (
        pl.BlockSpec((tm, TN_IN), lambda i, j: (i, j)),
        pl.BlockSpec((tm, TN_IN), lambda i, j: (i, jnp.clip(j - jk, 0, nkv - 1)),
                     pipeline_mode=pl.Buffered(1)),
        pl.BlockSpec((tm, TN_IN), lambda i, j: (i, jnp.clip(j - jv, 0, nkv - 1)),
                     pipeline_mode=pl.Buffered(1)),
        pl.BlockSpec((tm, N_HEADS), lambda i, j: (i, 0)),
        pl.BlockSpec((tm, LANES), lambda i, j: (i, 0)),
        pl.BlockSpec((None, N_HEADS, tm), lambda i, j: (i, 0, 0)),
        pl.BlockSpec((None, N_HEADS, LANES), lambda i, j: (i, 0, 0)),
    )
    return pl.pallas_call(
        functools.partial(_inproj_body, seg=seg, use_carry=use_carry, has_meta=has_meta),
        grid=(nm, nj),
        in_specs=in_specs,
        out_specs=out_specs,
        out_shape=out_shape,
        scratch_shapes=[pltpu.VMEM((tm, D_MODEL), BF16), pltpu.VMEM((1, LANES), F32),
                        pltpu.VMEM((tm, tm), BF16)],
        compiler_params=_params(2),
        name="inproj",
    )(*args)


def _tiles_to_visit(plan):
    qn, kn, c_first, c_last = (plan[:, :, i].T for i in range(4))
    nq = qn.shape[1]
    gap = (qn[:, :, None] * (kn[:, None, :] + kn[:, :, None])
           + c_first[:, :, None] - c_last[:, None, :])
    qi = jnp.arange(nq)[:, None]
    kb = jnp.arange(nq)[None, :]
    needed = (kb < qi)[None] & (gap >= -EXP_ZERO_BELOW)
    return 1 + jnp.max(jnp.where(needed, (qi - kb)[None], 0), axis=2).astype(jnp.int32)


def _interleave(chains):
    chains = list(chains)
    while chains:
        for c in list(chains):
            try:
                next(c)
            except StopIteration:
                chains.remove(c)


def _fox_prompt_body(nvisit_ref, q_ref, k_ref, v_ref, ccol_ref, crow_ref, *rest, n_stack):
    layer_kv = rest[:2 * n_stack]
    o_ref = rest[2 * n_stack]
    stacked = rest[2 * n_stack + 1:len(rest) - 4]
    m_scr, l_scr, acc_scr, cq_scr = rest[len(rest) - 4:]
    for which, out_ref in enumerate(stacked):
        for layer in range(n_stack):
            out_ref[layer] = layer_kv[2 * layer + which][...]
    hg = pl.program_id(0)
    qi = pl.program_id(1)
    tq = q_ref.shape[0]
    tk = tq
    m_scr[...] = jnp.full_like(m_scr, NEG_INF)
    l_scr[...] = jnp.zeros_like(l_scr)
    acc_scr[...] = jnp.zeros_like(acc_scr)
    lane = lax.broadcasted_iota(jnp.int32, (tq, LANES), 1)
    heads = [hg * HEADS_PER_STEP + sub for sub in range(HEADS_PER_STEP)]
    for sub, h in enumerate(heads):
        cq_scr[sub] = jnp.sum(jnp.where(lane == h, ccol_ref[...], 0.0), axis=1, keepdims=True)

    def tile(sub, kb, diagonal):
        h = heads[sub]
        cols = slice(sub * HEAD_DIM, (sub + 1) * HEAD_DIM)
        start = pl.multiple_of(kb * tk, ROW_CHUNK)
        k = k_ref[pl.ds(start, tk), cols]
        v = v_ref[pl.ds(start, tk), cols]
        ck = crow_ref[kb, pl.ds(h, 1), :]
        chunks = list(range(0, tq, ROW_CHUNK))

        def n_keys(r0):
            return r0 + ROW_CHUNK if diagonal else tk

        def scores(r0):
            nk = n_keys(r0)
            return _dot_nt(q_ref[r0:r0 + ROW_CHUNK, cols], k[:nk])

        def softmax(r0, s):
            rows = slice(r0, r0 + ROW_CHUNK)
            nk = n_keys(r0)
            s = s + cq_scr[sub, rows, :] - ck[:, :nk]
            if diagonal:
                r = lax.broadcasted_iota(jnp.int32, (ROW_CHUNK, nk), 0) + r0
                c = lax.broadcasted_iota(jnp.int32, (ROW_CHUNK, nk), 1)
                s = jnp.where(c <= r, s, NEG_INF)
            m_prev = m_scr[sub, rows, :]
            m_new = jnp.maximum(m_prev, jnp.max(s, axis=1, keepdims=True))
            alpha = jnp.exp(m_prev - m_new)
            p = jnp.exp(s - m_new)
            l_scr[sub, rows, :] = alpha * l_scr[sub, rows, :] + jnp.sum(p, axis=1, keepdims=True)
            m_scr[sub, rows, :] = m_new
            return alpha, p.astype(BF16)

        def values(r0, alpha, p):
            rows = slice(r0, r0 + ROW_CHUNK)
            acc_scr[sub, rows, :] = alpha * acc_scr[sub, rows, :] + _dot(p, v[:n_keys(r0)])

        s_next = scores(chunks[0])
        yield
        pending = None
        for idx, r0 in enumerate(chunks):
            s_cur = s_next
            if idx + 1 < len(chunks):
                s_next = scores(chunks[idx + 1])
                yield
            alpha, p = softmax(r0, s_cur)
            yield
            if pending is not None:
                values(*pending)
                yield
            pending = (r0, alpha, p)
        values(*pending)

    subs = range(HEADS_PER_STEP)
    _interleave(tile(sub, qi, True) for sub in subs)

    visits = [nvisit_ref[h, qi] for h in heads]
    common = functools.reduce(jnp.minimum, visits)

    def together(kk, carry):
        _interleave(tile(sub, qi - kk, False) for sub in subs)
        return carry

    lax.fori_loop(1, common, together, 0)
    for sub in subs:
        def alone(kk, carry, sub=sub):
            _interleave([tile(sub, qi - kk, False)])
            return carry

        lax.fori_loop(common, visits[sub], alone, 0)
    for sub in subs:
        o_ref[:, sub * HEAD_DIM:(sub + 1) * HEAD_DIM] = (acc_scr[sub] / l_scr[sub]).astype(BF16)


def _fox_prompt(z, ccol, crow, plan, tq, layer_kv=()):
    rows = z.shape[0]
    nq = rows // tq
    width = HEADS_PER_STEP * HEAD_DIM
    qb, kb, vb = Q_OFF // width, K_OFF // width, V_OFF // width
    nvisit = _tiles_to_visit(plan)
    n_stack = len(layer_kv) // 2
    tile_spec = pl.BlockSpec((tq, width), lambda hg, qi, nv: (qi, hg))
    stack_spec = pl.BlockSpec((n_stack, tq, width), lambda hg, qi, nv: (0, qi, hg))
    stack_shape = jax.ShapeDtypeStruct((n_stack, rows, ATTN_DIM), F32)
    out = pl.pallas_call(
        functools.partial(_fox_prompt_body, n_stack=n_stack),
        grid_spec=pltpu.PrefetchScalarGridSpec(
            num_scalar_prefetch=1,
            grid=(N_HEADS // HEADS_PER_STEP, nq),
            in_specs=[
                pl.BlockSpec((tq, width), lambda hg, qi, nv: (qi, qb + hg)),
                pl.BlockSpec((rows, width), lambda hg, qi, nv: (0, kb + hg)),
                pl.BlockSpec((rows, width), lambda hg, qi, nv: (0, vb + hg)),
                pl.BlockSpec((tq, LANES), lambda hg, qi, nv: (qi, 0)),
                pl.BlockSpec((nq, N_HEADS, tq), lambda hg, qi, nv: (0, 0, 0)),
            ] + [tile_spec] * len(layer_kv),
            out_specs=(tile_spec,) + ((stack_spec, stack_spec) if n_stack else ()),
            scratch_shapes=[pltpu.VMEM((HEADS_PER_STEP, tq, 1), F32),
                            pltpu.VMEM((HEADS_PER_STEP, tq, 1), F32),
                            pltpu.VMEM((HEADS_PER_STEP, tq, HEAD_DIM), F32),
                            pltpu.VMEM((HEADS_PER_STEP, tq, 1), F32)]),
        out_shape=(jax.ShapeDtypeStruct((rows, ATTN_DIM), BF16),)
        + ((stack_shape, stack_shape) if n_stack else ()),
        compiler_params=_params(2),
        name="fox_prompt",
    )(nvisit, z, z, z, ccol, crow, *layer_kv)
    return out if n_stack else out[0]


def _fox_sample_body(*refs):
    (q_ref, kn_ref, vn_ref, cn_ref, lc_ref, ck_ref, cv_ref, o_ref,
     qbd_scr, m_scr, l_scr, acc_scr, tail_scr) = refs
    c = pl.program_id(1)
    nc = pl.num_programs(1)
    t_new = q_ref.shape[0]
    nrow = t_new * N_HEADS
    kc = ck_ref.shape[0] // N_HEADS
    nblk = kc // SUFFIX_BLOCK
    cn = cn_ref[...]
    cq = jnp.concatenate([cn[:, t:t + 1] for t in range(t_new)], axis=0)
    col_head = lax.broadcasted_iota(jnp.int32, (N_HEADS, ATTN_DIM), 1) // HEAD_DIM
    row_head = lax.broadcasted_iota(jnp.int32, (N_HEADS, ATTN_DIM), 0)
    head_mask = col_head == row_head

    @pl.when(c == 0)
    def _():
        q = q_ref[...].astype(F32)
        q3 = jnp.where(head_mask[None], jnp.broadcast_to(q[:, None, :], (t_new, N_HEADS, ATTN_DIM)), 0.0)
        qbd_scr[...] = q3.reshape(nrow, ATTN_DIM).astype(BF16)
        m_scr[...] = jnp.full_like(m_scr, NEG_INF)
        l_scr[...] = jnp.zeros_like(l_scr)
        acc_scr[...] = jnp.zeros_like(acc_scr)
        tail_scr[...] = jnp.zeros_like(tail_scr)

    def online(s, vals):
        m_prev = m_scr[...]
        m_new = jnp.maximum(m_prev, jnp.max(s, axis=1, keepdims=True))
        alpha = jnp.exp(m_prev - m_new)
        p = jnp.exp(s - m_new)
        l_scr[...] = alpha * l_scr[...] + jnp.sum(p, axis=1, keepdims=True)
        acc_scr[...] = alpha * acc_scr[...] + _dot(p.astype(BF16), vals)
        m_scr[...] = m_new

    lc = lc_ref[...]
    x = jnp.concatenate([lc[:, b * SUFFIX_BLOCK:(b + 1) * SUFFIX_BLOCK] for b in range(nblk)], axis=0)
    jj = lax.broadcasted_iota(jnp.int32, (SUFFIX_BLOCK, SUFFIX_BLOCK), 0)
    ss = lax.broadcasted_iota(jnp.int32, (SUFFIX_BLOCK, SUFFIX_BLOCK), 1)
    excl = _dot_exact01(jnp.where(jj > ss, 1.0, 0.0).astype(BF16), x, nt=True)
    off = tail_scr[...]
    pieces = [None] * nblk
    for b in range(nblk - 1, -1, -1):
        eb = excl[b * N_HEADS:(b + 1) * N_HEADS, :]
        pieces[b] = eb + off
        off = off + (eb[:, 0:1] + x[b * N_HEADS:(b + 1) * N_HEADS, 0:1])
    tail_scr[...] = off
    suffix = jnp.concatenate(pieces, axis=1)

    def all_heads(ref):
        return jnp.concatenate([ref[pl.ds(h, kc, stride=N_HEADS), :].astype(BF16)
                                for h in range(N_HEADS)], axis=1)

    s = _dot_nt(qbd_scr[...], all_heads(ck_ref))
    s = (s.reshape(t_new, N_HEADS, kc) + cq.reshape(t_new, N_HEADS, 1) + suffix[None]).reshape(nrow, kc)
    online(s, all_heads(cv_ref))

    @pl.when(c == nc - 1)
    def _():
        sn = _dot_nt(qbd_scr[...], kn_ref[...])
        sn = sn.reshape(t_new, N_HEADS, t_new) + cq.reshape(t_new, N_HEADS, 1) + (-cn)[None]
        tq = lax.broadcasted_iota(jnp.int32, (t_new, N_HEADS, t_new), 0)
        tk = lax.broadcasted_iota(jnp.int32, (t_new, N_HEADS, t_new), 2)
        sn = jnp.where(tk <= tq, sn, NEG_INF).reshape(nrow, t_new)
        online(sn, vn_ref[...])
        o = (acc_scr[...] / l_scr[...]).reshape(t_new, N_HEADS, ATTN_DIM)
        o_ref[...] = jnp.sum(jnp.where(head_mask[None], o, 0.0), axis=1).astype(BF16)


def _fox_sample(z, cn_t, cache_k, cache_v, lc_t, layer, t_new):
    nb = cache_k.shape[1]
    past = cache_k.shape[2] // N_HEADS
    nc = past // KV_CHUNK
    nrow = t_new * N_HEADS
    qb, kb, vb = Q_OFF // ATTN_DIM, K_OFF // ATTN_DIM, V_OFF // ATTN_DIM
    cache_spec = pl.BlockSpec((None, None, KV_CHUNK * N_HEADS, HEAD_DIM),
                              lambda b, c: (layer, b, nc - 1 - c, 0))
    return pl.pallas_call(
        _fox_sample_body,
        grid=(nb, nc),
        in_specs=[
            pl.BlockSpec((t_new, ATTN_DIM), lambda b, c: (b, qb)),
            pl.BlockSpec((t_new, ATTN_DIM), lambda b, c: (b, kb)),
            pl.BlockSpec((t_new, ATTN_DIM), lambda b, c: (b, vb)),
            pl.BlockSpec((None, N_HEADS, t_new), lambda b, c: (b, 0, 0)),
            pl.BlockSpec((None, None, N_HEADS, KV_CHUNK), lambda b, c: (layer, b, 0, nc - 1 - c)),
            cache_spec, cache_spec,
        ],
        out_specs=pl.BlockSpec((t_new, ATTN_DIM), lambda b, c: (b, 0)),
        out_shape=jax.ShapeDtypeStruct((nb * t_new, ATTN_DIM), BF16),
        scratch_shapes=[pltpu.VMEM((nrow, ATTN_DIM), BF16), pltpu.VMEM((nrow, 1), F32),
                        pltpu.VMEM((nrow, 1), F32), pltpu.VMEM((nrow, ATTN_DIM), F32),
                        pltpu.VMEM((N_HEADS, 1), F32)],
        compiler_params=_params(2),
        name="fox_sample",
    )(z, z, z, cn_t, lc_t, cache_k, cache_v)


def _mix_body(*refs, seg, has_hist):
    zc_ref, ya_ref, gc_ref, ga_ref, wc_ref, wpc_ref, wpa_ref = refs[:7]
    n_in = 8 if has_hist else 7
    mg_ref, st_ref = refs[n_in:n_in + 2]
    i = pl.program_id(0)
    tm = zc_ref.shape[0]
    xc = zc_ref[:, 0:CONV_DIM].astype(F32)
    bc = zc_ref[:, CONV_DIM:2 * CONV_DIM].astype(F32)
    cc = zc_ref[:, 2 * CONV_DIM:3 * CONV_DIM].astype(F32)
    u = cc * xc
    if has_hist:
        hist = refs[7][...]
        h0 = _seg_rows(hist[:, 0, :], seg)
        h1 = _seg_rows(hist[:, 1, :], seg)
        nseg = tm // seg
        st_ref[...] = u.reshape(nseg, seg, CONV_DIM)[:, seg - 2:seg, :]
    else:
        carry_scr = refs[n_in + 2]

        @pl.when(i == 0)
        def _():
            carry_scr[...] = jnp.zeros_like(carry_scr)
        h0 = carry_scr[0:1, :]
        h1 = carry_scr[1:2, :]
        st_ref[...] = u[tm - 2:tm, :][None]
    yc = (bc * _conv3(u, h0, h1, wc_ref[...], seg)).astype(BF16)
    if not has_hist:
        carry_scr[0:2, :] = u[tm - 2:tm, :]
    t1 = _dot(yc, wpc_ref[...])
    t2 = _dot(ya_ref[...], wpa_ref[...])
    mg_ref[...] = (gc_ref[...].astype(F32) * t1 + ga_ref[...].astype(F32) * t2).astype(BF16)


def _mix(z, ya, w_conv, w_pc, w_pa, layer, tm, seg, hist):
    rows = z.shape[0]
    nm = rows // tm
    has_hist = hist is not None
    nseg = tm // seg if has_hist else 1
    in_specs = [
        pl.BlockSpec((tm, 3 * CONV_DIM), lambda i: (i, 0)),
        pl.BlockSpec((tm, ATTN_DIM), lambda i: (i, 0)),
        pl.BlockSpec((tm, D_MODEL), lambda i: (i, G_OFF // D_MODEL)),
        pl.BlockSpec((tm, D_MODEL), lambda i: (i, G_OFF // D_MODEL + 1)),
        pl.BlockSpec((None, 3, CONV_DIM), lambda i: (layer, 0, 0)),
        pl.BlockSpec((None, CONV_DIM, D_MODEL), lambda i: (layer, 0, 0)),
        pl.BlockSpec((None, ATTN_DIM, D_MODEL), lambda i: (layer, 0, 0)),
    ]
    args = [z, ya, z, z, w_conv, w_pc, w_pa]
    scratch = []
    if has_hist:
        in_specs.append(pl.BlockSpec((nseg, 2, CONV_DIM), lambda i: (0, 0, 0)))
        args.append(hist)
    else:
        scratch.append(pltpu.VMEM((8, CONV_DIM), F32))
    return pl.pallas_call(
        functools.partial(_mix_body, seg=seg, has_hist=has_hist),
        grid=(nm,),
        in_specs=in_specs,
        out_specs=(pl.BlockSpec((tm, D_MODEL), lambda i: (i, 0)),
                   pl.BlockSpec((nseg, 2, CONV_DIM), lambda i: (0, 0, 0))),
        out_shape=(jax.ShapeDtypeStruct((rows, D_MODEL), BF16),
                   jax.ShapeDtypeStruct((nseg, 2, CONV_DIM), F32)),
        scratch_shapes=scratch,
        compiler_params=_params(1),
        name="mix",
    )(*args)


def _oproj_body(x_ref, mg_ref, wo_ref, *rest):
    o_ref = rest[-1]
    x = _rows_after_meta(x_ref, rest[0], pl.program_id(0)) if len(rest) == 2 else x_ref[...]
    o_ref[...] = x + _dot(mg_ref[...], wo_ref[...])


def _oproj(x, meta, mg, w_o, layer, tm):
    has_meta = meta is not None
    rows = mg.shape[0]
    in_specs = [_meta_row_spec(tm) if has_meta else pl.BlockSpec((tm, D_MODEL), lambda i: (i, 0)),
                pl.BlockSpec((tm, D_MODEL), lambda i: (i, 0)),
                pl.BlockSpec((None, D_MODEL, D_MODEL), lambda i: (layer, 0, 0))]
    args = [x, mg, w_o]
    if has_meta:
        in_specs.append(pl.BlockSpec((N_META, D_MODEL), lambda i: (0, 0)))
        args.append(meta)
    return pl.pallas_call(
        _oproj_body,
        grid=(rows // tm,),
        in_specs=in_specs,
        out_specs=pl.BlockSpec((tm, D_MODEL), lambda i: (i, 0)),
        out_shape=jax.ShapeDtypeStruct((rows, D_MODEL), F32),
        compiler_params=_params(1),
        name="oproj",
    )(*args)


def _ffn_body(*refs, seg, has_hist):
    (x_ref, g_ref, wua_ref, wub_ref, wca_ref, wcb_ref, ba_ref, bb_ref, wd_ref) = refs[:9]
    n_in = 11 if has_hist else 9
    o_ref, st_ref, h_scr = refs[n_in:n_in + 3]
    i = pl.program_id(0)
    j = pl.program_id(1)
    tm = x_ref.shape[0]
    tf = wua_ref.shape[1]

    @pl.when(j == 0)
    def _():
        x = x_ref[...]
        h_scr[...] = _rmsnorm(x, g_ref[...]).astype(BF16)
        o_ref[...] = x

    rc = ROW_CHUNK if (not has_hist and tm % ROW_CHUNK == 0) else tm
    chunks = list(range(0, tm, rc))

    def up(r0):
        hb = h_scr[r0:r0 + rc, :]
        return _dot(hb, wua_ref[...]), _dot(hb, wub_ref[...])

    def gate(u, hist, w_ref, bias_ref):
        return _conv3(u, hist[0], hist[1], w_ref[...], min(seg, rc)) + bias_ref[...]

    def down(r0, gated):
        o_ref[r0:r0 + rc, :] += _dot(gated, wd_ref[...])

    if has_hist:
        nseg = tm // seg
        hist_a, hist_b = ((_seg_rows(r[:, 0, :], seg), _seg_rows(r[:, 1, :], seg))
                          for r in (refs[9][...], refs[10][...]))
    else:
        ca_scr, cb_scr = refs[n_in + 3:n_in + 5]

        @pl.when(i == 0)
        def _():
            ca_scr[j] = jnp.zeros((8, tf), F32)
            cb_scr[j] = jnp.zeros((8, tf), F32)
        hist_a = (ca_scr[j, 0:1, :], ca_scr[j, 1:2, :])
        hist_b = (cb_scr[j, 0:1, :], cb_scr[j, 1:2, :])

    u_next = up(chunks[0])
    pending = None
    for idx, r0 in enumerate(chunks):
        ua, ub = u_next
        if idx + 1 < len(chunks):
            u_next = up(chunks[idx + 1])
        a = gate(ua, hist_a, wca_ref, ba_ref)
        b = gate(ub, hist_b, wcb_ref, bb_ref)
        gated = ((a * _sigmoid(a)) * b).astype(BF16)
        hist_a = (ua[rc - 2:rc - 1, :], ua[rc - 1:rc, :])
        hist_b = (ub[rc - 2:rc - 1, :], ub[rc - 1:rc, :])
        if pending is not None:
            down(*pending)
        pending = (r0, gated)
    down(*pending)

    if has_hist:
        st_ref[:, :, 0, :] = ua.reshape(nseg, seg, tf)[:, seg - 2:seg, :]
        st_ref[:, :, 1, :] = ub.reshape(nseg, seg, tf)[:, seg - 2:seg, :]
    else:
        st_ref[0, :, 0, :] = ua[rc - 2:rc, :]
        st_ref[0, :, 1, :] = ub[rc - 2:rc, :]
        ca_scr[j, 0:2, :] = ua[rc - 2:rc, :]
        cb_scr[j, 0:2, :] = ub[rc - 2:rc, :]


def _ffn(x, g, w_up, w_fconv, b_fconv, w_down, layer, tm, seg, hist):
    rows = x.shape[0]
    nm = rows // tm
    nf = D_FF // TF
    has_hist = hist is not None
    nseg = tm // seg if has_hist else 1
    in_specs = [
        pl.BlockSpec((tm, D_MODEL), lambda i, j: (i, 0)),
        pl.BlockSpec((None, 1, D_MODEL), lambda i, j: (layer, 0, 0)),
        pl.BlockSpec((None, D_MODEL, TF), lambda i, j: (layer, 0, j)),
        pl.BlockSpec((None, D_MODEL, TF), lambda i, j: (layer, 0, nf + j)),
        pl.BlockSpec((None, 3, TF), lambda i, j: (layer, 0, j)),
        pl.BlockSpec((None, 3, TF), lambda i, j: (layer, 0, nf + j)),
        pl.BlockSpec((None, 1, TF), lambda i, j: (layer, 0, j)),
        pl.BlockSpec((None, 1, TF), lambda i, j: (layer, 0, nf + j)),
        pl.BlockSpec((None, TF, D_MODEL), lambda i, j: (layer, j, 0)),
    ]
    args = [x, g, w_up, w_up, w_fconv, w_fconv, b_fconv, b_fconv, w_down]
    scratch = [pltpu.VMEM((tm, D_MODEL), BF16)]
    if has_hist:
        in_specs += [pl.BlockSpec((nseg, 2, TF), lambda i, j: (0, 0, j)),
                     pl.BlockSpec((nseg, 2, TF), lambda i, j: (0, 0, nf + j))]
        args += [hist, hist]
    else:
        scratch += [pltpu.VMEM((nf, 8, TF), F32), pltpu.VMEM((nf, 8, TF), F32)]
    return pl.pallas_call(
        functools.partial(_ffn_body, seg=seg, has_hist=has_hist),
        grid=(nm, nf),
        in_specs=in_specs,
        out_specs=(pl.BlockSpec((tm, D_MODEL), lambda i, j: (i, 0)),
                   pl.BlockSpec((None, nseg, 2, 2, TF), lambda i, j: (i, 0, 0, 0, j))),
        out_shape=(jax.ShapeDtypeStruct((rows, D_MODEL), F32),
                   jax.ShapeDtypeStruct((nm, nseg, 2, 2, D_FF), F32)),
        scratch_shapes=scratch,
        compiler_params=_params(2),
        name="ffn",
    )(*args)


def _final_norm_body(x_ref, g_ref, o_ref):
    o_ref[...] = _rmsnorm(x_ref[...], g_ref[...])


def _final_norm(x, g, tm, skip_rows):
    rows = x.shape[0] - skip_rows
    return pl.pallas_call(
        _final_norm_body,
        grid=(rows // tm,),
        in_specs=[pl.BlockSpec((pl.Element(tm), pl.Element(D_MODEL)),
                               lambda i: (pl.multiple_of(i * tm + skip_rows, 8), 0)),
                  pl.BlockSpec((1, D_MODEL), lambda i: (0, 0))],
        out_specs=pl.BlockSpec((tm, D_MODEL), lambda i: (i, 0)),
        out_shape=jax.ShapeDtypeStruct((rows, D_MODEL), F32),
        compiler_params=_params(1),
        name="final_norm",
    )(x, g)


def kernel(x_prompt, x_sample, cache_k, cache_v, cache_logf, state_conv, state_ffn_conv, meta_tokens,
           g_mix, w_in, b_in, w_conv, w_pc, w_pa, w_o, g_ffn, w_up, w_fconv, b_fconv, w_down, g_final):
    depth = w_in.shape[0]
    batch, seq, _ = x_prompt.shape
    nb, t_new, _ = x_sample.shape
    past = cache_k.shape[2]
    lp = N_META + seq
    assert batch == 1 and past % KV_CHUNK == 0 and lp % TM_BIG == 0 and lp % TM_SMALL == 0
    rows_s = nb * t_new

    w_left = jnp.swapaxes(w_in, 1, 2).astype(BF16)
    w_gate = w_left[:, F_OFF + N_HEADS:]
    w_f = jnp.pad(w_left[:, F_OFF:F_OFF + N_HEADS], ((0, 0), (0, LANES - N_HEADS), (0, 0)))
    b_main = jnp.concatenate([b_in[:, :F_OFF], b_in[:, F_OFF + N_HEADS:]], axis=1)[:, None, :]
    b_f = jnp.pad(b_in[:, F_OFF:F_OFF + N_HEADS], ((0, 0), (0, LANES - N_HEADS)))[:, None, :]
    w_pc_b, w_pa_b, w_o_b = w_pc.astype(BF16), w_pa.astype(BF16), w_o.astype(BF16)
    w_up_b, w_down_b = w_up.astype(BF16), w_down.astype(BF16)
    g_mix3, g_ffn3, b_fconv3 = g_mix[:, None, :], g_ffn[:, None, :], b_fconv[:, None, :]

    xp, meta = x_prompt[0], meta_tokens.astype(F32)
    xs = x_sample.reshape(rows_s, D_MODEL)
    ck = cache_k.reshape(depth, nb, past * N_HEADS, HEAD_DIM)
    cv = cache_v.reshape(depth, nb, past * N_HEADS, HEAD_DIM)
    lc_t = jnp.swapaxes(cache_logf.astype(F32), 2, 3)

    k_p, v_p, k_s, v_s = [], [], [], []
    lf_p, lf_s, cst_p, cst_s, fst_p, fst_s = [], [], [], [], [], []
    for l in range(depth):
        z, k_new, v_new, lf, ccol, crow, plan = _inproj(xp, meta, g_mix3, w_left, w_gate, b_main, w_f, b_f,
                                                         l, lp, TM_BIG, TM_BIG, True)
        k_p.append(k_new); v_p.append(v_new)
        if l < depth - 1:
            ya = _fox_prompt(z, ccol, crow, plan, TM_BIG)
        else:
            ya, k_prompt, v_prompt = _fox_prompt(z, ccol, crow, plan, TM_BIG,
                                                 [a for kv in zip(k_p, v_p) for a in kv])
        mg, cst = _mix(z, ya, w_conv, w_pc_b, w_pa_b, l, TM_SMALL, TM_SMALL, None)
        x1 = _oproj(xp, meta, mg, w_o_b, l, TM_SMALL)
        xp, fst = _ffn(x1, g_ffn3, w_up_b, w_fconv, b_fconv3, w_down_b, l, TM_BIG, TM_BIG, None)
        meta = None
        lf_p.append(lf); cst_p.append(cst); fst_p.append(fst[-1].reshape(1, 2, 2 * D_FF))
        z, k_new, v_new, lf, _, crow, _ = _inproj(xs, None, g_mix3, w_left, w_gate, b_main, w_f, b_f, l,
                                                  rows_s, rows_s, t_new, False)
        k_s.append(k_new); v_s.append(v_new)
        cn_t = jnp.swapaxes(crow[0].reshape(N_HEADS, nb, t_new), 0, 1)
        ya = _fox_sample(z, cn_t, ck, cv, lc_t, l, t_new)
        mg, cst = _mix(z, ya, w_conv, w_pc_b, w_pa_b, l, rows_s, t_new, state_conv[l])
        x1 = _oproj(xs, None, mg, w_o_b, l, rows_s)
        xs, fst = _ffn(x1, g_ffn3, w_up_b, w_fconv, b_fconv3, w_down_b, l, rows_s, t_new,
                       state_ffn_conv[l])
        lf_s.append(lf); cst_s.append(cst); fst_s.append(fst[-1].reshape(nb, 2, 2 * D_FF))

    g_fin = g_final[None, :]
    y_prompt = _final_norm(xp, g_fin, TM_OUT, N_META)[None]
    y_sample = _final_norm(xs, g_fin, rows_s, 0).reshape(nb, t_new, D_MODEL)
    return (y_prompt, y_sample,
            k_prompt.reshape(depth, 1, lp, N_HEADS, HEAD_DIM),
            v_prompt.reshape(depth, 1, lp, N_HEADS, HEAD_DIM),
            jnp.stack(lf_p).reshape(depth, 1, lp, N_HEADS),
            jnp.stack(cst_p),
            jnp.stack(fst_p),
            jnp.stack(k_s).reshape(depth, nb, t_new, N_HEADS, HEAD_DIM),
            jnp.stack(v_s).reshape(depth, nb, t_new, N_HEADS, HEAD_DIM),
            jnp.stack(lf_s).reshape(depth, nb, t_new, N_HEADS),
            jnp.stack(cst_s),
            jnp.stack(fst_s))
```

```python
import functools

import jax
import jax.numpy as jnp
from jax import lax
from jax.experimental import pallas as pl
from jax.experimental.pallas import tpu as pltpu

F32 = jnp.float32
BF16 = jnp.bfloat16

D_MODEL = 2048
N_META = 16
CONV_DIM = D_MODEL // 2
HEAD_DIM = 128
ATTN_DIM = D_MODEL // 2
N_HEADS = ATTN_DIM // HEAD_DIM
D_FF = 2 * D_MODEL
EPS = 1e-6
ATTN_SCALE = HEAD_DIM ** -0.5
NEG_INF = -1e30
EXP_ZERO_BELOW = 106.0
F_OFF = 3 * CONV_DIM + 3 * ATTN_DIM
MAIN_DIM = F_OFF + 2 * D_MODEL

LANES = 128
VMEM_LIMIT = 60000 * 1024

Q_OFF = 3 * CONV_DIM
K_OFF = Q_OFF + ATTN_DIM
V_OFF = K_OFF + ATTN_DIM
G_OFF = V_OFF + ATTN_DIM

TN_IN = 1024
TF = 512
TM_BIG = 912
TM_SMALL = 432
TM_OUT = 512
ROW_CHUNK = 304
HEADS_PER_STEP = 2
KV_CHUNK = 2048
SUFFIX_BLOCK = 256


def _params(n_axes):
    return pltpu.CompilerParams(dimension_semantics=("arbitrary",) * n_axes,
                                vmem_limit_bytes=VMEM_LIMIT)


def _dot(a, b):
    return jnp.dot(a, b, preferred_element_type=F32)


def _dot_nt(a, b):
    return lax.dot_general(a, b, (((1,), (1,)), ((), ())), preferred_element_type=F32)


def _dot_exact01(mask_bf16, x, nt=False):
    hi = x.astype(BF16)
    r1 = x - hi.astype(F32)
    mid = r1.astype(BF16)
    lo = (r1 - mid.astype(F32)).astype(BF16)
    if nt:
        return _dot(hi, mask_bf16) + _dot(mid, mask_bf16) + _dot(lo, mask_bf16)
    return _dot(mask_bf16, hi) + _dot(mask_bf16, mid) + _dot(mask_bf16, lo)


def _rmsnorm(x, g):
    ms = jnp.mean(x * x, axis=-1, keepdims=True)
    return (x * lax.rsqrt(ms + EPS)) * g


def _sigmoid(x):
    return 1.0 / (1.0 + jnp.exp(-x))


def _conv3(u, h0rows, h1rows, w, seg):
    tm = u.shape[0]
    r = lax.broadcasted_iota(jnp.int32, (tm, 1), 0)
    if seg < tm:
        r = r % seg
    um1 = jnp.where(r == 0, h1rows, pltpu.roll(u, 1, 0))
    um2 = jnp.where(r == 0, h0rows, jnp.where(r == 1, h1rows, pltpu.roll(u, 2, 0)))
    return um2 * w[0:1] + um1 * w[1:2] + u * w[2:3]


def _seg_rows(h, seg):
    nseg, c = h.shape
    return jnp.broadcast_to(h[:, None, :], (nseg, seg, c)).reshape(nseg * seg, c)


def _rows_after_meta(x_ref, meta_ref, i):
    x = x_ref[...]
    shifted = jnp.concatenate([meta_ref[...], x[:x.shape[0] - N_META]], axis=0)
    return jnp.where(i == 0, shifted, x)


def _meta_row_spec(tm):
    return pl.BlockSpec((pl.Element(tm), pl.Element(D_MODEL)),
                        lambda i, *_: (pl.multiple_of(jnp.maximum(i * tm - N_META, 0), 8), 0))


def _max_row_norms(x):
    cols = []
    for h in range(N_HEADS):
        xh = x[:, h * HEAD_DIM:(h + 1) * HEAD_DIM].astype(F32)
        cols.append(jnp.sqrt(jnp.max(jnp.sum(xh * xh, axis=1, keepdims=True), axis=0, keepdims=True)))
    return jnp.concatenate(cols, axis=0)


def _inproj_body(*refs, seg, use_carry, has_meta):
    x_ref, g_ref, wl_ref, wg_ref, b_ref, wf_ref, bf_ref = refs[:7]
    (z_ref, k_ref, v_ref, lf_ref, ccol_ref, crow_ref, plan_ref,
     h_scr, carry_scr, tri_scr) = refs[7 + has_meta:]
    i = pl.program_id(0)
    j = pl.program_id(1)
    tm = x_ref.shape[0]
    jq, jk, jv, jg = (Q_OFF // TN_IN, K_OFF // TN_IN, V_OFF // TN_IN, G_OFF // TN_IN)
    plan_lane = lax.broadcasted_iota(jnp.int32, (N_HEADS, LANES), 1)

    @pl.when(j == 0)
    def _():
        x = _rows_after_meta(x_ref, refs[7], i) if has_meta else x_ref[...]
        hb = _rmsnorm(x, g_ref[...]).astype(BF16)
        h_scr[...] = hb
        fl = _dot_nt(hb, wf_ref[...]) + bf_ref[...]
        lf = jnp.minimum(fl, 0.0) - jnp.log1p(jnp.exp(-jnp.abs(fl)))
        lf_ref[...] = lf[:, :N_HEADS]

        @pl.when(i == 0)
        def _():
            r = lax.broadcasted_iota(jnp.int32, (tm, tm), 0)
            c = lax.broadcasted_iota(jnp.int32, (tm, tm), 1)
            tri = c <= r
            if seg < tm:
                tri = tri & ((r // seg) == (c // seg))
            tri_scr[...] = jnp.where(tri, 1.0, 0.0).astype(BF16)

        lane = lax.broadcasted_iota(jnp.int32, (tm, LANES), 1)
        hi = jnp.where(lane < N_HEADS, lf, 0.0)
        hi_b = hi.astype(BF16).astype(F32)
        mid = hi - hi_b
        mid_b = mid.astype(BF16).astype(F32)
        lo_b = (mid - mid_b).astype(BF16).astype(F32)
        packed = hi_b + pltpu.roll(mid_b, N_HEADS, 1) + pltpu.roll(lo_b, 2 * N_HEADS, 1)
        part = _dot(tri_scr[...], packed.astype(BF16))
        cum = part + pltpu.roll(part, LANES - N_HEADS, 1) + pltpu.roll(part, LANES - 2 * N_HEADS, 1)
        if use_carry:
            @pl.when(i == 0)
            def _():
                carry_scr[...] = jnp.zeros_like(carry_scr)
            cum = cum + carry_scr[...]
            carry_scr[...] = cum[tm - 1:tm, :]
        ccol_ref[...] = cum
        crow = cum.T[:N_HEADS, :]
        crow_ref[...] = crow
        plan_ref[...] = jnp.where(plan_lane == 2, crow[:, 0:1],
                                  jnp.where(plan_lane == 3, crow[:, tm - 1:tm], 0.0))

    @pl.when(j < jg)
    def _():
        acc = _dot_nt(h_scr[...], wl_ref[...]) + b_ref[...]
        is_q = (j >= jq) & (j < jk)
        zb = jnp.where(is_q, acc * ATTN_SCALE, acc).astype(BF16)
        z_ref[...] = zb

        @pl.when(is_q)
        def _():
            plan_ref[...] = jnp.where(plan_lane == 0, _max_row_norms(zb), plan_ref[...])

        @pl.when((j >= jk) & (j < jv))
        def _():
            k_ref[...] = acc
            plan_ref[...] = jnp.where(plan_lane == 1, _max_row_norms(zb), plan_ref[...])

        @pl.when((j >= jv) & (j < jg))
        def _():
            v_ref[...] = acc

    @pl.when(j >= jg)
    def _():
        z_ref[...] = _sigmoid(_dot_nt(h_scr[...], wg_ref[...]) + b_ref[...]).astype(BF16)


def _inproj(x, meta, g, w_left, w_gate, b_main, w_f, b_f, layer, n_valid, tm, seg, use_carry):
    has_meta = meta is not None
    rows = x.shape[0] + (N_META if has_meta else 0)
    nm = rows // tm
    nj = MAIN_DIM // TN_IN
    jk, jv, jg = K_OFF // TN_IN, V_OFF // TN_IN, G_OFF // TN_IN
    assert TN_IN == ATTN_DIM
    nkv = ATTN_DIM // TN_IN
    in_specs = [
        _meta_row_spec(tm) if has_meta else pl.BlockSpec((tm, D_MODEL), lambda i, j: (i, 0)),
        pl.BlockSpec((None, 1, D_MODEL), lambda i, j: (layer, 0, 0)),
        pl.BlockSpec((None, TN_IN, D_MODEL), lambda i, j: (layer, jnp.minimum(j, jg - 1), 0)),
        pl.BlockSpec((None, TN_IN, D_MODEL), lambda i, j: (layer, jnp.maximum(j - jg, 0), 0)),
        pl.BlockSpec((None, 1, TN_IN), lambda i, j: (layer, 0, j)),
        pl.BlockSpec((None, LANES, D_MODEL), lambda i, j: (layer, 0, 0)),
        pl.BlockSpec((None, 1, LANES), lambda i, j: (layer, 0, 0)),
    ]
    args = [x, g, w_left, w_gate, b_main, w_f, b_f]
    if has_meta:
        in_specs.append(pl.BlockSpec((N_META, D_MODEL), lambda i, j: (0, 0)))
        args.append(meta)
    out_shape = (
        jax.ShapeDtypeStruct((rows, MAIN_DIM), BF16),
        jax.ShapeDtypeStruct((n_valid, ATTN_DIM), F32),
        jax.ShapeDtypeStruct((n_valid, ATTN_DIM), F32),
        jax.ShapeDtypeStruct((n_valid, N_HEADS), F32),
        jax.ShapeDtypeStruct((rows, LANES), F32),
        jax.ShapeDtypeStruct((nm, N_HEADS, tm), F32),
        jax.ShapeDtypeStruct((nm, N_HEADS, LANES), F32),
    )
    out_specs = (
        pl.BlockSpec((tm, TN_IN), lambda i, j: (i, j)),
        pl.BlockSpec((tm, TN_IN), lambda i, j: (i, jnp.clip(j - jk, 0, nkv - 1)),
                     pipeline_mode=pl.Buffered(1)),
        pl.BlockSpec((tm, TN_IN), lambda i, j: (i, jnp.clip(j - jv, 0, nkv - 1)),
                     pipeline_mode=pl.Buffered(1)),
        pl.BlockSpec((tm, N_HEADS), lambda i, j: (i, 0)),
        pl.BlockSpec((tm, LANES), lambda i, j: (i, 0)),
        pl.BlockSpec((None, N_HEADS, tm), lambda i, j: (i, 0, 0)),
        pl.BlockSpec((None, N_HEADS, LANES), lambda i, j: (i, 0, 0)),
    )
    return pl.pallas_call(
        functools.partial(_inproj_body, seg=seg, use_carry=use_carry, has_meta=has_meta),
        grid=(nm, nj),
        in_specs=in_specs,
        out_specs=out_specs,
        out_shape=out_shape,
        scratch_shapes=[pltpu.VMEM((tm, D_MODEL), BF16), pltpu.VMEM((1, LANES), F32),
                        pltpu.VMEM((tm, tm), BF16)],
        compiler_params=_params(2),
        name="inproj",
    )(*args)


def _tiles_to_visit(plan):
    qn, kn, c_first, c_last = (plan[:, :, i].T for i in range(4))
    nq = qn.shape[1]
    gap = (qn[:, :, None] * (kn[:, None, :] + kn[:, :, None])
           + c_first[:, :, None] - c_last[:, None, :])
    qi = jnp.arange(nq)[:, None]
    kb = jnp.arange(nq)[None, :]
    needed = (kb < qi)[None] & (gap >= -EXP_ZERO_BELOW)
    return 1 + jnp.max(jnp.where(needed, (qi - kb)[None], 0), axis=2).astype(jnp.int32)


def _interleave(chains):
    chains = list(chains)
    while chains:
        for c in list(chains):
            try:
                next(c)
            except StopIteration:
                chains.remove(c)


def _fox_prompt_body(nvisit_ref, q_ref, k_ref, v_ref, ccol_ref, crow_ref, *rest, n_stack):
    layer_kv = rest[:2 * n_stack]
    o_ref = rest[2 * n_stack]
    stacked = rest[2 * n_stack + 1:len(rest) - 4]
    m_scr, l_scr, acc_scr, cq_scr = rest[len(rest) - 4:]
    for which, out_ref in enumerate(stacked):
        for layer in range(n_stack):
            out_ref[layer] = layer_kv[2 * layer + which][...]
    hg = pl.program_id(0)
    qi = pl.program_id(1)
    tq = q_ref.shape[0]
    tk = tq
    m_scr[...] = jnp.full_like(m_scr, NEG_INF)
    l_scr[...] = jnp.zeros_like(l_scr)
    acc_scr[...] = jnp.zeros_like(acc_scr)
    lane = lax.broadcasted_iota(jnp.int32, (tq, LANES), 1)
    heads = [hg * HEADS_PER_STEP + sub for sub in range(HEADS_PER_STEP)]
    for sub, h in enumerate(heads):
        cq_scr[sub] = jnp.sum(jnp.where(lane == h, ccol_ref[...], 0.0), axis=1, keepdims=True)

    def tile(sub, kb, diagonal):
        h = heads[sub]
        cols = slice(sub * HEAD_DIM, (sub + 1) * HEAD_DIM)
        start = pl.multiple_of(kb * tk, ROW_CHUNK)
        k = k_ref[pl.ds(start, tk), cols]
        v = v_ref[pl.ds(start, tk), cols]
        ck = crow_ref[kb, pl.ds(h, 1), :]
        chunks = list(range(0, tq, ROW_CHUNK))

        def n_keys(r0):
            return r0 + ROW_CHUNK if diagonal else tk

        def scores(r0):
            nk = n_keys(r0)
            return _dot_nt(q_ref[r0:r0 + ROW_CHUNK, cols], k[:nk])

        def softmax(r0, s):
            rows = slice(r0, r0 + ROW_CHUNK)
            nk = n_keys(r0)
            s = s + cq_scr[sub, rows, :] - ck[:, :nk]
            if diagonal:
                r = lax.broadcasted_iota(jnp.int32, (ROW_CHUNK, nk), 0) + r0
                c = lax.broadcasted_iota(jnp.int32, (ROW_CHUNK, nk), 1)
                s = jnp.where(c <= r, s, NEG_INF)
            m_prev = m_scr[sub, rows, :]
            m_new = jnp.maximum(m_prev, jnp.max(s, axis=1, keepdims=True))
            alpha = jnp.exp(m_prev - m_new)
            p = jnp.exp(s - m_new)
            l_scr[sub, rows, :] = alpha * l_scr[sub, rows, :] + jnp.sum(p, axis=1, keepdims=True)
            m_scr[sub, rows, :] = m_new
            return alpha, p.astype(BF16)

        def values(r0, alpha, p):
            rows = slice(r0, r0 + ROW_CHUNK)
            acc_scr[sub, rows, :] = alpha * acc_scr[sub, rows, :] + _dot(p, v[:n_keys(r0)])

        s_next = scores(chunks[0])
        yield
        pending = None
        for idx, r0 in enumerate(chunks):
            s_cur = s_next
            if idx + 1 < len(chunks):
                s_next = scores(chunks[idx + 1])
                yield
            alpha, p = softmax(r0, s_cur)
            yield
            if pending is not None:
                values(*pending)
                yield
            pending = (r0, alpha, p)
        values(*pending)

    subs = range(HEADS_PER_STEP)
    _interleave(tile(sub, qi, True) for sub in subs)

    visits = [nvisit_ref[h, qi] for h in heads]
    common = functools.reduce(jnp.minimum, visits)

    def together(kk, carry):
        _interleave(tile(sub, qi - kk, False) for sub in subs)
        return carry

    lax.fori_loop(1, common, together, 0)
    for sub in subs:
        def alone(kk, carry, sub=sub):
            _interleave([tile(sub, qi - kk, False)])
            return carry

        lax.fori_loop(common, visits[sub], alone, 0)
    for sub in subs:
        o_ref[:, sub * HEAD_DIM:(sub + 1) * HEAD_DIM] = (acc_scr[sub] / l_scr[sub]).astype(BF16)


def _fox_prompt(z, ccol, crow, plan, tq, layer_kv=()):
    rows = z.shape[0]
    nq = rows // tq
    width = HEADS_PER_STEP * HEAD_DIM
    qb, kb, vb = Q_OFF // width, K_OFF // width, V_OFF // width
    nvisit = _tiles_to_visit(plan)
    n_stack = len(layer_kv) // 2
    tile_spec = pl.BlockSpec((tq, width), lambda hg, qi, nv: (qi, hg))
    stack_spec = pl.BlockSpec((n_stack, tq, width), lambda hg, qi, nv: (0, qi, hg))
    stack_shape = jax.ShapeDtypeStruct((n_stack, rows, ATTN_DIM), F32)
    out = pl.pallas_call(
        functools.partial(_fox_prompt_body, n_stack=n_stack),
        grid_spec=pltpu.PrefetchScalarGridSpec(
            num_scalar_prefetch=1,
            grid=(N_HEADS // HEADS_PER_STEP, nq),
            in_specs=[
                pl.BlockSpec((tq, width), lambda hg, qi, nv: (qi, qb + hg)),
                pl.BlockSpec((rows, width), lambda hg, qi, nv: (0, kb + hg)),
                pl.BlockSpec((rows, width), lambda hg, qi, nv: (0, vb + hg)),
                pl.BlockSpec((tq, LANES), lambda hg, qi, nv: (qi, 0)),
                pl.BlockSpec((nq, N_HEADS, tq), lambda hg, qi, nv: (0, 0, 0)),
            ] + [tile_spec] * len(layer_kv),
            out_specs=(tile_spec,) + ((stack_spec, stack_spec) if n_stack else ()),
            scratch_shapes=[pltpu.VMEM((HEADS_PER_STEP, tq, 1), F32),
                            pltpu.VMEM((HEADS_PER_STEP, tq, 1), F32),
                            pltpu.VMEM((HEADS_PER_STEP, tq, HEAD_DIM), F32),
                            pltpu.VMEM((HEADS_PER_STEP, tq, 1), F32)]),
        out_shape=(jax.ShapeDtypeStruct((rows, ATTN_DIM), BF16),)
        + ((stack_shape, stack_shape) if n_stack else ()),
        compiler_params=_params(2),
        name="fox_prompt",
    )(nvisit, z, z, z, ccol, crow, *layer_kv)
    return out if n_stack else out[0]


def _fox_sample_body(*refs):
    (q_ref, kn_ref, vn_ref, cn_ref, lc_ref, ck_ref, cv_ref, o_ref,
     qbd_scr, m_scr, l_scr, acc_scr, tail_scr) = refs
    c = pl.program_id(1)
    nc = pl.num_programs(1)
    t_new = q_ref.shape[0]
    nrow = t_new * N_HEADS
    kc = ck_ref.shape[0] // N_HEADS
    nblk = kc // SUFFIX_BLOCK
    cn = cn_ref[...]
    cq = jnp.concatenate([cn[:, t:t + 1] for t in range(t_new)], axis=0)
    col_head = lax.broadcasted_iota(jnp.int32, (N_HEADS, ATTN_DIM), 1) // HEAD_DIM
    row_head = lax.broadcasted_iota(jnp.int32, (N_HEADS, ATTN_DIM), 0)
    head_mask = col_head == row_head

    @pl.when(c == 0)
    def _():
        q = q_ref[...].astype(F32)
        q3 = jnp.where(head_mask[None], jnp.broadcast_to(q[:, None, :], (t_new, N_HEADS, ATTN_DIM)), 0.0)
        qbd_scr[...] = q3.reshape(nrow, ATTN_DIM).astype(BF16)
        m_scr[...] = jnp.full_like(m_scr, NEG_INF)
        l_scr[...] = jnp.zeros_like(l_scr)
        acc_scr[...] = jnp.zeros_like(acc_scr)
        tail_scr[...] = jnp.zeros_like(tail_scr)

    def online(s, vals):
        m_prev = m_scr[...]
        m_new = jnp.maximum(m_prev, jnp.max(s, axis=1, keepdims=True))
        alpha = jnp.exp(m_prev - m_new)
        p = jnp.exp(s - m_new)
        l_scr[...] = alpha * l_scr[...] + jnp.sum(p, axis=1, keepdims=True)
        acc_scr[...] = alpha * acc_scr[...] + _dot(p.astype(BF16), vals)
        m_scr[...] = m_new

    lc = lc_ref[...]
    x = jnp.concatenate([lc[:, b * SUFFIX_BLOCK:(b + 1) * SUFFIX_BLOCK] for b in range(nblk)], axis=0)
    jj = lax.broadcasted_iota(jnp.int32, (SUFFIX_BLOCK, SUFFIX_BLOCK), 0)
    ss = lax.broadcasted_iota(jnp.int32, (SUFFIX_BLOCK, SUFFIX_BLOCK), 1)
    excl = _dot_exact01(jnp.where(jj > ss, 1.0, 0.0).astype(BF16), x, nt=True)
    off = tail_scr[...]
    pieces = [None] * nblk
    for b in range(nblk - 1, -1, -1):
        eb = excl[b * N_HEADS:(b + 1) * N_HEADS, :]
        pieces[b] = eb + off
        off = off + (eb[:, 0:1] + x[b * N_HEADS:(b + 1) * N_HEADS, 0:1])
    tail_scr[...] = off
    suffix = jnp.concatenate(pieces, axis=1)

    def all_heads(ref):
        return jnp.concatenate([ref[pl.ds(h, kc, stride=N_HEADS), :].astype(BF16)
                                for h in range(N_HEADS)], axis=1)

    s = _dot_nt(qbd_scr[...], all_heads(ck_ref))
    s = (s.reshape(t_new, N_HEADS, kc) + cq.reshape(t_new, N_HEADS, 1) + suffix[None]).reshape(nrow, kc)
    online(s, all_heads(cv_ref))

    @pl.when(c == nc - 1)
    def _():
        sn = _dot_nt(qbd_scr[...], kn_ref[...])
        sn = sn.reshape(t_new, N_HEADS, t_new) + cq.reshape(t_new, N_HEADS, 1) + (-cn)[None]
        tq = lax.broadcasted_iota(jnp.int32, (t_new, N_HEADS, t_new), 0)
        tk = lax.broadcasted_iota(jnp.int32, (t_new, N_HEADS, t_new), 2)
        sn = jnp.where(tk <= tq, sn, NEG_INF).reshape(nrow, t_new)
        online(sn, vn_ref[...])
        o = (acc_scr[...] / l_scr[...]).reshape(t_new, N_HEADS, ATTN_DIM)
        o_ref[...] = jnp.sum(jnp.where(head_mask[None], o, 0.0), axis=1).astype(BF16)


def _fox_sample(z, cn_t, cache_k, cache_v, lc_t, layer, t_new):
    nb = cache_k.shape[1]
    past = cache_k.shape[2] // N_HEADS
    nc = past // KV_CHUNK
    nrow = t_new * N_HEADS
    qb, kb, vb = Q_OFF // ATTN_DIM, K_OFF // ATTN_DIM, V_OFF // ATTN_DIM
    cache_spec = pl.BlockSpec((None, None, KV_CHUNK * N_HEADS, HEAD_DIM),
                              lambda b, c: (layer, b, nc - 1 - c, 0))
    return pl.pallas_call(
        _fox_sample_body,
        grid=(nb, nc),
        in_specs=[
            pl.BlockSpec((t_new, ATTN_DIM), lambda b, c: (b, qb)),
            pl.BlockSpec((t_new, ATTN_DIM), lambda b, c: (b, kb)),
            pl.BlockSpec((t_new, ATTN_DIM), lambda b, c: (b, vb)),
            pl.BlockSpec((None, N_HEADS, t_new), lambda b, c: (b, 0, 0)),
            pl.BlockSpec((None, None, N_HEADS, KV_CHUNK), lambda b, c: (layer, b, 0, nc - 1 - c)),
            cache_spec, cache_spec,
        ],
        out_specs=pl.BlockSpec((t_new, ATTN_DIM), lambda b, c: (b, 0)),
        out_shape=jax.ShapeDtypeStruct((nb * t_new, ATTN_DIM), BF16),
        scratch_shapes=[pltpu.VMEM((nrow, ATTN_DIM), BF16), pltpu.VMEM((nrow, 1), F32),
                        pltpu.VMEM((nrow, 1), F32), pltpu.VMEM((nrow, ATTN_DIM), F32),
                        pltpu.VMEM((N_HEADS, 1), F32)],
        compiler_params=_params(2),
        name="fox_sample",
    )(z, z, z, cn_t, lc_t, cache_k, cache_v)


def _mix_body(*refs, seg, has_hist):
    zc_ref, ya_ref, gc_ref, ga_ref, wc_ref, wpc_ref, wpa_ref = refs[:7]
    n_in = 8 if has_hist else 7
    mg_ref, st_ref = refs[n_in:n_in + 2]
    i = pl.program_id(0)
    tm = zc_ref.shape[0]
    xc = zc_ref[:, 0:CONV_DIM].astype(F32)
    bc = zc_ref[:, CONV_DIM:2 * CONV_DIM].astype(F32)
    cc = zc_ref[:, 2 * CONV_DIM:3 * CONV_DIM].astype(F32)
    u = cc * xc
    if has_hist:
        hist = refs[7][...]
        h0 = _seg_rows(hist[:, 0, :], seg)
        h1 = _seg_rows(hist[:, 1, :], seg)
        nseg = tm // seg
        st_ref[...] = u.reshape(nseg, seg, CONV_DIM)[:, seg - 2:seg, :]
    else:
        carry_scr = refs[n_in + 2]

        @pl.when(i == 0)
        def _():
            carry_scr[...] = jnp.zeros_like(carry_scr)
        h0 = carry_scr[0:1, :]
        h1 = carry_scr[1:2, :]
        st_ref[...] = u[tm - 2:tm, :][None]
    yc = (bc * _conv3(u, h0, h1, wc_ref[...], seg)).astype(BF16)
    if not has_hist:
        carry_scr[0:2, :] = u[tm - 2:tm, :]
    t1 = _dot(yc, wpc_ref[...])
    t2 = _dot(ya_ref[...], wpa_ref[...])
    mg_ref[...] = (gc_ref[...].astype(F32) * t1 + ga_ref[...].astype(F32) * t2).astype(BF16)


def _mix(z, ya, w_conv, w_pc, w_pa, layer, tm, seg, hist):
    rows = z.shape[0]
    nm = rows // tm
    has_hist = hist is not None
    nseg = tm // seg if has_hist else 1
    in_specs = [
        pl.BlockSpec((tm, 3 * CONV_DIM), lambda i: (i, 0)),
        pl.BlockSpec((tm, ATTN_DIM), lambda i: (i, 0)),
        pl.BlockSpec((tm, D_MODEL), lambda i: (i, G_OFF // D_MODEL)),
        pl.BlockSpec((tm, D_MODEL), lambda i: (i, G_OFF // D_MODEL + 1)),
        pl.BlockSpec((None, 3, CONV_DIM), lambda i: (layer, 0, 0)),
        pl.BlockSpec((None, CONV_DIM, D_MODEL), lambda i: (layer, 0, 0)),
        pl.BlockSpec((None, ATTN_DIM, D_MODEL), lambda i: (layer, 0, 0)),
    ]
    args = [z, ya, z, z, w_conv, w_pc, w_pa]
    scratch = []
    if has_hist:
        in_specs.append(pl.BlockSpec((nseg, 2, CONV_DIM), lambda i: (0, 0, 0)))
        args.append(hist)
    else:
        scratch.append(pltpu.VMEM((8, CONV_DIM), F32))
    return pl.pallas_call(
        functools.partial(_mix_body, seg=seg, has_hist=has_hist),
        grid=(nm,),
        in_specs=in_specs,
        out_specs=(pl.BlockSpec((tm, D_MODEL), lambda i: (i, 0)),
                   pl.BlockSpec((nseg, 2, CONV_DIM), lambda i: (0, 0, 0))),
        out_shape=(jax.ShapeDtypeStruct((rows, D_MODEL), BF16),
                   jax.ShapeDtypeStruct((nseg, 2, CONV_DIM), F32)),
        scratch_shapes=scratch,
        compiler_params=_params(1),
        name="mix",
    )(*args)


def _oproj_body(x_ref, mg_ref, wo_ref, *rest):
    o_ref = rest[-1]
    x = _rows_after_meta(x_ref, rest[0], pl.program_id(0)) if len(rest) == 2 else x_ref[...]
    o_ref[...] = x + _dot(mg_ref[...], wo_ref[...])


def _oproj(x, meta, mg, w_o, layer, tm):
    has_meta = meta is not None
    rows = mg.shape[0]
    in_specs = [_meta_row_spec(tm) if has_meta else pl.BlockSpec((tm, D_MODEL), lambda i: (i, 0)),
                pl.BlockSpec((tm, D_MODEL), lambda i: (i, 0)),
                pl.BlockSpec((None, D_MODEL, D_MODEL), lambda i: (layer, 0, 0))]
    args = [x, mg, w_o]
    if has_meta:
        in_specs.append(pl.BlockSpec((N_META, D_MODEL), lambda i: (0, 0)))
        args.append(meta)
    return pl.pallas_call(
        _oproj_body,
        grid=(rows // tm,),
        in_specs=in_specs,
        out_specs=pl.BlockSpec((tm, D_MODEL), lambda i: (i, 0)),
        out_shape=jax.ShapeDtypeStruct((rows, D_MODEL), F32),
        compiler_params=_params(1),
        name="oproj",
    )(*args)


def _ffn_body(*refs, seg, has_hist):
    (x_ref, g_ref, wua_ref, wub_ref, wca_ref, wcb_ref, ba_ref, bb_ref, wd_ref) = refs[:9]
    n_in = 11 if has_hist else 9
    o_ref, st_ref, h_scr = refs[n_in:n_in + 3]
    i = pl.program_id(0)
    j = pl.program_id(1)
    tm = x_ref.shape[0]
    tf = wua_ref.shape[1]

    @pl.when(j == 0)
    def _():
        x = x_ref[...]
        h_scr[...] = _rmsnorm(x, g_ref[...]).astype(BF16)
        o_ref[...] = x

    rc = ROW_CHUNK if (not has_hist and tm % ROW_CHUNK == 0) else tm
    chunks = list(range(0, tm, rc))

    def up(r0):
        hb = h_scr[r0:r0 + rc, :]
        return _dot(hb, wua_ref[...]), _dot(hb, wub_ref[...])

    def gate(u, hist, w_ref, bias_ref):
        return _conv3(u, hist[0], hist[1], w_ref[...], min(seg, rc)) + bias_ref[...]

    def down(r0, gated):
        o_ref[r0:r0 + rc, :] += _dot(gated, wd_ref[...])

    if has_hist:
        nseg = tm // seg
        hist_a, hist_b = ((_seg_rows(r[:, 0, :], seg), _seg_rows(r[:, 1, :], seg))
                          for r in (refs[9][...], refs[10][...]))
    else:
        ca_scr, cb_scr = refs[n_in + 3:n_in + 5]

        @pl.when(i == 0)
        def _():
            ca_scr[j] = jnp.zeros((8, tf), F32)
            cb_scr[j] = jnp.zeros((8, tf), F32)
        hist_a = (ca_scr[j, 0:1, :], ca_scr[j, 1:2, :])
        hist_b = (cb_scr[j, 0:1, :], cb_scr[j, 1:2, :])

    u_next = up(chunks[0])
    pending = None
    for idx, r0 in enumerate(chunks):
        ua, ub = u_next
        if idx + 1 < len(chunks):
            u_next = up(chunks[idx + 1])
        a = gate(ua, hist_a, wca_ref, ba_ref)
        b = gate(ub, hist_b, wcb_ref, bb_ref)
        gated = ((a * _sigmoid(a)) * b).astype(BF16)
        hist_a = (ua[rc - 2:rc - 1, :], ua[rc - 1:rc, :])
        hist_b = (ub[rc - 2:rc - 1, :], ub[rc - 1:rc, :])
        if pending is not None:
            down(*pending)
        pending = (r0, gated)
    down(*pending)

    if has_hist:
        st_ref[:, :, 0, :] = ua.reshape(nseg, seg, tf)[:, seg - 2:seg, :]
        st_ref[:, :, 1, :] = ub.reshape(nseg, seg, tf)[:, seg - 2:seg, :]
    else:
        st_ref[0, :, 0, :] = ua[rc - 2:rc, :]
        st_ref[0, :, 1, :] = ub[rc - 2:rc, :]
        ca_scr[j, 0:2, :] = ua[rc - 2:rc, :]
        cb_scr[j, 0:2, :] = ub[rc - 2:rc, :]


def _ffn(x, g, w_up, w_fconv, b_fconv, w_down, layer, tm, seg, hist):
    rows = x.shape[0]
    nm = rows // tm
    nf = D_FF // TF
    has_hist = hist is not None
    nseg = tm // seg if has_hist else 1
    in_specs = [
        pl.BlockSpec((tm, D_MODEL), lambda i, j: (i, 0)),
        pl.BlockSpec((None, 1, D_MODEL), lambda i, j: (layer, 0, 0)),
        pl.BlockSpec((None, D_MODEL, TF), lambda i, j: (layer, 0, j)),
        pl.BlockSpec((None, D_MODEL, TF), lambda i, j: (layer, 0, nf + j)),
        pl.BlockSpec((None, 3, TF), lambda i, j: (layer, 0, j)),
        pl.BlockSpec((None, 3, TF), lambda i, j: (layer, 0, nf + j)),
        pl.BlockSpec((None, 1, TF), lambda i, j: (layer, 0, j)),
        pl.BlockSpec((None, 1, TF), lambda i, j: (layer, 0, nf + j)),
        pl.BlockSpec((None, TF, D_MODEL), lambda i, j: (layer, j, 0)),
    ]
    args = [x, g, w_up, w_up, w_fconv, w_fconv, b_fconv, b_fconv, w_down]
    scratch = [pltpu.VMEM((tm, D_MODEL), BF16)]
    if has_hist:
        in_specs += [pl.BlockSpec((nseg, 2, TF), lambda i, j: (0, 0, j)),
                     pl.BlockSpec((nseg, 2, TF), lambda i, j: (0, 0, nf + j))]
        args += [hist, hist]
    else:
        scratch += [pltpu.VMEM((nf, 8, TF), F32), pltpu.VMEM((nf, 8, TF), F32)]
    return pl.pallas_call(
        functools.partial(_ffn_body, seg=seg, has_hist=has_hist),
        grid=(nm, nf),
        in_specs=in_specs,
        out_specs=(pl.BlockSpec((tm, D_MODEL), lambda i, j: (i, 0)),
                   pl.BlockSpec((None, nseg, 2, 2, TF), lambda i, j: (i, 0, 0, 0, j))),
        out_shape=(jax.ShapeDtypeStruct((rows, D_MODEL), F32),
                   jax.ShapeDtypeStruct((nm, nseg, 2, 2, D_FF), F32)),
        scratch_shapes=scratch,
        compiler_params=_params(2),
        name="ffn",
    )(*args)


def _final_norm_body(x_ref, g_ref, o_ref):
    o_ref[...] = _rmsnorm(x_ref[...], g_ref[...])


def _final_norm(x, g, tm, skip_rows):
    rows = x.shape[0] - skip_rows
    return pl.pallas_call(
        _final_norm_body,
        grid=(rows // tm,),
        in_specs=[pl.BlockSpec((pl.Element(tm), pl.Element(D_MODEL)),
                               lambda i: (pl.multiple_of(i * tm + skip_rows, 8), 0)),
                  pl.BlockSpec((1, D_MODEL), lambda i: (0, 0))],
        out_specs=pl.BlockSpec((tm, D_MODEL), lambda i: (i, 0)),
        out_shape=jax.ShapeDtypeStruct((rows, D_MODEL), F32),
        compiler_params=_params(1),
        name="final_norm",
    )(x, g)


def kernel(x_prompt, x_sample, cache_k, cache_v, cache_logf, state_conv, state_ffn_conv, meta_tokens,
           g_mix, w_in, b_in, w_conv, w_pc, w_pa, w_o, g_ffn, w_up, w_fconv, b_fconv, w_down, g_final):
    depth = w_in.shape[0]
    batch, seq, _ = x_prompt.shape
    nb, t_new, _ = x_sample.shape
    past = cache_k.shape[2]
    lp = N_META + seq
    assert batch == 1 and past % KV_CHUNK == 0 and lp % TM_BIG == 0 and lp % TM_SMALL == 0
    rows_s = nb * t_new

    w_left = jnp.swapaxes(w_in, 1, 2).astype(BF16)
    w_gate = w_left[:, F_OFF + N_HEADS:]
    w_f = jnp.pad(w_left[:, F_OFF:F_OFF + N_HEADS], ((0, 0), (0, LANES - N_HEADS), (0, 0)))
    b_main = jnp.concatenate([b_in[:, :F_OFF], b_in[:, F_OFF + N_HEADS:]], axis=1)[:, None, :]
    b_f = jnp.pad(b_in[:, F_OFF:F_OFF + N_HEADS], ((0, 0), (0, LANES - N_HEADS)))[:, None, :]
    w_pc_b, w_pa_b, w_o_b = w_pc.astype(BF16), w_pa.astype(BF16), w_o.astype(BF16)
    w_up_b, w_down_b = w_up.astype(BF16), w_down.astype(BF16)
    g_mix3, g_ffn3, b_fconv3 = g_mix[:, None, :], g_ffn[:, None, :], b_fconv[:, None, :]

    xp, meta = x_prompt[0], meta_tokens.astype(F32)
    xs = x_sample.reshape(rows_s, D_MODEL)
    ck = cache_k.reshape(depth, nb, past * N_HEADS, HEAD_DIM)
    cv = cache_v.reshape(depth, nb, past * N_HEADS, HEAD_DIM)
    lc_t = jnp.swapaxes(cache_logf.astype(F32), 2, 3)

    k_p, v_p, k_s, v_s = [], [], [], []
    lf_p, lf_s, cst_p, cst_s, fst_p, fst_s = [], [], [], [], [], []
    for l in range(depth):
        z, k_new, v_new, lf, _, crow, _ = _inproj(xs, None, g_mix3, w_left, w_gate, b_main, w_f, b_f, l,
                                                  rows_s, rows_s, t_new, False)
        k_s.append(k_new); v_s.append(v_new)
        cn_t = jnp.swapaxes(crow[0].reshape(N_HEADS, nb, t_new), 0, 1)
        ya = _fox_sample(z, cn_t, ck, cv, lc_t, l, t_new)
        mg, cst = _mix(z, ya, w_conv, w_pc_b, w_pa_b, l, rows_s, t_new, state_conv[l])
        x1 = _oproj(xs, None, mg, w_o_b, l, rows_s)
        xs, fst = _ffn(x1, g_ffn3, w_up_b, w_fconv, b_fconv3, w_down_b, l, rows_s, t_new,
                       state_ffn_conv[l])
        lf_s.append(lf); cst_s.append(cst); fst_s.append(fst[-1].reshape(nb, 2, 2 * D_FF))

    xp, xs = lax.optimization_barrier((xp, xs))
    for l in range(depth):
        z, k_new, v_new, lf, ccol, crow, plan = _inproj(xp, meta, g_mix3, w_left, w_gate, b_main, w_f, b_f,
                                                         l, lp, TM_BIG, TM_BIG, True)
        k_p.append(k_new); v_p.append(v_new)
        if l < depth - 1:
            ya = _fox_prompt(z, ccol, crow, plan, TM_BIG)
        else:
            ya, k_prompt, v_prompt = _fox_prompt(z, ccol, crow, plan, TM_BIG,
                                                 [a for kv in zip(k_p, v_p) for a in kv])
        mg, cst = _mix(z, ya, w_conv, w_pc_b, w_pa_b, l, TM_SMALL, TM_SMALL, None)
        x1 = _oproj(xp, meta, mg, w_o_b, l, TM_SMALL)
        xp, fst = _ffn(x1, g_ffn3, w_up_b, w_fconv, b_fconv3, w_down_b, l, TM_BIG, TM_BIG, None)
        meta = None
        lf_p.append(lf); cst_p.append(cst); fst_p.append(fst[-1].reshape(1, 2, 2 * D_FF))

    g_fin = g_final[None, :]
    y_prompt = _final_norm(xp, g_fin, TM_OUT, N_META)[None]
    y_sample = _final_norm(xs, g_fin, rows_s, 0).reshape(nb, t_new, D_MODEL)
    return (y_prompt, y_sample,
            k_prompt.reshape(depth, 1, lp, N_HEADS, HEAD_DIM),
            v_prompt.reshape(depth, 1, lp, N_HEADS, HEAD_DIM),
            jnp.stack(lf_p).reshape(depth, 1, lp, N_HEADS),
            jnp.stack(cst_p),
            jnp.stack(fst_p),
            jnp.stack(k_s).reshape(depth, nb, t_new, N_HEADS, HEAD_DIM),
            jnp.stack(v_s).reshape(depth, nb, t_new, N_HEADS, HEAD_DIM),
            jnp.stack(lf_s).reshape(depth, nb, t_new, N_HEADS),
            jnp.stack(cst_s),
            jnp.stack(fst_s))
```

```python
import functools

import jax
import jax.numpy as jnp
from jax import lax
from jax.experimental import pallas as pl
from jax.experimental.pallas import tpu as pltpu

F32 = jnp.float32
BF16 = jnp.bfloat16

D_MODEL = 2048
N_META = 16
CONV_DIM = D_MODEL // 2
HEAD_DIM = 128
ATTN_DIM = D_MODEL // 2
N_HEADS = ATTN_DIM // HEAD_DIM
D_FF = 2 * D_MODEL
EPS = 1e-6
ATTN_SCALE = HEAD_DIM ** -0.5
NEG_INF = -1e30
EXP_ZERO_BELOW = 106.0
F_OFF = 3 * CONV_DIM + 3 * ATTN_DIM
MAIN_DIM = F_OFF + 2 * D_MODEL

LANES = 128
VMEM_LIMIT = 60000 * 1024

Q_OFF = 3 * CONV_DIM
K_OFF = Q_OFF + ATTN_DIM
V_OFF = K_OFF + ATTN_DIM
G_OFF = V_OFF + ATTN_DIM

TN_IN = 1024
TF = 512
TM_BIG = 912
TM_SMALL = 432
TM_OUT = 512
ROW_CHUNK = 304
HEADS_PER_STEP = 2
KV_CHUNK = 2048
SUFFIX_BLOCK = 256


def _params(n_axes):
    return pltpu.CompilerParams(dimension_semantics=("arbitrary",) * n_axes,
                                vmem_limit_bytes=VMEM_LIMIT)


def _dot(a, b):
    return jnp.dot(a, b, preferred_element_type=F32)


def _dot_nt(a, b):
    return lax.dot_general(a, b, (((1,), (1,)), ((), ())), preferred_element_type=F32)


def _dot_exact01(mask_bf16, x, nt=False):
    hi = x.astype(BF16)
    r1 = x - hi.astype(F32)
    mid = r1.astype(BF16)
    lo = (r1 - mid.astype(F32)).astype(BF16)
    if nt:
        return _dot(hi, mask_bf16) + _dot(mid, mask_bf16) + _dot(lo, mask_bf16)
    return _dot(mask_bf16, hi) + _dot(mask_bf16, mid) + _dot(mask_bf16, lo)


def _rmsnorm(x, g):
    ms = jnp.mean(x * x, axis=-1, keepdims=True)
    return (x * lax.rsqrt(ms + EPS)) * g


def _sigmoid(x):
    return 1.0 / (1.0 + jnp.exp(-x))


def _conv3(u, h0rows, h1rows, w, seg):
    tm = u.shape[0]
    r = lax.broadcasted_iota(jnp.int32, (tm, 1), 0)
    if seg < tm:
        r = r % seg
    um1 = jnp.where(r == 0, h1rows, pltpu.roll(u, 1, 0))
    um2 = jnp.where(r == 0, h0rows, jnp.where(r == 1, h1rows, pltpu.roll(u, 2, 0)))
    return um2 * w[0:1] + um1 * w[1:2] + u * w[2:3]


def _seg_rows(h, seg):
    nseg, c = h.shape
    return jnp.broadcast_to(h[:, None, :], (nseg, seg, c)).reshape(nseg * seg, c)


def _rows_after_meta(x_ref, meta_ref, i):
    x = x_ref[...]
    shifted = jnp.concatenate([meta_ref[...], x[:x.shape[0] - N_META]], axis=0)
    return jnp.where(i == 0, shifted, x)


def _meta_row_spec(tm):
    return pl.BlockSpec((pl.Element(tm), pl.Element(D_MODEL)),
                        lambda i, *_: (pl.multiple_of(jnp.maximum(i * tm - N_META, 0), 8), 0))


def _max_row_norms(x):
    cols = []
    for h in range(N_HEADS):
        xh = x[:, h * HEAD_DIM:(h + 1) * HEAD_DIM].astype(F32)
        cols.append(jnp.sqrt(jnp.max(jnp.sum(xh * xh, axis=1, keepdims=True), axis=0, keepdims=True)))
    return jnp.concatenate(cols, axis=0)


def _inproj_body(*refs, seg, use_carry, has_meta):
    x_ref, g_ref, wl_ref, wg_ref, b_ref, wf_ref, bf_ref = refs[:7]
    (z_ref, k_ref, v_ref, lf_ref, ccol_ref, crow_ref, plan_ref,
     h_scr, carry_scr, tri_scr) = refs[7 + has_meta:]
    i = pl.program_id(0)
    j = pl.program_id(1)
    tm = x_ref.shape[0]
    jq, jk, jv, jg = (Q_OFF // TN_IN, K_OFF // TN_IN, V_OFF // TN_IN, G_OFF // TN_IN)
    plan_lane = lax.broadcasted_iota(jnp.int32, (N_HEADS, LANES), 1)

    @pl.when(j == 0)
    def _():
        x = _rows_after_meta(x_ref, refs[7], i) if has_meta else x_ref[...]
        hb = _rmsnorm(x, g_ref[...]).astype(BF16)
        h_scr[...] = hb
        fl = _dot_nt(hb, wf_ref[...]) + bf_ref[...]
        lf = jnp.minimum(fl, 0.0) - jnp.log1p(jnp.exp(-jnp.abs(fl)))
        lf_ref[...] = lf[:, :N_HEADS]

        @pl.when(i == 0)
        def _():
            r = lax.broadcasted_iota(jnp.int32, (tm, tm), 0)
            c = lax.broadcasted_iota(jnp.int32, (tm, tm), 1)
            tri = c <= r
            if seg < tm:
                tri = tri & ((r // seg) == (c // seg))
            tri_scr[...] = jnp.where(tri, 1.0, 0.0).astype(BF16)

        lane = lax.broadcasted_iota(jnp.int32, (tm, LANES), 1)
        hi = jnp.where(lane < N_HEADS, lf, 0.0)
        hi_b = hi.astype(BF16).astype(F32)
        mid = hi - hi_b
        mid_b = mid.astype(BF16).astype(F32)
        lo_b = (mid - mid_b).astype(BF16).astype(F32)
        packed = hi_b + pltpu.roll(mid_b, N_HEADS, 1) + pltpu.roll(lo_b, 2 * N_HEADS, 1)
        part = _dot(tri_scr[...], packed.astype(BF16))
        cum = part + pltpu.roll(part, LANES - N_HEADS, 1) + pltpu.roll(part, LANES - 2 * N_HEADS, 1)
        if use_carry:
            @pl.when(i == 0)
            def _():
                carry_scr[...] = jnp.zeros_like(carry_scr)
            cum = cum + carry_scr[...]
            carry_scr[...] = cum[tm - 1:tm, :]
        ccol_ref[...] = cum
        crow = cum.T[:N_HEADS, :]
        crow_ref[...] = crow
        plan_ref[...] = jnp.where(plan_lane == 2, crow[:, 0:1],
                                  jnp.where(plan_lane == 3, crow[:, tm - 1:tm], 0.0))

    @pl.when(j < jg)
    def _():
        acc = _dot_nt(h_scr[...], wl_ref[...]) + b_ref[...]
        is_q = (j >= jq) & (j < jk)
        zb = jnp.where(is_q, acc * ATTN_SCALE, acc).astype(BF16)
        z_ref[...] = zb

        @pl.when(is_q)
        def _():
            plan_ref[...] = jnp.where(plan_lane == 0, _max_row_norms(zb), plan_ref[...])

        def store_heads(ref):
            for h in range(N_HEADS):
                ref[pl.ds(h, tm, stride=N_HEADS), :] = acc[:, h * HEAD_DIM:(h + 1) * HEAD_DIM]

        @pl.when((j >= jk) & (j < jv))
        def _():
            store_heads(k_ref)
            plan_ref[...] = jnp.where(plan_lane == 1, _max_row_norms(zb), plan_ref[...])

        @pl.when((j >= jv) & (j < jg))
        def _():
            store_heads(v_ref)

    @pl.when(j >= jg)
    def _():
        z_ref[...] = _sigmoid(_dot_nt(h_scr[...], wg_ref[...]) + b_ref[...]).astype(BF16)


def _inproj(x, meta, g, w_left, w_gate, b_main, w_f, b_f, layer, n_valid, tm, seg, use_carry):
    has_meta = meta is not None
    rows = x.shape[0] + (N_META if has_meta else 0)
    nm = rows // tm
    nj = MAIN_DIM // TN_IN
    jk, jv, jg = K_OFF // TN_IN, V_OFF // TN_IN, G_OFF // TN_IN
    assert TN_IN == ATTN_DIM
    in_specs = [
        _meta_row_spec(tm) if has_meta else pl.BlockSpec((tm, D_MODEL), lambda i, j: (i, 0)),
        pl.BlockSpec((None, 1, D_MODEL), lambda i, j: (layer, 0, 0)),
        pl.BlockSpec((None, TN_IN, D_MODEL), lambda i, j: (layer, jnp.minimum(j, jg - 1), 0)),
        pl.BlockSpec((None, TN_IN, D_MODEL), lambda i, j: (layer, jnp.maximum(j - jg, 0), 0)),
        pl.BlockSpec((None, 1, TN_IN), lambda i, j: (layer, 0, j)),
        pl.BlockSpec((None, LANES, D_MODEL), lambda i, j: (layer, 0, 0)),
        pl.BlockSpec((None, 1, LANES), lambda i, j: (layer, 0, 0)),
    ]
    args = [x, g, w_left, w_gate, b_main, w_f, b_f]
    if has_meta:
        in_specs.append(pl.BlockSpec((N_META, D_MODEL), lambda i, j: (0, 0)))
        args.append(meta)
    out_shape = (
        jax.ShapeDtypeStruct((rows, MAIN_DIM), BF16),
        jax.ShapeDtypeStruct((n_valid * N_HEADS, HEAD_DIM), F32),
        jax.ShapeDtypeStruct((n_valid * N_HEADS, HEAD_DIM), F32),
        jax.ShapeDtypeStruct((n_valid, N_HEADS), F32),
        jax.ShapeDtypeStruct((rows, LANES), F32),
        jax.ShapeDtypeStruct((nm, N_HEADS, tm), F32),
        jax.ShapeDtypeStruct((nm, N_HEADS, LANES), F32),
    )
    out_specs = (
        pl.BlockSpec((tm, TN_IN), lambda i, j: (i, j)),
        pl.BlockSpec((tm * N_HEADS, HEAD_DIM), lambda i, j: (i, 0), pipeline_mode=pl.Buffered(1)),
        pl.BlockSpec((tm * N_HEADS, HEAD_DIM), lambda i, j: (i, 0), pipeline_mode=pl.Buffered(1)),
        pl.BlockSpec((tm, N_HEADS), lambda i, j: (i, 0)),
        pl.BlockSpec((tm, LANES), lambda i, j: (i, 0)),
        pl.BlockSpec((None, N_HEADS, tm), lambda i, j: (i, 0, 0)),
        pl.BlockSpec((None, N_HEADS, LANES), lambda i, j: (i, 0, 0)),
    )
    return pl.pallas_call(
        functools.partial(_inproj_body, seg=seg, use_carry=use_carry, has_meta=has_meta),
        grid=(nm, nj),
        in_specs=in_specs,
        out_specs=out_specs,
        out_shape=out_shape,
        scratch_shapes=[pltpu.VMEM((tm, D_MODEL), BF16), pltpu.VMEM((1, LANES), F32),
                        pltpu.VMEM((tm, tm), BF16)],
        compiler_params=_params(2),
        name="inproj",
    )(*args)


def _tiles_to_visit(plan):
    qn, kn, c_first, c_last = (plan[:, :, i].T for i in range(4))
    nq = qn.shape[1]
    gap = (qn[:, :, None] * (kn[:, None, :] + kn[:, :, None])
           + c_first[:, :, None] - c_last[:, None, :])
    qi = jnp.arange(nq)[:, None]
    kb = jnp.arange(nq)[None, :]
    needed = (kb < qi)[None] & (gap >= -EXP_ZERO_BELOW)
    return 1 + jnp.max(jnp.where(needed, (qi - kb)[None], 0), axis=2).astype(jnp.int32)


def _interleave(chains):
    chains = list(chains)
    while chains:
        for c in list(chains):
            try:
                next(c)
            except StopIteration:
                chains.remove(c)


def _fox_prompt_body(nvisit_ref, q_ref, k_ref, v_ref, ccol_ref, crow_ref, *rest, n_stack):
    layer_kv = rest[:2 * n_stack]
    o_ref = rest[2 * n_stack]
    stacked = rest[2 * n_stack + 1:len(rest) - 4]
    m_scr, l_scr, acc_scr, cq_scr = rest[len(rest) - 4:]
    for which, out_ref in enumerate(stacked):
        for layer in range(n_stack):
            out_ref[layer] = layer_kv[2 * layer + which][...]
    hg = pl.program_id(0)
    qi = pl.program_id(1)
    tq = q_ref.shape[0]
    tk = tq
    m_scr[...] = jnp.full_like(m_scr, NEG_INF)
    l_scr[...] = jnp.zeros_like(l_scr)
    acc_scr[...] = jnp.zeros_like(acc_scr)
    lane = lax.broadcasted_iota(jnp.int32, (tq, LANES), 1)
    heads = [hg * HEADS_PER_STEP + sub for sub in range(HEADS_PER_STEP)]
    for sub, h in enumerate(heads):
        cq_scr[sub] = jnp.sum(jnp.where(lane == h, ccol_ref[...], 0.0), axis=1, keepdims=True)

    def tile(sub, kb, diagonal):
        h = heads[sub]
        cols = slice(sub * HEAD_DIM, (sub + 1) * HEAD_DIM)
        start = pl.multiple_of(kb * tk, ROW_CHUNK)
        k = k_ref[pl.ds(start, tk), cols]
        v = v_ref[pl.ds(start, tk), cols]
        ck = crow_ref[kb, pl.ds(h, 1), :]
        chunks = list(range(0, tq, ROW_CHUNK))

        def n_keys(r0):
            return r0 + ROW_CHUNK if diagonal else tk

        def scores(r0):
            nk = n_keys(r0)
            return _dot_nt(q_ref[r0:r0 + ROW_CHUNK, cols], k[:nk])

        def softmax(r0, s):
            rows = slice(r0, r0 + ROW_CHUNK)
            nk = n_keys(r0)
            s = s + cq_scr[sub, rows, :] - ck[:, :nk]
            if diagonal:
                r = lax.broadcasted_iota(jnp.int32, (ROW_CHUNK, nk), 0) + r0
                c = lax.broadcasted_iota(jnp.int32, (ROW_CHUNK, nk), 1)
                s = jnp.where(c <= r, s, NEG_INF)
            m_prev = m_scr[sub, rows, :]
            m_new = jnp.maximum(m_prev, jnp.max(s, axis=1, keepdims=True))
            alpha = jnp.exp(m_prev - m_new)
            p = jnp.exp(s - m_new)
            l_scr[sub, rows, :] = alpha * l_scr[sub, rows, :] + jnp.sum(p, axis=1, keepdims=True)
            m_scr[sub, rows, :] = m_new
            return alpha, p.astype(BF16)

        def values(r0, alpha, p):
            rows = slice(r0, r0 + ROW_CHUNK)
            acc_scr[sub, rows, :] = alpha * acc_scr[sub, rows, :] + _dot(p, v[:n_keys(r0)])

        s_next = scores(chunks[0])
        yield
        pending = None
        for idx, r0 in enumerate(chunks):
            s_cur = s_next
            if idx + 1 < len(chunks):
                s_next = scores(chunks[idx + 1])
                yield
            alpha, p = softmax(r0, s_cur)
            yield
            if pending is not None:
                values(*pending)
                yield
            pending = (r0, alpha, p)
        values(*pending)

    subs = range(HEADS_PER_STEP)
    _interleave(tile(sub, qi, True) for sub in subs)

    visits = [nvisit_ref[h, qi] for h in heads]
    common = functools.reduce(jnp.minimum, visits)

    def together(kk, carry):
        _interleave(tile(sub, qi - kk, False) for sub in subs)
        return carry

    lax.fori_loop(1, common, together, 0)
    for sub in subs:
        def alone(kk, carry, sub=sub):
            _interleave([tile(sub, qi - kk, False)])
            return carry

        lax.fori_loop(common, visits[sub], alone, 0)
    for sub in subs:
        o_ref[:, sub * HEAD_DIM:(sub + 1) * HEAD_DIM] = (acc_scr[sub] / l_scr[sub]).astype(BF16)


def _fox_prompt(z, ccol, crow, plan, tq, layer_kv=()):
    rows = z.shape[0]
    nq = rows // tq
    width = HEADS_PER_STEP * HEAD_DIM
    qb, kb, vb = Q_OFF // width, K_OFF // width, V_OFF // width
    nvisit = _tiles_to_visit(plan)
    n_stack = len(layer_kv) // 2
    tile_spec = pl.BlockSpec((tq, width), lambda hg, qi, nv: (qi, hg))
    n_hg = N_HEADS // HEADS_PER_STEP
    run = tq * N_HEADS // n_hg
    copy_spec = pl.BlockSpec((run, HEAD_DIM), lambda hg, qi, nv: (qi * n_hg + hg, 0))
    stack_spec = pl.BlockSpec((n_stack, run, HEAD_DIM), lambda hg, qi, nv: (0, qi * n_hg + hg, 0))
    stack_shape = jax.ShapeDtypeStruct((n_stack, rows * N_HEADS, HEAD_DIM), F32)
    out = pl.pallas_call(
        functools.partial(_fox_prompt_body, n_stack=n_stack),
        grid_spec=pltpu.PrefetchScalarGridSpec(
            num_scalar_prefetch=1,
            grid=(N_HEADS // HEADS_PER_STEP, nq),
            in_specs=[
                pl.BlockSpec((tq, width), lambda hg, qi, nv: (qi, qb + hg)),
                pl.BlockSpec((rows, width), lambda hg, qi, nv: (0, kb + hg)),
                pl.BlockSpec((rows, width), lambda hg, qi, nv: (0, vb + hg)),
                pl.BlockSpec((tq, LANES), lambda hg, qi, nv: (qi, 0)),
                pl.BlockSpec((nq, N_HEADS, tq), lambda hg, qi, nv: (0, 0, 0)),
            ] + [copy_spec] * len(layer_kv),
            out_specs=(tile_spec,) + ((stack_spec, stack_spec) if n_stack else ()),
            scratch_shapes=[pltpu.VMEM((HEADS_PER_STEP, tq, 1), F32),
                            pltpu.VMEM((HEADS_PER_STEP, tq, 1), F32),
                            pltpu.VMEM((HEADS_PER_STEP, tq, HEAD_DIM), F32),
                            pltpu.VMEM((HEADS_PER_STEP, tq, 1), F32)]),
        out_shape=(jax.ShapeDtypeStruct((rows, ATTN_DIM), BF16),)
        + ((stack_shape, stack_shape) if n_stack else ()),
        compiler_params=_params(2),
        name="fox_prompt",
    )(nvisit, z, z, z, ccol, crow, *layer_kv)
    return out if n_stack else out[0]


def _fox_sample_body(*refs):
    (q_ref, kn_ref, vn_ref, cn_ref, lc_ref, ck_ref, cv_ref, o_ref,
     qbd_scr, m_scr, l_scr, acc_scr, tail_scr) = refs
    c = pl.program_id(1)
    nc = pl.num_programs(1)
    t_new = q_ref.shape[0]
    nrow = t_new * N_HEADS
    kc = ck_ref.shape[0] // N_HEADS
    nblk = kc // SUFFIX_BLOCK
    cn = cn_ref[...]
    cq = jnp.concatenate([cn[:, t:t + 1] for t in range(t_new)], axis=0)
    col_head = lax.broadcasted_iota(jnp.int32, (N_HEADS, ATTN_DIM), 1) // HEAD_DIM
    row_head = lax.broadcasted_iota(jnp.int32, (N_HEADS, ATTN_DIM), 0)
    head_mask = col_head == row_head

    @pl.when(c == 0)
    def _():
        q = q_ref[...].astype(F32)
        q3 = jnp.where(head_mask[None], jnp.broadcast_to(q[:, None, :], (t_new, N_HEADS, ATTN_DIM)), 0.0)
        qbd_scr[...] = q3.reshape(nrow, ATTN_DIM).astype(BF16)
        m_scr[...] = jnp.full_like(m_scr, NEG_INF)
        l_scr[...] = jnp.zeros_like(l_scr)
        acc_scr[...] = jnp.zeros_like(acc_scr)
        tail_scr[...] = jnp.zeros_like(tail_scr)

    def online(s, vals):
        m_prev = m_scr[...]
        m_new = jnp.maximum(m_prev, jnp.max(s, axis=1, keepdims=True))
        alpha = jnp.exp(m_prev - m_new)
        p = jnp.exp(s - m_new)
        l_scr[...] = alpha * l_scr[...] + jnp.sum(p, axis=1, keepdims=True)
        acc_scr[...] = alpha * acc_scr[...] + _dot(p.astype(BF16), vals)
        m_scr[...] = m_new

    lc = lc_ref[...]
    x = jnp.concatenate([lc[:, b * SUFFIX_BLOCK:(b + 1) * SUFFIX_BLOCK] for b in range(nblk)], axis=0)
    jj = lax.broadcasted_iota(jnp.int32, (SUFFIX_BLOCK, SUFFIX_BLOCK), 0)
    ss = lax.broadcasted_iota(jnp.int32, (SUFFIX_BLOCK, SUFFIX_BLOCK), 1)
    excl = _dot_exact01(jnp.where(jj > ss, 1.0, 0.0).astype(BF16), x, nt=True)
    off = tail_scr[...]
    pieces = [None] * nblk
    for b in range(nblk - 1, -1, -1):
        eb = excl[b * N_HEADS:(b + 1) * N_HEADS, :]
        pieces[b] = eb + off
        off = off + (eb[:, 0:1] + x[b * N_HEADS:(b + 1) * N_HEADS, 0:1])
    tail_scr[...] = off
    suffix = jnp.concatenate(pieces, axis=1)

    def all_heads(ref):
        return jnp.concatenate([ref[pl.ds(h, kc, stride=N_HEADS), :].astype(BF16)
                                for h in range(N_HEADS)], axis=1)

    s = _dot_nt(qbd_scr[...], all_heads(ck_ref))
    s = (s.reshape(t_new, N_HEADS, kc) + cq.reshape(t_new, N_HEADS, 1) + suffix[None]).reshape(nrow, kc)
    online(s, all_heads(cv_ref))

    @pl.when(c == nc - 1)
    def _():
        sn = _dot_nt(qbd_scr[...], kn_ref[...])
        sn = sn.reshape(t_new, N_HEADS, t_new) + cq.reshape(t_new, N_HEADS, 1) + (-cn)[None]
        tq = lax.broadcasted_iota(jnp.int32, (t_new, N_HEADS, t_new), 0)
        tk = lax.broadcasted_iota(jnp.int32, (t_new, N_HEADS, t_new), 2)
        sn = jnp.where(tk <= tq, sn, NEG_INF).reshape(nrow, t_new)
        online(sn, vn_ref[...])
        o = (acc_scr[...] / l_scr[...]).reshape(t_new, N_HEADS, ATTN_DIM)
        o_ref[...] = jnp.sum(jnp.where(head_mask[None], o, 0.0), axis=1).astype(BF16)


def _fox_sample(z, cn_t, cache_k, cache_v, lc_t, layer, t_new):
    nb = cache_k.shape[1]
    past = cache_k.shape[2] // N_HEADS
    nc = past // KV_CHUNK
    nrow = t_new * N_HEADS
    qb, kb, vb = Q_OFF // ATTN_DIM, K_OFF // ATTN_DIM, V_OFF // ATTN_DIM
    cache_spec = pl.BlockSpec((None, None, KV_CHUNK * N_HEADS, HEAD_DIM),
                              lambda b, c: (layer, b, nc - 1 - c, 0))
    return pl.pallas_call(
        _fox_sample_body,
        grid=(nb, nc),
        in_specs=[
            pl.BlockSpec((t_new, ATTN_DIM), lambda b, c: (b, qb)),
            pl.BlockSpec((t_new, ATTN_DIM), lambda b, c: (b, kb)),
            pl.BlockSpec((t_new, ATTN_DIM), lambda b, c: (b, vb)),
            pl.BlockSpec((None, N_HEADS, t_new), lambda b, c: (b, 0, 0)),
            pl.BlockSpec((None, None, N_HEADS, KV_CHUNK), lambda b, c: (layer, b, 0, nc - 1 - c)),
            cache_spec, cache_spec,
        ],
        out_specs=pl.BlockSpec((t_new, ATTN_DIM), lambda b, c: (b, 0)),
        out_shape=jax.ShapeDtypeStruct((nb * t_new, ATTN_DIM), BF16),
        scratch_shapes=[pltpu.VMEM((nrow, ATTN_DIM), BF16), pltpu.VMEM((nrow, 1), F32),
                        pltpu.VMEM((nrow, 1), F32), pltpu.VMEM((nrow, ATTN_DIM), F32),
                        pltpu.VMEM((N_HEADS, 1), F32)],
        compiler_params=_params(2),
        name="fox_sample",
    )(z, z, z, cn_t, lc_t, cache_k, cache_v)


def _mix_body(*refs, seg, has_hist):
    zc_ref, ya_ref, gc_ref, ga_ref, wc_ref, wpc_ref, wpa_ref = refs[:7]
    n_in = 8 if has_hist else 7
    mg_ref, st_ref = refs[n_in:n_in + 2]
    i = pl.program_id(0)
    tm = zc_ref.shape[0]
    xc = zc_ref[:, 0:CONV_DIM].astype(F32)
    bc = zc_ref[:, CONV_DIM:2 * CONV_DIM].astype(F32)
    cc = zc_ref[:, 2 * CONV_DIM:3 * CONV_DIM].astype(F32)
    u = cc * xc
    if has_hist:
        hist = refs[7][...]
        h0 = _seg_rows(hist[:, 0, :], seg)
        h1 = _seg_rows(hist[:, 1, :], seg)
        nseg = tm // seg
        st_ref[...] = u.reshape(nseg, seg, CONV_DIM)[:, seg - 2:seg, :]
    else:
        carry_scr = refs[n_in + 2]

        @pl.when(i == 0)
        def _():
            carry_scr[...] = jnp.zeros_like(carry_scr)
        h0 = carry_scr[0:1, :]
        h1 = carry_scr[1:2, :]
        st_ref[...] = u[tm - 2:tm, :][None]
    yc = (bc * _conv3(u, h0, h1, wc_ref[...], seg)).astype(BF16)
    if not has_hist:
        carry_scr[0:2, :] = u[tm - 2:tm, :]
    t1 = _dot(yc, wpc_ref[...])
    t2 = _dot(ya_ref[...], wpa_ref[...])
    mg_ref[...] = (gc_ref[...].astype(F32) * t1 + ga_ref[...].astype(F32) * t2).astype(BF16)


def _mix(z, ya, w_conv, w_pc, w_pa, layer, tm, seg, hist):
    rows = z.shape[0]
    nm = rows // tm
    has_hist = hist is not None
    nseg = tm // seg if has_hist else 1
    in_specs = [
        pl.BlockSpec((tm, 3 * CONV_DIM), lambda i: (i, 0)),
        pl.BlockSpec((tm, ATTN_DIM), lambda i: (i, 0)),
        pl.BlockSpec((tm, D_MODEL), lambda i: (i, G_OFF // D_MODEL)),
        pl.BlockSpec((tm, D_MODEL), lambda i: (i, G_OFF // D_MODEL + 1)),
        pl.BlockSpec((None, 3, CONV_DIM), lambda i: (layer, 0, 0)),
        pl.BlockSpec((None, CONV_DIM, D_MODEL), lambda i: (layer, 0, 0)),
        pl.BlockSpec((None, ATTN_DIM, D_MODEL), lambda i: (layer, 0, 0)),
    ]
    args = [z, ya, z, z, w_conv, w_pc, w_pa]
    scratch = []
    if has_hist:
        in_specs.append(pl.BlockSpec((nseg, 2, CONV_DIM), lambda i: (0, 0, 0)))
        args.append(hist)
    else:
        scratch.append(pltpu.VMEM((8, CONV_DIM), F32))
    return pl.pallas_call(
        functools.partial(_mix_body, seg=seg, has_hist=has_hist),
        grid=(nm,),
        in_specs=in_specs,
        out_specs=(pl.BlockSpec((tm, D_MODEL), lambda i: (i, 0)),
                   pl.BlockSpec((nseg, 2, CONV_DIM), lambda i: (0, 0, 0))),
        out_shape=(jax.ShapeDtypeStruct((rows, D_MODEL), BF16),
                   jax.ShapeDtypeStruct((nseg, 2, CONV_DIM), F32)),
        scratch_shapes=scratch,
        compiler_params=_params(1),
        name="mix",
    )(*args)


def _oproj_body(x_ref, mg_ref, wo_ref, *rest):
    o_ref = rest[-1]
    x = _rows_after_meta(x_ref, rest[0], pl.program_id(0)) if len(rest) == 2 else x_ref[...]
    o_ref[...] = x + _dot(mg_ref[...], wo_ref[...])


def _oproj(x, meta, mg, w_o, layer, tm):
    has_meta = meta is not None
    rows = mg.shape[0]
    in_specs = [_meta_row_spec(tm) if has_meta else pl.BlockSpec((tm, D_MODEL), lambda i: (i, 0)),
                pl.BlockSpec((tm, D_MODEL), lambda i: (i, 0)),
                pl.BlockSpec((None, D_MODEL, D_MODEL), lambda i: (layer, 0, 0))]
    args = [x, mg, w_o]
    if has_meta:
        in_specs.append(pl.BlockSpec((N_META, D_MODEL), lambda i: (0, 0)))
        args.append(meta)
    return pl.pallas_call(
        _oproj_body,
        grid=(rows // tm,),
        in_specs=in_specs,
        out_specs=pl.BlockSpec((tm, D_MODEL), lambda i: (i, 0)),
        out_shape=jax.ShapeDtypeStruct((rows, D_MODEL), F32),
        compiler_params=_params(1),
        name="oproj",
    )(*args)


def _ffn_body(*refs, seg, has_hist):
    (x_ref, g_ref, wua_ref, wub_ref, wca_ref, wcb_ref, ba_ref, bb_ref, wd_ref) = refs[:9]
    n_in = 11 if has_hist else 9
    o_ref, st_ref, h_scr = refs[n_in:n_in + 3]
    i = pl.program_id(0)
    j = pl.program_id(1)
    tm = x_ref.shape[0]
    tf = wua_ref.shape[1]

    @pl.when(j == 0)
    def _():
        x = x_ref[...]
        h_scr[...] = _rmsnorm(x, g_ref[...]).astype(BF16)
        o_ref[...] = x

    rc = ROW_CHUNK if (not has_hist and tm % ROW_CHUNK == 0) else tm
    chunks = list(range(0, tm, rc))

    def up(r0):
        hb = h_scr[r0:r0 + rc, :]
        return _dot(hb, wua_ref[...]), _dot(hb, wub_ref[...])

    def gate(u, hist, w_ref, bias_ref):
        return _conv3(u, hist[0], hist[1], w_ref[...], min(seg, rc)) + bias_ref[...]

    def down(r0, gated):
        o_ref[r0:r0 + rc, :] += _dot(gated, wd_ref[...])

    if has_hist:
        nseg = tm // seg
        hist_a, hist_b = ((_seg_rows(r[:, 0, :], seg), _seg_rows(r[:, 1, :], seg))
                          for r in (refs[9][...], refs[10][...]))
    else:
        ca_scr, cb_scr = refs[n_in + 3:n_in + 5]

        @pl.when(i == 0)
        def _():
            ca_scr[j] = jnp.zeros((8, tf), F32)
            cb_scr[j] = jnp.zeros((8, tf), F32)
        hist_a = (ca_scr[j, 0:1, :], ca_scr[j, 1:2, :])
        hist_b = (cb_scr[j, 0:1, :], cb_scr[j, 1:2, :])

    u_next = up(chunks[0])
    pending = None
    for idx, r0 in enumerate(chunks):
        ua, ub = u_next
        if idx + 1 < len(chunks):
            u_next = up(chunks[idx + 1])
        a = gate(ua, hist_a, wca_ref, ba_ref)
        b = gate(ub, hist_b, wcb_ref, bb_ref)
        gated = ((a * _sigmoid(a)) * b).astype(BF16)
        hist_a = (ua[rc - 2:rc - 1, :], ua[rc - 1:rc, :])
        hist_b = (ub[rc - 2:rc - 1, :], ub[rc - 1:rc, :])
        if pending is not None:
            down(*pending)
        pending = (r0, gated)
    down(*pending)

    if has_hist:
        st_ref[:, :, 0, :] = ua.reshape(nseg, seg, tf)[:, seg - 2:seg, :]
        st_ref[:, :, 1, :] = ub.reshape(nseg, seg, tf)[:, seg - 2:seg, :]
    else:
        st_ref[0, :, 0, :] = ua[rc - 2:rc, :]
        st_ref[0, :, 1, :] = ub[rc - 2:rc, :]
        ca_scr[j, 0:2, :] = ua[rc - 2:rc, :]
        cb_scr[j, 0:2, :] = ub[rc - 2:rc, :]


def _ffn(x, g, w_up, w_fconv, b_fconv, w_down, layer, tm, seg, hist):
    rows = x.shape[0]
    nm = rows // tm
    nf = D_FF // TF
    has_hist = hist is not None
    nseg = tm // seg if has_hist else 1
    in_specs = [
        pl.BlockSpec((tm, D_MODEL), lambda i, j: (i, 0)),
        pl.BlockSpec((None, 1, D_MODEL), lambda i, j: (layer, 0, 0)),
        pl.BlockSpec((None, D_MODEL, TF), lambda i, j: (layer, 0, j)),
        pl.BlockSpec((None, D_MODEL, TF), lambda i, j: (layer, 0, nf + j)),
        pl.BlockSpec((None, 3, TF), lambda i, j: (layer, 0, j)),
        pl.BlockSpec((None, 3, TF), lambda i, j: (layer, 0, nf + j)),
        pl.BlockSpec((None, 1, TF), lambda i, j: (layer, 0, j)),
        pl.BlockSpec((None, 1, TF), lambda i, j: (layer, 0, nf + j)),
        pl.BlockSpec((None, TF, D_MODEL), lambda i, j: (layer, j, 0)),
    ]
    args = [x, g, w_up, w_up, w_fconv, w_fconv, b_fconv, b_fconv, w_down]
    scratch = [pltpu.VMEM((tm, D_MODEL), BF16)]
    if has_hist:
        in_specs += [pl.BlockSpec((nseg, 2, TF), lambda i, j: (0, 0, j)),
                     pl.BlockSpec((nseg, 2, TF), lambda i, j: (0, 0, nf + j))]
        args += [hist, hist]
    else:
        scratch += [pltpu.VMEM((nf, 8, TF), F32), pltpu.VMEM((nf, 8, TF), F32)]
    return pl.pallas_call(
        functools.partial(_ffn_body, seg=seg, has_hist=has_hist),
        grid=(nm, nf),
        in_specs=in_specs,
        out_specs=(pl.BlockSpec((tm, D_MODEL), lambda i, j: (i, 0)),
                   pl.BlockSpec((None, nseg, 2, 2, TF), lambda i, j: (i, 0, 0, 0, j))),
        out_shape=(jax.ShapeDtypeStruct((rows, D_MODEL), F32),
                   jax.ShapeDtypeStruct((nm, nseg, 2, 2, D_FF), F32)),
        scratch_shapes=scratch,
        compiler_params=_params(2),
        name="ffn",
    )(*args)


def _final_norm_body(x_ref, g_ref, o_ref):
    o_ref[...] = _rmsnorm(x_ref[...], g_ref[...])


def _final_norm(x, g, tm, skip_rows):
    rows = x.shape[0] - skip_rows
    return pl.pallas_call(
        _final_norm_body,
        grid=(rows // tm,),
        in_specs=[pl.BlockSpec((pl.Element(tm), pl.Element(D_MODEL)),
                               lambda i: (pl.multiple_of(i * tm + skip_rows, 8), 0)),
                  pl.BlockSpec((1, D_MODEL), lambda i: (0, 0))],
        out_specs=pl.BlockSpec((tm, D_MODEL), lambda i: (i, 0)),
        out_shape=jax.ShapeDtypeStruct((rows, D_MODEL), F32),
        compiler_params=_params(1),
        name="final_norm",
    )(x, g)


def kernel(x_prompt, x_sample, cache_k, cache_v, cache_logf, state_conv, state_ffn_conv, meta_tokens,
           g_mix, w_in, b_in, w_conv, w_pc, w_pa, w_o, g_ffn, w_up, w_fconv, b_fconv, w_down, g_final):
    depth = w_in.shape[0]
    batch, seq, _ = x_prompt.shape
    nb, t_new, _ = x_sample.shape
    past = cache_k.shape[2]
    lp = N_META + seq
    assert batch == 1 and past % KV_CHUNK == 0 and lp % TM_BIG == 0 and lp % TM_SMALL == 0
    rows_s = nb * t_new

    w_left = jnp.swapaxes(w_in, 1, 2).astype(BF16)
    w_gate = w_left[:, F_OFF + N_HEADS:]
    w_f = jnp.pad(w_left[:, F_OFF:F_OFF + N_HEADS], ((0, 0), (0, LANES - N_HEADS), (0, 0)))
    b_main = jnp.concatenate([b_in[:, :F_OFF], b_in[:, F_OFF + N_HEADS:]], axis=1)[:, None, :]
    b_f = jnp.pad(b_in[:, F_OFF:F_OFF + N_HEADS], ((0, 0), (0, LANES - N_HEADS)))[:, None, :]
    w_pc_b, w_pa_b, w_o_b = w_pc.astype(BF16), w_pa.astype(BF16), w_o.astype(BF16)
    w_up_b, w_down_b = w_up.astype(BF16), w_down.astype(BF16)
    g_mix3, g_ffn3, b_fconv3 = g_mix[:, None, :], g_ffn[:, None, :], b_fconv[:, None, :]

    xp, meta = x_prompt[0], meta_tokens.astype(F32)
    xs = x_sample.reshape(rows_s, D_MODEL)
    ck = cache_k.reshape(depth, nb, past * N_HEADS, HEAD_DIM)
    cv = cache_v.reshape(depth, nb, past * N_HEADS, HEAD_DIM)
    lc_t = jnp.swapaxes(cache_logf.astype(F32), 2, 3)

    k_p, v_p, k_s, v_s = [], [], [], []
    lf_p, lf_s, cst_p, cst_s, fst_p, fst_s = [], [], [], [], [], []
    for l in range(depth):
        z, k_new, v_new, lf, _, crow, _ = _inproj(xs, None, g_mix3, w_left, w_gate, b_main, w_f, b_f, l,
                                                  rows_s, rows_s, t_new, False)
        k_s.append(k_new); v_s.append(v_new)
        cn_t = jnp.swapaxes(crow[0].reshape(N_HEADS, nb, t_new), 0, 1)
        ya = _fox_sample(z, cn_t, ck, cv, lc_t, l, t_new)
        mg, cst = _mix(z, ya, w_conv, w_pc_b, w_pa_b, l, rows_s, t_new, state_conv[l])
        x1 = _oproj(xs, None, mg, w_o_b, l, rows_s)
        xs, fst = _ffn(x1, g_ffn3, w_up_b, w_fconv, b_fconv3, w_down_b, l, rows_s, t_new,
                       state_ffn_conv[l])
        lf_s.append(lf); cst_s.append(cst); fst_s.append(fst[-1].reshape(nb, 2, 2 * D_FF))

    xp, xs = lax.optimization_barrier((xp, xs))
    for l in range(depth):
        z, k_new, v_new, lf, ccol, crow, plan = _inproj(xp, meta, g_mix3, w_left, w_gate, b_main, w_f, b_f,
                                                         l, lp, TM_BIG, TM_BIG, True)
        k_p.append(k_new); v_p.append(v_new)
        if l < depth - 1:
            ya = _fox_prompt(z, ccol, crow, plan, TM_BIG)
        else:
            ya, k_prompt, v_prompt = _fox_prompt(z, ccol, crow, plan, TM_BIG,
                                                 [a for kv in zip(k_p, v_p) for a in kv])
        mg, cst = _mix(z, ya, w_conv, w_pc_b, w_pa_b, l, TM_SMALL, TM_SMALL, None)
        x1 = _oproj(xp, meta, mg, w_o_b, l, TM_SMALL)
        xp, fst = _ffn(x1, g_ffn3, w_up_b, w_fconv, b_fconv3, w_down_b, l, TM_BIG, TM_BIG, None)
        meta = None
        lf_p.append(lf); cst_p.append(cst); fst_p.append(fst[-1].reshape(1, 2, 2 * D_FF))

    g_fin = g_final[None, :]
    y_prompt = _final_norm(xp, g_fin, TM_OUT, N_META)[None]
    y_sample = _final_norm(xs, g_fin, rows_s, 0).reshape(nb, t_new, D_MODEL)
    return (y_prompt, y_sample,
            k_prompt.reshape(depth, 1, lp, N_HEADS, HEAD_DIM),
            v_prompt.reshape(depth, 1, lp, N_HEADS, HEAD_DIM),
            jnp.stack(lf_p).reshape(depth, 1, lp, N_HEADS),
            jnp.stack(cst_p),
            jnp.stack(fst_p),
            jnp.stack(k_s).reshape(depth, nb, t_new, N_HEADS, HEAD_DIM),
            jnp.stack(v_s).reshape(depth, nb, t_new, N_HEADS, HEAD_DIM),
            jnp.stack(lf_s).reshape(depth, nb, t_new, N_HEADS),
            jnp.stack(cst_s),
            jnp.stack(fst_s))
```
